```python
import jax, jax.numpy as jnp
from jax import lax
import numpy as np

D_MODEL = 2048
BATCH = 2
SEQ = 16384
DEPTH = 2
DEC_BATCH = 8
DEC_SEQ = 4096
PAST_LEN = 128

GRID_W = 64
ROPE_THETA = 10000.0
CHUNK = 128
HEAD_DIM = 128
NORM_EPS = 1e-6

SGU_WIDTH = 1024
SGU_GROUPS = 4
ATTN_HEADS = 8
ATTN_KV_HEADS = 2
ATTN_Q_W = ATTN_HEADS * HEAD_DIM
ATTN_KV_W = ATTN_KV_HEADS * HEAD_DIM
RET_HEADS = 4
RET_QK_DIM = HEAD_DIM
RET_V_DIM = 256
RET_QK_W = RET_HEADS * RET_QK_DIM
RET_V_W = RET_HEADS * RET_V_DIM

IN_SPLITS = (SGU_WIDTH, SGU_WIDTH, ATTN_Q_W, ATTN_KV_W, ATTN_KV_W, RET_QK_W, RET_QK_W, RET_V_W, RET_V_W, D_MODEL, D_MODEL, D_MODEL)
IN_WIDTH = 2 * SGU_WIDTH + ATTN_Q_W + 2 * ATTN_KV_W + 2 * RET_QK_W + 2 * RET_V_W + 3 * D_MODEL

MEM_LEN = 256
XATTN_HEADS = 4
XATTN_HEAD_DIM = 128
XATTN_W = XATTN_HEADS * XATTN_HEAD_DIM

N_EXPERT_GROUPS = 4
EXPERTS_PER_GROUP = 4
N_EXPERTS = N_EXPERT_GROUPS * EXPERTS_PER_GROUP
TOP_K_INNER = 2
EXPERT_FF = 512

kernel_name = 'hybrid_bidir_gated_encoder'

F32 = jnp.float32


def rmsnorm(x, g):
    xf = x.astype(F32)
    y = xf * lax.rsqrt(jnp.mean(xf * xf, axis=-1, keepdims=True) + NORM_EPS)
    return (y * g).astype(x.dtype)


def layernorm(x, g):
    xf = x.astype(F32)
    mu = jnp.mean(xf, axis=-1, keepdims=True)
    xc = xf - mu
    y = xc * lax.rsqrt(jnp.mean(xc * xc, axis=-1, keepdims=True) + NORM_EPS)
    return (y * g).astype(x.dtype)


def axial_rope_tables(seq_len):
    rows = seq_len // GRID_W
    row = jnp.repeat(jnp.arange(rows, dtype=F32), GRID_W)
    col = jnp.tile(jnp.arange(GRID_W, dtype=F32), rows)
    n_freq = HEAD_DIM // 4
    inv = ROPE_THETA ** (-jnp.arange(n_freq, dtype=F32) / n_freq)
    ang = jnp.stack([row[:, None] * inv, col[:, None] * inv], axis=1)
    return jnp.cos(ang), jnp.sin(ang)


def apply_axial_rope(x, cos, sin):
    B, S, H, dh = x.shape
    xr = x.astype(F32).reshape(B, S, H, 2, 2, dh // 4)
    x1 = xr[..., 0, :]
    x2 = xr[..., 1, :]
    c = cos[None, :, None]
    s = sin[None, :, None]
    out = jnp.stack([x1 * c - x2 * s, x2 * c + x1 * s], axis=-2)
    return out.reshape(B, S, H, dh).astype(x.dtype)


def spatial_gating_unit(u, v, norm_g, w_s, b_s):
    B, S, _ = v.shape
    n = S // CHUNK
    vn = layernorm(v, norm_g).reshape(B, n, CHUNK, SGU_GROUPS, SGU_WIDTH // SGU_GROUPS)
    s = jnp.einsum('gij,bnjgc->bnigc', w_s, vn) + jnp.transpose(b_s)[None, None, :, :, None]
    return u * s.reshape(B, S, SGU_WIDTH)


def axial_gqa_attention(q, k, v, q_norm_g, k_norm_g, cos, sin):
    q = apply_axial_rope(rmsnorm(q, q_norm_g), cos, sin)
    k = apply_axial_rope(rmsnorm(k, k_norm_g), cos, sin)
    B, S, H, dh = q.shape
    rep = H // ATTN_KV_HEADS
    nb = S // CHUNK
    scale = dh ** -0.5
    qb = q.reshape(B, nb, CHUNK, ATTN_KV_HEADS, rep, dh).transpose(1, 0, 3, 4, 2, 5)

    def block(qi):
        s = jnp.einsum('bgrqd,bkgd->bgrqk', qi, k).astype(F32) * scale
        p = jax.nn.softmax(s, axis=-1).astype(v.dtype)
        return jnp.einsum('bgrqk,bkgd->bgrqd', p, v)

    o = lax.map(block, qb)
    return o.transpose(1, 0, 4, 2, 3, 5).reshape(B, S, H * dh)


def _to_chunks(t):
    B, S, H, d = t.shape
    return t.reshape(B, S // CHUNK, CHUNK, H, d).transpose(0, 3, 1, 2, 4)


def _from_chunks(t):
    B, H, n, C, d = t.shape
    return t.transpose(0, 2, 3, 1, 4).reshape(B, n * C, H, d)


def _retention_one_direction(q, k, v, log_gamma, include_diag):
    B, H, n, C, dk = q.shape
    dv = v.shape[-1]
    idx = jnp.arange(C, dtype=F32)
    diff = idx[:, None] - idx[None, :]
    mask = (diff >= 0) if include_diag else (diff > 0)
    lg = log_gamma[:, None, None]
    dmat = jnp.where(mask, jnp.exp(lg * jnp.where(mask, diff, 0.0)), 0.0)
    scores = jnp.einsum('bhnid,bhnjd->bhnij', q, k) * dmat[None, :, None]
    intra = jnp.einsum('bhnij,bhnje->bhnie', scores, v)
    kdec = jnp.exp(log_gamma[:, None] * (C - 1 - idx))
    u = jnp.einsum('bhnjd,bhnje->bhnde', k * kdec[None, :, None, :, None], v)
    chunk_decay = jnp.exp(log_gamma * C)[None, :, None, None]

    def step(state, u_n):
        return chunk_decay * state + u_n, state

    _, states = lax.scan(step, jnp.zeros((B, H, dk, dv), F32), jnp.moveaxis(u, 2, 0))
    qdec = jnp.exp(log_gamma[:, None] * (idx + 1.0))
    inter = jnp.einsum('bhnid,nbhde->bhnie', q * qdec[None, :, None, :, None], states)
    return intra + inter


def bidirectional_retention(q, k, v, g, ret_decay, norm_g, cos, sin):
    B, S, H, _ = q.shape
    dt = v.dtype
    q = apply_axial_rope(q, cos, sin).astype(F32) * (RET_QK_DIM ** -0.5)
    k = apply_axial_rope(k, cos, sin).astype(F32)
    v = v.astype(F32)
    log_gamma = -jnp.exp(ret_decay.astype(F32))
    fwd = _retention_one_direction(_to_chunks(q), _to_chunks(k), _to_chunks(v), log_gamma[0], True)
    bwd = _retention_one_direction(_to_chunks(q[:, ::-1]), _to_chunks(k[:, ::-1]), _to_chunks(v[:, ::-1]), log_gamma[1], False)
    o = _from_chunks(fwd) + _from_chunks(bwd)[:, ::-1]
    mu = jnp.mean(o, axis=-1, keepdims=True)
    oc = o - mu
    o = oc * lax.rsqrt(jnp.mean(oc * oc, axis=-1, keepdims=True) + NORM_EPS)
    o = o.reshape(B, S, H * RET_V_DIM) * norm_g
    return (o * jax.nn.silu(g.astype(F32))).astype(dt)


def parallel_mixers(h, cos, sin, w_in, sgu_norm_g, sgu_w, sgu_b, q_norm_g, k_norm_g, ret_decay, ret_norm_g, w_a, w_b, w_c, w_out):
    B, S, _ = h.shape
    z = h @ w_in
    offsets = np.cumsum(IN_SPLITS)[:-1].tolist()
    a_u, a_v, b_q, b_k, b_v, c_q, c_k, c_v, c_g, gate_a, gate_b, gate_c = jnp.split(z, offsets, axis=-1)
    out_a = spatial_gating_unit(jax.nn.gelu(a_u, approximate=False), jax.nn.gelu(a_v, approximate=False), sgu_norm_g, sgu_w, sgu_b)
    out_b = axial_gqa_attention(b_q.reshape(B, S, ATTN_HEADS, HEAD_DIM), b_k.reshape(B, S, ATTN_KV_HEADS, HEAD_DIM), b_v.reshape(B, S, ATTN_KV_HEADS, HEAD_DIM), q_norm_g, k_norm_g, cos, sin)
    out_c = bidirectional_retention(c_q.reshape(B, S, RET_HEADS, RET_QK_DIM), c_k.reshape(B, S, RET_HEADS, RET_QK_DIM), c_v.reshape(B, S, RET_HEADS, RET_V_DIM), c_g, ret_decay, ret_norm_g, cos, sin)
    merged = (jax.nn.sigmoid(gate_a) * (out_a @ w_a)
              + jax.nn.sigmoid(gate_b) * (out_b @ w_b)
              + jax.nn.sigmoid(gate_c) * (out_c @ w_c))
    return merged @ w_out


def memory_cross_attention(h, mem_n, wq, wkv, wo):
    B, S, _ = h.shape
    M = mem_n.shape[1]
    q = (h @ wq).reshape(B, S, XATTN_HEADS, XATTN_HEAD_DIM)
    kv = (mem_n @ wkv).reshape(B, M, 2, XATTN_HEADS, XATTN_HEAD_DIM)
    k = kv[:, :, 0]
    v = kv[:, :, 1]
    s = jnp.einsum('bqhd,bkhd->bhqk', q, k).astype(F32) * (XATTN_HEAD_DIM ** -0.5)
    p = jax.nn.softmax(s, axis=-1).astype(v.dtype)
    o = jnp.einsum('bhqk,bkhd->bqhd', p, v).reshape(B, S, XATTN_W)
    return o @ wo


def hierarchical_moe(h, wg, bg, we, be, w_gate, w_up, w_down):
    B, S, D = h.shape
    T = B * S
    t = h.reshape(T, D)
    group_probs = jax.nn.softmax((t @ wg).astype(F32) + bg, axis=-1)
    g_prob, g_idx = lax.top_k(group_probs, 1)
    expert_logits = (t @ we).astype(F32).reshape(T, N_EXPERT_GROUPS, EXPERTS_PER_GROUP) + be
    sel_logits = jnp.take_along_axis(expert_logits, g_idx[:, :, None], axis=1)[:, 0]
    top_logits, top_idx = lax.top_k(sel_logits, TOP_K_INNER)
    weights = g_prob * jax.nn.softmax(top_logits, axis=-1)
    expert_ids = g_idx * EXPERTS_PER_GROUP + top_idx
    combine = jnp.sum(jax.nn.one_hot(expert_ids, N_EXPERTS, dtype=F32) * weights[..., None], axis=1)

    def expert_step(acc, xs):
        wg_e, wu_e, wd_e, c_e = xs
        hid = jax.nn.silu(t @ wg_e) * (t @ wu_e)
        return acc + (hid @ wd_e) * c_e[:, None], None

    acc, _ = lax.scan(expert_step, jnp.zeros_like(t), (w_gate, w_up, w_down, jnp.transpose(combine).astype(t.dtype)))
    return acc.reshape(B, S, D)


def trunk(x, mem, params, final_norm_g):
    (norm_mix_g, w_in, sgu_norm_g, sgu_w, sgu_b, attn_q_norm_g, attn_k_norm_g, ret_decay, ret_norm_g,
     w_branch_a, w_branch_b, w_branch_c, w_out, xattn_norm_g, mem_norm_g, xattn_wq, xattn_wkv, xattn_wo,
     ffn_norm_g, router_group_w, router_group_b, router_expert_w, router_expert_b,
     expert_w_gate, expert_w_up, expert_w_down) = params
    cos, sin = axial_rope_tables(x.shape[1])
    for l in range(DEPTH):
        x = x + parallel_mixers(rmsnorm(x, norm_mix_g[l]), cos, sin, w_in[l], sgu_norm_g[l], sgu_w[l], sgu_b[l],
                                attn_q_norm_g[l], attn_k_norm_g[l], ret_decay[l], ret_norm_g[l],
                                w_branch_a[l], w_branch_b[l], w_branch_c[l], w_out[l])
        x = x + memory_cross_attention(rmsnorm(x, xattn_norm_g[l]), rmsnorm(mem, mem_norm_g[l]),
                                       xattn_wq[l], xattn_wkv[l], xattn_wo[l])
        x = x + hierarchical_moe(rmsnorm(x, ffn_norm_g[l]), router_group_w[l], router_group_b[l],
                                 router_expert_w[l], router_expert_b[l],
                                 expert_w_gate[l], expert_w_up[l], expert_w_down[l])
    return rmsnorm(x, final_norm_g)


def setup_inputs(seed: int = 0) -> dict:
    key = jax.random.key(seed)
    ks = iter(jax.random.split(key, 48))

    def nrm(shape, scale):
        return jax.random.normal(next(ks), shape, F32) * scale

    L, D = DEPTH, D_MODEL
    base_decay = jnp.log(-jnp.log1p(-(2.0 ** (-5.0 - jnp.arange(RET_HEADS, dtype=F32)))))
    return {
        'x_prompt': nrm((BATCH, SEQ, D), 1.0),
        'x_sample': nrm((DEC_BATCH, DEC_SEQ, D), 1.0),
        'mem_prompt': nrm((BATCH, MEM_LEN, D), 1.0),
        'mem_sample': nrm((DEC_BATCH, MEM_LEN, D), 1.0),
        'norm_mix_g': 1.0 + nrm((L, D), 0.02),
        'w_in': nrm((L, D, IN_WIDTH), D ** -0.5),
        'sgu_norm_g': 1.0 + nrm((L, SGU_WIDTH), 0.02),
        'sgu_w': nrm((L, SGU_GROUPS, CHUNK, CHUNK), CHUNK ** -0.5),
        'sgu_b': 1.0 + nrm((L, SGU_GROUPS, CHUNK), 0.02),
        'attn_q_norm_g': 1.0 + nrm((L, HEAD_DIM), 0.02),
        'attn_k_norm_g': 1.0 + nrm((L, HEAD_DIM), 0.02),
        'ret_decay': base_decay[None, None, :] + nrm((L, 2, RET_HEADS), 0.05),
        'ret_norm_g': 1.0 + nrm((L, RET_V_W), 0.02),
        'w_branch_a': nrm((L, SGU_WIDTH, D), SGU_WIDTH ** -0.5),
        'w_branch_b': nrm((L, ATTN_Q_W, D), ATTN_Q_W ** -0.5),
        'w_branch_c': nrm((L, RET_V_W, D), RET_V_W ** -0.5),
        'w_out': nrm((L, D, D), D ** -0.5),
        'xattn_norm_g': 1.0 + nrm((L, D), 0.02),
        'mem_norm_g': 1.0 + nrm((L, D), 0.02),
        'xattn_wq': nrm((L, D, XATTN_W), D ** -0.5),
        'xattn_wkv': nrm((L, D, 2 * XATTN_W), D ** -0.5),
        'xattn_wo': nrm((L, XATTN_W, D), XATTN_W ** -0.5),
        'ffn_norm_g': 1.0 + nrm((L, D), 0.02),
        'router_group_w': nrm((L, D, N_EXPERT_GROUPS), D ** -0.5),
        'router_group_b': nrm((L, N_EXPERT_GROUPS), 0.01),
        'router_expert_w': nrm((L, D, N_EXPERTS), D ** -0.5),
        'router_expert_b': nrm((L, N_EXPERT_GROUPS, EXPERTS_PER_GROUP), 0.01),
        'expert_w_gate': nrm((L, N_EXPERTS, D, EXPERT_FF), D ** -0.5),
        'expert_w_up': nrm((L, N_EXPERTS, D, EXPERT_FF), D ** -0.5),
        'expert_w_down': nrm((L, N_EXPERTS, EXPERT_FF, D), EXPERT_FF ** -0.5),
        'final_norm_g': 1.0 + nrm((D,), 0.02),
    }


def reference(x_prompt, x_sample, mem_prompt, mem_sample, norm_mix_g, w_in, sgu_norm_g, sgu_w, sgu_b,
              attn_q_norm_g, attn_k_norm_g, ret_decay, ret_norm_g, w_branch_a, w_branch_b, w_branch_c, w_out,
              xattn_norm_g, mem_norm_g, xattn_wq, xattn_wkv, xattn_wo, ffn_norm_g,
              router_group_w, router_group_b, router_expert_w, router_expert_b,
              expert_w_gate, expert_w_up, expert_w_down, final_norm_g):
    params = (norm_mix_g, w_in, sgu_norm_g, sgu_w, sgu_b, attn_q_norm_g, attn_k_norm_g, ret_decay, ret_norm_g,
              w_branch_a, w_branch_b, w_branch_c, w_out, xattn_norm_g, mem_norm_g, xattn_wq, xattn_wkv, xattn_wo,
              ffn_norm_g, router_group_w, router_group_b, router_expert_w, router_expert_b,
              expert_w_gate, expert_w_up, expert_w_down)
    y_prompt = trunk(x_prompt, mem_prompt, params, final_norm_g)
    y_sample = trunk(x_sample, mem_sample, params, final_norm_g)
    return (y_prompt, y_sample)
```

```python
import functools

import jax
import jax.numpy as jnp
import numpy as np
from jax import lax
from jax.experimental import pallas as pl
from jax.experimental.pallas import tpu as pltpu

F32 = jnp.float32
BF16 = jnp.bfloat16

D_MODEL = 2048
DEPTH = 2
GRID_W = 64
ROPE_THETA = 10000.0
CHUNK = 128
HEAD_DIM = 128
NORM_EPS = 1e-6
SGU_WIDTH = 1024
SGU_GROUPS = 4
SGU_GROUP_W = SGU_WIDTH // SGU_GROUPS
ATTN_HEADS = 8
ATTN_KV_HEADS = 2
ATTN_REP = ATTN_HEADS // ATTN_KV_HEADS
ATTN_Q_W = ATTN_HEADS * HEAD_DIM
ATTN_KV_W = ATTN_KV_HEADS * HEAD_DIM
RET_HEADS = 4
RET_V_DIM = 256
RET_QK_W = RET_HEADS * HEAD_DIM
RET_V_W = RET_HEADS * RET_V_DIM
XATTN_HEADS = 4
XATTN_W = XATTN_HEADS * HEAD_DIM
N_GROUPS = 4
EXPERTS_PER_GROUP = 4
N_EXPERTS = N_GROUPS * EXPERTS_PER_GROUP
EXPERT_FF = 512
ROUTER_ROWS = 24

OFF_A_U = 0
OFF_A_V = OFF_A_U + SGU_WIDTH
OFF_B_Q = OFF_A_V + SGU_WIDTH
OFF_B_K = OFF_B_Q + ATTN_Q_W
OFF_B_V = OFF_B_K + ATTN_KV_W
OFF_C_Q = OFF_B_V + ATTN_KV_W
OFF_C_K = OFF_C_Q + RET_QK_W
OFF_C_V = OFF_C_K + RET_QK_W
OFF_C_G = OFF_C_V + RET_V_W
OFF_GATE_A = OFF_C_G + RET_V_W
OFF_GATE_B = OFF_GATE_A + D_MODEL
OFF_GATE_C = OFF_GATE_B + D_MODEL
IN_WIDTH = OFF_GATE_C + D_MODEL

V7X_VMEM_LIMIT_BYTES = 56 * 1024 * 1024

_NT = (((1,), (1,)), ((), ()))


def _tile(n, pref):
    t = min(pref, n)
    while n % t:
        t //= 2
    return t


def _params(*sem):
    return pltpu.CompilerParams(dimension_semantics=sem, vmem_limit_bytes=V7X_VMEM_LIMIT_BYTES)


def _rms(x, g):
    return x * lax.rsqrt(jnp.mean(x * x, axis=-1, keepdims=True) + NORM_EPS) * g


def _rms_matmul_kernel(x_ref, g_ref, w_ref, o_ref, h_ref):
    @pl.when(pl.program_id(1) == 0)
    def _():
        h_ref[...] = _rms(x_ref[...], g_ref[...]).astype(BF16)

    o_ref[...] = jnp.dot(h_ref[...], w_ref[...], preferred_element_type=F32).astype(o_ref.dtype)


def _rms_matmul(x, g, w, *, tm, tn):
    t, d = x.shape
    n = w.shape[1]
    tm, tn = _tile(t, tm), _tile(n, tn)
    return pl.pallas_call(
        _rms_matmul_kernel,
        grid=(t // tm, n // tn),
        in_specs=[
            pl.BlockSpec((tm, d), lambda i, j: (i, 0)),
            pl.BlockSpec((1, d), lambda i, j: (0, 0)),
            pl.BlockSpec((d, tn), lambda i, j: (0, j)),
        ],
        out_specs=pl.BlockSpec((tm, tn), lambda i, j: (i, j)),
        out_shape=jax.ShapeDtypeStruct((t, n), BF16),
        scratch_shapes=[pltpu.VMEM((tm, d), BF16)],
        compiler_params=_params("parallel", "arbitrary"),
        name="rms_matmul",
    )(x, g.reshape(1, d), w)


def _gelu(x):
    return 0.5 * x * (1.0 + lax.erf(x * (2.0 ** -0.5)))


def _sgu_kernel(u_ref, v_ref, ng_ref, ws_ref, bs_ref, o_ref, *, n_chunks):
    for c in range(n_chunks):
        rows = slice(c * CHUNK, (c + 1) * CHUNK)
        u = _gelu(u_ref[rows, :].astype(F32))
        v = _gelu(v_ref[rows, :].astype(F32))
        vc = v - jnp.mean(v, axis=-1, keepdims=True)
        vn = vc * lax.rsqrt(jnp.mean(vc * vc, axis=-1, keepdims=True) + NORM_EPS) * ng_ref[...]
        vn = vn.astype(BF16)
        for g in range(SGU_GROUPS):
            cols = slice(g * SGU_GROUP_W, (g + 1) * SGU_GROUP_W)
            s = jnp.dot(ws_ref[g], vn[:, cols], preferred_element_type=F32) + bs_ref[:, g:g + 1]
            o_ref[rows, cols] = (u[:, cols] * s).astype(o_ref.dtype)


def _sgu(z3, norm_g, w_s, b_s_t, *, ts):
    b, s, _ = z3.shape
    ts = _tile(s, ts)
    return pl.pallas_call(
        functools.partial(_sgu_kernel, n_chunks=ts // CHUNK),
        grid=(b, s // ts),
        in_specs=[
            pl.BlockSpec((None, ts, SGU_WIDTH), lambda bi, i: (bi, i, OFF_A_U // SGU_WIDTH)),
            pl.BlockSpec((None, ts, SGU_WIDTH), lambda bi, i: (bi, i, OFF_A_V // SGU_WIDTH)),
            pl.BlockSpec((1, SGU_WIDTH), lambda bi, i: (0, 0)),
            pl.BlockSpec((SGU_GROUPS, CHUNK, CHUNK), lambda bi, i: (0, 0, 0)),
            pl.BlockSpec((CHUNK, SGU_GROUPS), lambda bi, i: (0, 0)),
        ],
        out_specs=pl.BlockSpec((None, ts, SGU_WIDTH), lambda bi, i: (bi, i, 0)),
        out_shape=jax.ShapeDtypeStruct((b, s, SGU_WIDTH), BF16),
        compiler_params=_params("parallel", "parallel"),
        name="sgu",
    )(z3, z3, norm_g.reshape(1, SGU_WIDTH), w_s, b_s_t)


def _rope(x, cos, sin_signed):
    lane = lax.broadcasted_iota(jnp.int32, x.shape, 1)
    first_half = (lane % (HEAD_DIM // 2)) < (HEAD_DIM // 4)
    partner = jnp.where(first_half, pltpu.roll(x, HEAD_DIM - HEAD_DIM // 4, 1), pltpu.roll(x, HEAD_DIM // 4, 1))
    return x * cos + partner * sin_signed


def _prep_kernel(bq_ref, bk_ref, cq_ref, ck_ref, cos_ref, sin_ref, qg_ref, kg_ref,
                 oq_ref, ok_ref, ocq_ref, ock_ref):
    cos = cos_ref[...]
    sin = sin_ref[...]
    scale = HEAD_DIM ** -0.5
    for h in range(ATTN_HEADS):
        cols = slice(h * HEAD_DIM, (h + 1) * HEAD_DIM)
        x = _rms(bq_ref[:, cols].astype(F32), qg_ref[...])
        oq_ref[:, cols] = (_rope(x, cos, sin) * scale).astype(BF16)
    for h in range(ATTN_KV_HEADS):
        cols = slice(h * HEAD_DIM, (h + 1) * HEAD_DIM)
        x = _rms(bk_ref[:, cols].astype(F32), kg_ref[...])
        ok_ref[:, cols] = _rope(x, cos, sin).astype(BF16)
    for h in range(RET_HEADS):
        cols = slice(h * HEAD_DIM, (h + 1) * HEAD_DIM)
        ocq_ref[:, cols] = (_rope(cq_ref[:, cols].astype(F32), cos, sin) * scale).astype(BF16)
        ock_ref[:, cols] = _rope(ck_ref[:, cols].astype(F32), cos, sin).astype(BF16)


def _prep(z3, cos_t, sin_t, q_norm_g, k_norm_g, *, ts):
    b, s, _ = z3.shape
    ts = _tile(s, ts)

    def zspec(width, off):
        return pl.BlockSpec((None, ts, width), lambda bi, i: (bi, i, off // width))

    def ospec(width):
        return pl.BlockSpec((None, ts, width), lambda bi, i: (bi, i, 0))

    tab = pl.BlockSpec((ts, HEAD_DIM), lambda bi, i: (i, 0))
    gain = pl.BlockSpec((1, HEAD_DIM), lambda bi, i: (0, 0))
    return pl.pallas_call(
        _prep_kernel,
        grid=(b, s // ts),
        in_specs=[zspec(ATTN_Q_W, OFF_B_Q), zspec(ATTN_KV_W, OFF_B_K), zspec(RET_QK_W, OFF_C_Q),
                  zspec(RET_QK_W, OFF_C_K), tab, tab, gain, gain],
        out_specs=[ospec(ATTN_Q_W), ospec(ATTN_KV_W), ospec(RET_QK_W), ospec(RET_QK_W)],
        out_shape=[jax.ShapeDtypeStruct((b, s, w), BF16) for w in (ATTN_Q_W, ATTN_KV_W, RET_QK_W, RET_QK_W)],
        compiler_params=_params("parallel", "parallel"),
        name="qk_prep",
    )(z3, z3, z3, z3, cos_t, sin_t, q_norm_g.reshape(1, HEAD_DIM), k_norm_g.reshape(1, HEAD_DIM))


def _flash_kernel(q_ref, k_ref, v_ref, o_ref, qc_ref, m_ref, l_ref, acc_ref, *, tq):
    j = pl.program_id(3)

    @pl.when(j == 0)
    def _():
        for r in range(ATTN_REP):
            qc_ref[r * tq:(r + 1) * tq, :] = q_ref[:, r * HEAD_DIM:(r + 1) * HEAD_DIM]
        m_ref[...] = jnp.full(m_ref.shape, -jnp.inf, F32)
        l_ref[...] = jnp.zeros(l_ref.shape, F32)
        acc_ref[...] = jnp.zeros(acc_ref.shape, F32)

    s = lax.dot_general(qc_ref[...], k_ref[...], _NT, preferred_element_type=F32)
    m_old = m_ref[...]
    m_new = jnp.maximum(m_old, jnp.max(s, axis=-1, keepdims=True))
    p = jnp.exp(s - m_new)
    alpha = jnp.exp(m_old - m_new)
    l_ref[...] = alpha * l_ref[...] + jnp.sum(p, axis=-1, keepdims=True)
    acc_ref[...] = alpha * acc_ref[...] + jnp.dot(p.astype(BF16), v_ref[...], preferred_element_type=F32)
    m_ref[...] = m_new

    @pl.when(j == pl.num_programs(3) - 1)
    def _():
        out = acc_ref[...] / l_ref[...]
        for r in range(ATTN_REP):
            o_ref[:, r * HEAD_DIM:(r + 1) * HEAD_DIM] = out[r * tq:(r + 1) * tq, :].astype(o_ref.dtype)


def _flash(qr, kr, z3, *, tq, tk):
    b, s, _ = qr.shape
    tq, tk = _tile(s, tq), _tile(s, tk)
    gw = ATTN_REP * HEAD_DIM
    return pl.pallas_call(
        functools.partial(_flash_kernel, tq=tq),
        grid=(b, ATTN_KV_HEADS, s // tq, s // tk),
        in_specs=[
            pl.BlockSpec((None, tq, gw), lambda bi, g, i, j: (bi, i, g)),
            pl.BlockSpec((None, tk, HEAD_DIM), lambda bi, g, i, j: (bi, j, g)),
            pl.BlockSpec((None, tk, HEAD_DIM), lambda bi, g, i, j: (bi, j, OFF_B_V // HEAD_DIM + g)),
        ],
        out_specs=pl.BlockSpec((None, tq, gw), lambda bi, g, i, j: (bi, i, g)),
        out_shape=jax.ShapeDtypeStruct((b, s, ATTN_Q_W), BF16),
        scratch_shapes=[
            pltpu.VMEM((ATTN_REP * tq, HEAD_DIM), BF16),
            pltpu.VMEM((ATTN_REP * tq, 1), F32),
            pltpu.VMEM((ATTN_REP * tq, 1), F32),
            pltpu.VMEM((ATTN_REP * tq, HEAD_DIM), F32),
        ],
        compiler_params=_params("parallel", "parallel", "parallel", "arbitrary"),
        name="flash_attention",
    )(qr, kr, z3)


def _ret_chunk(q, k, v, st, dm, qd, kd, cd):
    s = lax.dot_general(q, k, _NT, preferred_element_type=F32) * dm
    o = jnp.dot(s.astype(BF16), v, preferred_element_type=F32)
    o = o + qd * jnp.dot(q, st.astype(BF16), preferred_element_type=F32)
    kt = (k.astype(F32).T * kd).astype(BF16)
    return o, cd * st + jnp.dot(kt, v, preferred_element_type=F32)


def _ret_fwd_kernel(q_ref, k_ref, v_ref, dm_ref, qd_ref, kd_ref, cd_ref, o_ref, st_ref, *, n_chunks):
    @pl.when(pl.program_id(2) == 0)
    def _():
        st_ref[...] = jnp.zeros(st_ref.shape, F32)

    for c in range(n_chunks):
        rows = slice(c * CHUNK, (c + 1) * CHUNK)
        o, st = _ret_chunk(q_ref[rows, :], k_ref[rows, :], v_ref[rows, :], st_ref[...],
                           dm_ref[...], qd_ref[...], kd_ref[...], cd_ref[...])
        o_ref[rows, :] = o
        st_ref[...] = st


def _ret_bwd_kernel(q_ref, k_ref, v_ref, g_ref, of_ref, dm_ref, qd_ref, kd_ref, cd_ref, ng_ref,
                    o_ref, st_ref, *, n_chunks):
    @pl.when(pl.program_id(2) == 0)
    def _():
        st_ref[...] = jnp.zeros(st_ref.shape, F32)

    for c in reversed(range(n_chunks)):
        rows = slice(c * CHUNK, (c + 1) * CHUNK)
        o, st = _ret_chunk(q_ref[rows, :], k_ref[rows, :], v_ref[rows, :], st_ref[...],
                           dm_ref[...], qd_ref[...], kd_ref[...], cd_ref[...])
        st_ref[...] = st
        o = o + of_ref[rows, :]
        oc = o - jnp.mean(o, axis=-1, keepdims=True)
        o = oc * lax.rsqrt(jnp.mean(oc * oc, axis=-1, keepdims=True) + NORM_EPS) * ng_ref[...]
        gate = g_ref[rows, :].astype(F32)
        o_ref[rows, :] = (o * (gate * jax.nn.sigmoid(gate))).astype(o_ref.dtype)


def _retention(cqr, ckr, z3, tabs, norm_g, *, ts):
    b, s, _ = cqr.shape
    ts = _tile(s, ts)
    n = s // ts
    dv = RET_V_DIM

    def seq(width, off, rev):
        if rev:
            return pl.BlockSpec((None, ts, width), lambda bi, h, i: (bi, n - 1 - i, off // width + h))
        return pl.BlockSpec((None, ts, width), lambda bi, h, i: (bi, i, off // width + h))

    def per_head(r, c):
        return pl.BlockSpec((None, r, c), lambda bi, h, i: (h, 0, 0))

    def tab_specs():
        return [per_head(CHUNK, CHUNK), per_head(CHUNK, 1), per_head(1, CHUNK), per_head(1, dv)]

    sem = _params("parallel", "parallel", "arbitrary")
    o_fwd = pl.pallas_call(
        functools.partial(_ret_fwd_kernel, n_chunks=ts // CHUNK),
        grid=(b, RET_HEADS, n),
        in_specs=[seq(HEAD_DIM, 0, False), seq(HEAD_DIM, 0, False), seq(dv, OFF_C_V, False)] + tab_specs(),
        out_specs=seq(dv, 0, False),
        out_shape=jax.ShapeDtypeStruct((b, s, RET_V_W), F32),
        scratch_shapes=[pltpu.VMEM((HEAD_DIM, dv), F32)],
        compiler_params=sem,
        name="retention_fwd",
    )(cqr, ckr, z3, *tabs[0])
    return pl.pallas_call(
        functools.partial(_ret_bwd_kernel, n_chunks=ts // CHUNK),
        grid=(b, RET_HEADS, n),
        in_specs=[seq(HEAD_DIM, 0, True), seq(HEAD_DIM, 0, True), seq(dv, OFF_C_V, True), seq(dv, OFF_C_G, True),
                  seq(dv, 0, True)] + tab_specs() + [pl.BlockSpec((1, dv), lambda bi, h, i: (0, h))],
        out_specs=seq(dv, 0, True),
        out_shape=jax.ShapeDtypeStruct((b, s, RET_V_W), BF16),
        scratch_shapes=[pltpu.VMEM((HEAD_DIM, dv), F32)],
        compiler_params=sem,
        name="retention_bwd",
    )(cqr, ckr, z3, z3, o_fwd, *tabs[1], norm_g.reshape(1, RET_V_W))


def _retention_tables(ret_decay):
    lg = -jnp.exp(ret_decay.astype(F32))
    idx = jnp.arange(CHUNK, dtype=F32)
    diff = idx[:, None] - idx[None, :]
    out = []
    for d, (mask, delta, qpow, kpow) in enumerate((
            (diff >= 0, diff, idx + 1.0, CHUNK - 1.0 - idx),
            (diff < 0, -diff, CHUNK - idx, idx))):
        l = lg[d][:, None, None]
        dm = jnp.where(mask, jnp.exp(l * jnp.where(mask, delta, 0.0)), 0.0)
        qd = jnp.exp(lg[d][:, None] * qpow)[:, :, None]
        kd = jnp.exp(lg[d][:, None] * kpow)[:, None, :]
        cd = jnp.broadcast_to(jnp.exp(lg[d] * CHUNK)[:, None, None], (RET_HEADS, 1, RET_V_DIM))
        out.append((dm, qd, kd, cd))
    return out


def _merge_kernel(a_ref, b_ref, c_ref, wa_ref, wb_ref, wc_ref, ga_ref, gb_ref, gc_ref, o_ref):
    acc = jax.nn.sigmoid(ga_ref[...].astype(F32)) * jnp.dot(a_ref[...], wa_ref[...], preferred_element_type=F32)
    acc = acc + jax.nn.sigmoid(gb_ref[...].astype(F32)) * jnp.dot(b_ref[...], wb_ref[...], preferred_element_type=F32)
    acc = acc + jax.nn.sigmoid(gc_ref[...].astype(F32)) * jnp.dot(c_ref[...], wc_ref[...], preferred_element_type=F32)
    o_ref[...] = acc.astype(o_ref.dtype)


def _merge(out_a, out_b, out_c, w_a, w_b, w_c, z, *, tm, tn):
    t, k = out_a.shape
    tm, tn = _tile(t, tm), _tile(D_MODEL, tn)
    lhs = pl.BlockSpec((tm, k), lambda i, j: (i, 0))
    rhs = pl.BlockSpec((k, tn), lambda i, j: (0, j))

    def gate(off):
        return pl.BlockSpec((tm, tn), lambda i, j: (i, off // tn + j))

    return pl.pallas_call(
        _merge_kernel,
        grid=(t // tm, D_MODEL // tn),
        in_specs=[lhs, lhs, lhs, rhs, rhs, rhs, gate(OFF_GATE_A), gate(OFF_GATE_B), gate(OFF_GATE_C)],
        out_specs=pl.BlockSpec((tm, tn), lambda i, j: (i, j)),
        out_shape=jax.ShapeDtypeStruct((t, D_MODEL), BF16),
        compiler_params=_params("parallel", "parallel"),
        name="branch_merge",
    )(out_a, out_b, out_c, w_a, w_b, w_c, z, z, z)


def _matmul_res_kernel(a_ref, w_ref, x_ref, o_ref):
    o_ref[...] = x_ref[...] + jnp.dot(a_ref[...], w_ref[...], preferred_element_type=F32)


def _matmul_res(a, w, x, *, tm, tn):
    t, k = a.shape
    n = w.shape[1]
    tm, tn = _tile(t, tm), _tile(n, tn)
    return pl.pallas_call(
        _matmul_res_kernel,
        grid=(t // tm, n // tn),
        in_specs=[
            pl.BlockSpec((tm, k), lambda i, j: (i, 0)),
            pl.BlockSpec((k, tn), lambda i, j: (0, j)),
            pl.BlockSpec((tm, tn), lambda i, j: (i, j)),
        ],
        out_specs=pl.BlockSpec((tm, tn), lambda i, j: (i, j)),
        out_shape=jax.ShapeDtypeStruct((t, n), F32),
        compiler_params=_params("parallel", "parallel"),
        name="out_proj_residual",
    )(a, w, x)


def _xattn_kernel(x_ref, g_ref, wq_ref, kv_ref, wo_ref, o_ref):
    x = x_ref[...]
    h = _rms(x, g_ref[...]).astype(BF16)
    q = (jnp.dot(h, wq_ref[...], preferred_element_type=F32) * (HEAD_DIM ** -0.5)).astype(BF16)
    heads = []
    for hh in range(XATTN_HEADS):
        cols = slice(hh * HEAD_DIM, (hh + 1) * HEAD_DIM)
        k = kv_ref[:, cols]
        v = kv_ref[:, XATTN_W + hh * HEAD_DIM:XATTN_W + (hh + 1) * HEAD_DIM]
        s = lax.dot_general(q[:, cols], k, _NT, preferred_element_type=F32)
        p = jnp.exp(s - jnp.max(s, axis=-1, keepdims=True))
        o = jnp.dot(p.astype(BF16), v, preferred_element_type=F32) / jnp.sum(p, axis=-1, keepdims=True)
        heads.append(o.astype(BF16))
    o_ref[...] = x + jnp.dot(jnp.concatenate(heads, axis=1), wo_ref[...], preferred_element_type=F32)


def _xattn(x3, norm_g, wq, kv3, wo, *, tm):
    b, s, d = x3.shape
    m = kv3.shape[1]
    tm = _tile(s, tm)
    return pl.pallas_call(
        _xattn_kernel,
        grid=(b, s // tm),
        in_specs=[
            pl.BlockSpec((None, tm, d), lambda bi, i: (bi, i, 0)),
            pl.BlockSpec((1, d), lambda bi, i: (0, 0)),
            pl.BlockSpec((d, XATTN_W), lambda bi, i: (0, 0)),
            pl.BlockSpec((None, m, 2 * XATTN_W), lambda bi, i: (bi, 0, 0)),
            pl.BlockSpec((XATTN_W, d), lambda bi, i: (0, 0)),
        ],
        out_specs=pl.BlockSpec((None, tm, d), lambda bi, i: (bi, i, 0)),
        out_shape=jax.ShapeDtypeStruct((b, s, d), F32),
        compiler_params=_params("parallel", "parallel"),
        name="memory_cross_attention",
    )(x3, norm_g.reshape(1, d), wq, kv3, wo)


def _router_kernel(x_ref, g_ref, wr_ref, br_ref, h_ref, ct_ref):
    h = _rms(x_ref[...], g_ref[...])
    h_ref[...] = h.astype(BF16)
    lt = lax.dot_general(wr_ref[...], h, _NT, precision=lax.Precision.HIGHEST,
                         preferred_element_type=F32) + br_ref[...]
    gl = [lt[i:i + 1, :] for i in range(N_GROUPS)]
    gmax = jnp.maximum(jnp.maximum(gl[0], gl[1]), jnp.maximum(gl[2], gl[3]))
    denom = sum(jnp.exp(v - gmax) for v in gl)
    g_prob = 1.0 / denom
    g_idx = jnp.where(gl[0] == gmax, 0, jnp.where(gl[1] == gmax, 1, jnp.where(gl[2] == gmax, 2, 3)))
    sel = []
    for j in range(EXPERTS_PER_GROUP):
        rows = [lt[N_GROUPS + g * EXPERTS_PER_GROUP + j:N_GROUPS + g * EXPERTS_PER_GROUP + j + 1, :]
                for g in range(N_GROUPS)]
        sel.append(jnp.where(g_idx == 0, rows[0], jnp.where(g_idx == 1, rows[1],
                                                            jnp.where(g_idx == 2, rows[2], rows[3]))))
    top1 = jnp.maximum(jnp.maximum(sel[0], sel[1]), jnp.maximum(sel[2], sel[3]))
    i1 = jnp.where(sel[0] == top1, 0, jnp.where(sel[1] == top1, 1, jnp.where(sel[2] == top1, 2, 3)))
    rest = [jnp.where(i1 == j, -jnp.inf, sel[j]) for j in range(EXPERTS_PER_GROUP)]
    top2 = jnp.maximum(jnp.maximum(rest[0], rest[1]), jnp.maximum(rest[2], rest[3]))
    i2 = jnp.where(rest[0] == top2, 0, jnp.where(rest[1] == top2, 1, jnp.where(rest[2] == top2, 2, 3)))
    e2 = jnp.exp(top2 - top1)
    w1 = g_prob / (1.0 + e2)
    w2 = g_prob * e2 / (1.0 + e2)
    id1 = g_idx * EXPERTS_PER_GROUP + i1
    id2 = g_idx * EXPERTS_PER_GROUP + i2
    eid = lax.broadcasted_iota(jnp.int32, ct_ref.shape, 0)
    ct_ref[...] = jnp.where(eid == id1, w1, 0.0) + jnp.where(eid == id2, w2, 0.0)


def _router(x, norm_g, wr_t, br, *, tm):
    t, d = x.shape
    tm = _tile(t, tm)
    return pl.pallas_call(
        _router_kernel,
        grid=(t // tm,),
        in_specs=[
            pl.BlockSpec((tm, d), lambda i: (i, 0)),
            pl.BlockSpec((1, d), lambda i: (0, 0)),
            pl.BlockSpec((ROUTER_ROWS, d), lambda i: (0, 0)),
            pl.BlockSpec((ROUTER_ROWS, 1), lambda i: (0, 0)),
        ],
        out_specs=[pl.BlockSpec((tm, d), lambda i: (i, 0)), pl.BlockSpec((N_EXPERTS, tm), lambda i: (0, i))],
        out_shape=[jax.ShapeDtypeStruct((t, d), BF16), jax.ShapeDtypeStruct((N_EXPERTS, t), F32)],
        compiler_params=_params("parallel"),
        name="router",
    )(x, norm_g.reshape(1, d), wr_t, br)


def _moe_kernel(h_ref, c_ref, x_ref, wg_ref, wu_ref, wd_ref, fg_ref, o_ref, acc_ref, *, final_norm):
    e = pl.program_id(1)

    @pl.when(e == 0)
    def _():
        acc_ref[...] = jnp.zeros(acc_ref.shape, F32)

    h = h_ref[...]
    a = jnp.dot(h, wg_ref[...], preferred_element_type=F32)
    hid = (a * jax.nn.sigmoid(a)) * jnp.dot(h, wu_ref[...], preferred_element_type=F32)
    c = c_ref[...]
    lane = lax.broadcasted_iota(jnp.int32, c.shape, 1)
    ce = jnp.sum(jnp.where(lane == e, c, 0.0), axis=-1, keepdims=True)
    acc_ref[...] += jnp.dot(hid.astype(BF16), wd_ref[...], preferred_element_type=F32) * ce

    @pl.when(e == pl.num_programs(1) - 1)
    def _():
        y = x_ref[...] + acc_ref[...]
        if final_norm:
            y = _rms(y, fg_ref[...])
        o_ref[...] = y


def _moe(h, comb, x, w_gate, w_up, w_down, final_g, *, tm, final_norm):
    t, d = x.shape
    tm = _tile(t, tm)
    return pl.pallas_call(
        functools.partial(_moe_kernel, final_norm=final_norm),
        grid=(t // tm, N_EXPERTS),
        in_specs=[
            pl.BlockSpec((tm, d), lambda i, e: (i, 0)),
            pl.BlockSpec((tm, N_EXPERTS), lambda i, e: (i, 0)),
            pl.BlockSpec((tm, d), lambda i, e: (i, 0)),
            pl.BlockSpec((None, d, EXPERT_FF), lambda i, e: (e, 0, 0)),
            pl.BlockSpec((None, d, EXPERT_FF), lambda i, e: (e, 0, 0)),
            pl.BlockSpec((None, EXPERT_FF, d), lambda i, e: (e, 0, 0)),
            pl.BlockSpec((1, d), lambda i, e: (0, 0)),
        ],
        out_specs=pl.BlockSpec((tm, d), lambda i, e: (i, 0)),
        out_shape=jax.ShapeDtypeStruct((t, d), F32),
        scratch_shapes=[pltpu.VMEM((tm, d), F32)],
        compiler_params=_params("parallel", "arbitrary"),
        name="experts",
    )(h, comb, x, w_gate, w_up, w_down, final_g.reshape(1, d))


def _rope_tables(seq_len):
    rows = seq_len // GRID_W
    row = jnp.repeat(jnp.arange(rows, dtype=F32), GRID_W)
    col = jnp.tile(jnp.arange(GRID_W, dtype=F32), rows)
    n_freq = HEAD_DIM // 4
    inv = ROPE_THETA ** (-jnp.arange(n_freq, dtype=F32) / n_freq)
    ar = row[:, None] * inv
    ac = col[:, None] * inv
    cos = jnp.concatenate([jnp.cos(ar), jnp.cos(ar), jnp.cos(ac), jnp.cos(ac)], axis=1)
    sin = jnp.concatenate([-jnp.sin(ar), jnp.sin(ar), -jnp.sin(ac), jnp.sin(ac)], axis=1)
    return cos, sin


def _prepare_layer(l, p):
    wr_t = jnp.concatenate([p["router_group_w"][l].T, p["router_expert_w"][l].T,
                            jnp.zeros((ROUTER_ROWS - N_GROUPS - N_EXPERTS, D_MODEL), F32)], axis=0)
    br = jnp.concatenate([p["router_group_b"][l], p["router_expert_b"][l].reshape(-1),
                          jnp.zeros((ROUTER_ROWS - N_GROUPS - N_EXPERTS,), F32)]).reshape(ROUTER_ROWS, 1)
    return dict(
        norm_mix_g=p["norm_mix_g"][l], w_in=p["w_in"][l].astype(BF16),
        sgu_norm_g=p["sgu_norm_g"][l], sgu_w=p["sgu_w"][l].astype(BF16), sgu_b_t=p["sgu_b"][l].T,
        q_norm_g=p["attn_q_norm_g"][l], k_norm_g=p["attn_k_norm_g"][l],
        ret_tabs=_retention_tables(p["ret_decay"][l]), ret_norm_g=p["ret_norm_g"][l],
        w_a=p["w_branch_a"][l].astype(BF16), w_b=p["w_branch_b"][l].astype(BF16),
        w_c=p["w_branch_c"][l].astype(BF16), w_out=p["w_out"][l].astype(BF16),
        xattn_norm_g=p["xattn_norm_g"][l], mem_norm_g=p["mem_norm_g"][l],
        wq=p["xattn_wq"][l].astype(BF16), wkv=p["xattn_wkv"][l].astype(BF16), wo=p["xattn_wo"][l].astype(BF16),
        ffn_norm_g=p["ffn_norm_g"][l], wr_t=wr_t, br=br,
        w_gate=p["expert_w_gate"][l].astype(BF16), w_up=p["expert_w_up"][l].astype(BF16),
        w_down=p["expert_w_down"][l].astype(BF16),
    )


def _trunk(x, mem, layers, final_norm_g):
    b, s, d = x.shape
    t = b * s
    m = mem.shape[1]
    cos_t, sin_t = _rope_tables(s)
    xt = x.reshape(t, d)
    memt = mem.reshape(b * m, d)
    for l, w in enumerate(layers):
        z = _rms_matmul(xt, w["norm_mix_g"], w["w_in"], tm=1024, tn=512)
        z3 = z.reshape(b, s, IN_WIDTH)
        out_a = _sgu(z3, w["sgu_norm_g"], w["sgu_w"], w["sgu_b_t"], ts=512)
        qr, kr, cqr, ckr = _prep(z3, cos_t, sin_t, w["q_norm_g"], w["k_norm_g"], ts=512)
        out_b = _flash(qr, kr, z3, tq=256, tk=512)
        out_c = _retention(cqr, ckr, z3, w["ret_tabs"], w["ret_norm_g"], ts=512)
        merged = _merge(out_a.reshape(t, SGU_WIDTH), out_b.reshape(t, ATTN_Q_W), out_c.reshape(t, RET_V_W),
                        w["w_a"], w["w_b"], w["w_c"], z, tm=512, tn=512)
        xt = _matmul_res(merged, w["w_out"], xt, tm=512, tn=512)
        kv = _rms_matmul(memt, w["mem_norm_g"], w["wkv"], tm=256, tn=512)
        xt = _xattn(xt.reshape(b, s, d), w["xattn_norm_g"], w["wq"], kv.reshape(b, m, 2 * XATTN_W), w["wo"],
                    tm=512).reshape(t, d)
        h, ct = _router(xt, w["ffn_norm_g"], w["wr_t"], w["br"], tm=512)
        xt = _moe(h, ct.T, xt, w["w_gate"], w["w_up"], w["w_down"], final_norm_g, tm=512,
                  final_norm=(l == len(layers) - 1))
    return xt.reshape(b, s, d)


def kernel(x_prompt, x_sample, mem_prompt, mem_sample, norm_mix_g, w_in, sgu_norm_g, sgu_w, sgu_b, attn_q_norm_g, attn_k_norm_g, ret_decay, ret_norm_g, w_branch_a, w_branch_b, w_branch_c, w_out, xattn_norm_g, mem_norm_g, xattn_wq, xattn_wkv, xattn_wo, ffn_norm_g, router_group_w, router_group_b, router_expert_w, router_expert_b, expert_w_gate, expert_w_up, expert_w_down, final_norm_g):
    p = dict(norm_mix_g=norm_mix_g, w_in=w_in, sgu_norm_g=sgu_norm_g, sgu_w=sgu_w, sgu_b=sgu_b,
             attn_q_norm_g=attn_q_norm_g, attn_k_norm_g=attn_k_norm_g, ret_decay=ret_decay, ret_norm_g=ret_norm_g,
             w_branch_a=w_branch_a, w_branch_b=w_branch_b, w_branch_c=w_branch_c, w_out=w_out,
             xattn_norm_g=xattn_norm_g, mem_norm_g=mem_norm_g, xattn_wq=xattn_wq, xattn_wkv=xattn_wkv,
             xattn_wo=xattn_wo, ffn_norm_g=ffn_norm_g, router_group_w=router_group_w, router_group_b=router_group_b,
             router_expert_w=router_expert_w, router_expert_b=router_expert_b, expert_w_gate=expert_w_gate,
             expert_w_up=expert_w_up, expert_w_down=expert_w_down)
    layers = [_prepare_layer(l, p) for l in range(DEPTH)]
    y_prompt = _trunk(x_prompt, mem_prompt, layers, final_norm_g)
    y_sample = _trunk(x_sample, mem_sample, layers, final_norm_g)
    return (y_prompt, y_sample)
```

```python
import functools

import jax
import jax.numpy as jnp
import numpy as np
from jax import lax
from jax.experimental import pallas as pl
from jax.experimental.pallas import tpu as pltpu

F32 = jnp.float32
BF16 = jnp.bfloat16

D_MODEL = 2048
DEPTH = 2
GRID_W = 64
ROPE_THETA = 10000.0
CHUNK = 128
HEAD_DIM = 128
NORM_EPS = 1e-6
SGU_WIDTH = 1024
SGU_GROUPS = 4
SGU_GROUP_W = SGU_WIDTH // SGU_GROUPS
ATTN_HEADS = 8
ATTN_KV_HEADS = 2
ATTN_REP = ATTN_HEADS // ATTN_KV_HEADS
ATTN_Q_W = ATTN_HEADS * HEAD_DIM
ATTN_KV_W = ATTN_KV_HEADS * HEAD_DIM
RET_HEADS = 4
RET_V_DIM = 256
RET_QK_W = RET_HEADS * HEAD_DIM
RET_V_W = RET_HEADS * RET_V_DIM
XATTN_HEADS = 4
XATTN_W = XATTN_HEADS * HEAD_DIM
N_GROUPS = 4
EXPERTS_PER_GROUP = 4
N_EXPERTS = N_GROUPS * EXPERTS_PER_GROUP
EXPERT_FF = 512
ROUTER_ROWS = 24

OFF_A_U = 0
OFF_A_V = OFF_A_U + SGU_WIDTH
OFF_B_Q = OFF_A_V + SGU_WIDTH
OFF_B_K = OFF_B_Q + ATTN_Q_W
OFF_B_V = OFF_B_K + ATTN_KV_W
OFF_C_Q = OFF_B_V + ATTN_KV_W
OFF_C_K = OFF_C_Q + RET_QK_W
OFF_C_V = OFF_C_K + RET_QK_W
OFF_C_G = OFF_C_V + RET_V_W
OFF_GATE_A = OFF_C_G + RET_V_W
OFF_GATE_B = OFF_GATE_A + D_MODEL
OFF_GATE_C = OFF_GATE_B + D_MODEL
IN_WIDTH = OFF_GATE_C + D_MODEL

V7X_VMEM_LIMIT_BYTES = 56 * 1024 * 1024

_NT = (((1,), (1,)), ((), ()))
LOG2_E = 1.4426950408889634
FIXED_SHIFT_LIMIT = 60.0


def _tile(n, pref):
    t = min(pref, n)
    while n % t:
        t //= 2
    return t


def _params(*sem):
    return pltpu.CompilerParams(dimension_semantics=sem, vmem_limit_bytes=V7X_VMEM_LIMIT_BYTES)


def _rms(x, g):
    return x * lax.rsqrt(jnp.mean(x * x, axis=-1, keepdims=True) + NORM_EPS) * g


def _rms_matmul_kernel(x_ref, g_ref, w_ref, o_ref, h_ref):
    @pl.when(pl.program_id(1) == 0)
    def _():
        h_ref[...] = _rms(x_ref[...], g_ref[...]).astype(BF16)

    o_ref[...] = jnp.dot(h_ref[...], w_ref[...], preferred_element_type=F32).astype(o_ref.dtype)


def _rms_matmul(x, g, w, *, tm, tn):
    t, d = x.shape
    n = w.shape[1]
    tm, tn = _tile(t, tm), _tile(n, tn)
    return pl.pallas_call(
        _rms_matmul_kernel,
        grid=(t // tm, n // tn),
        in_specs=[
            pl.BlockSpec((tm, d), lambda i, j: (i, 0)),
            pl.BlockSpec((1, d), lambda i, j: (0, 0)),
            pl.BlockSpec((d, tn), lambda i, j: (0, j)),
        ],
        out_specs=pl.BlockSpec((tm, tn), lambda i, j: (i, j)),
        out_shape=jax.ShapeDtypeStruct((t, n), BF16),
        scratch_shapes=[pltpu.VMEM((tm, d), BF16)],
        compiler_params=_params("parallel", "arbitrary"),
        name="rms_matmul",
    )(x, g.reshape(1, d), w)


def _gelu(x):
    return 0.5 * x * (1.0 + lax.erf(x * (2.0 ** -0.5)))


def _sgu_kernel(u_ref, v_ref, ng_ref, ws_ref, bs_ref, o_ref, *, n_chunks):
    for c in range(n_chunks):
        rows = slice(c * CHUNK, (c + 1) * CHUNK)
        u = _gelu(u_ref[rows, :].astype(F32))
        v = _gelu(v_ref[rows, :].astype(F32))
        vc = v - jnp.mean(v, axis=-1, keepdims=True)
        vn = vc * lax.rsqrt(jnp.mean(vc * vc, axis=-1, keepdims=True) + NORM_EPS) * ng_ref[...]
        vn = vn.astype(BF16)
        for g in range(SGU_GROUPS):
            cols = slice(g * SGU_GROUP_W, (g + 1) * SGU_GROUP_W)
            s = jnp.dot(ws_ref[g], vn[:, cols], preferred_element_type=F32) + bs_ref[:, g:g + 1]
            o_ref[rows, cols] = (u[:, cols] * s).astype(o_ref.dtype)


def _sgu(z3, norm_g, w_s, b_s_t, *, ts):
    b, s, _ = z3.shape
    ts = _tile(s, ts)
    return pl.pallas_call(
        functools.partial(_sgu_kernel, n_chunks=ts // CHUNK),
        grid=(b, s // ts),
        in_specs=[
            pl.BlockSpec((None, ts, SGU_WIDTH), lambda bi, i: (bi, i, OFF_A_U // SGU_WIDTH)),
            pl.BlockSpec((None, ts, SGU_WIDTH), lambda bi, i: (bi, i, OFF_A_V // SGU_WIDTH)),
            pl.BlockSpec((1, SGU_WIDTH), lambda bi, i: (0, 0)),
            pl.BlockSpec((SGU_GROUPS, CHUNK, CHUNK), lambda bi, i: (0, 0, 0)),
            pl.BlockSpec((CHUNK, SGU_GROUPS), lambda bi, i: (0, 0)),
        ],
        out_specs=pl.BlockSpec((None, ts, SGU_WIDTH), lambda bi, i: (bi, i, 0)),
        out_shape=jax.ShapeDtypeStruct((b, s, SGU_WIDTH), BF16),
        compiler_params=_params("parallel", "parallel"),
        name="sgu",
    )(z3, z3, norm_g.reshape(1, SGU_WIDTH), w_s, b_s_t)


def _rope(x, cos, sin_signed):
    lane = lax.broadcasted_iota(jnp.int32, x.shape, 1)
    first_half = (lane % (HEAD_DIM // 2)) < (HEAD_DIM // 4)
    partner = jnp.where(first_half, pltpu.roll(x, HEAD_DIM - HEAD_DIM // 4, 1), pltpu.roll(x, HEAD_DIM // 4, 1))
    return x * cos + partner * sin_signed


def _prep_kernel(bq_ref, bk_ref, bv_ref, cq_ref, ck_ref, cos_ref, sin_ref, qg_ref, kg_ref,
                 oq_ref, ok_ref, ov_ref, ocq_ref, ock_ref, qn_ref, kn_ref):
    cos = cos_ref[...]
    sin = sin_ref[...]
    scale = HEAD_DIM ** -0.5
    q_ss = None
    for h in range(ATTN_HEADS):
        cols = slice(h * HEAD_DIM, (h + 1) * HEAD_DIM)
        x = _rope(_rms(bq_ref[:, cols].astype(F32), qg_ref[...]), cos, sin) * (scale * LOG2_E)
        oq_ref[:, cols] = x.astype(BF16)
        ss = jnp.sum(x * x, axis=-1, keepdims=True)
        q_ss = ss if q_ss is None else jnp.maximum(q_ss, ss)
    k_ss = None
    for h in range(ATTN_KV_HEADS):
        cols = slice(h * HEAD_DIM, (h + 1) * HEAD_DIM)
        x = _rope(_rms(bk_ref[:, cols].astype(F32), kg_ref[...]), cos, sin)
        ok_ref[:, cols] = x.astype(BF16)
        ss = jnp.sum(x * x, axis=-1, keepdims=True)
        k_ss = ss if k_ss is None else jnp.maximum(k_ss, ss)
        ov_ref[:, 2 * h * HEAD_DIM:(2 * h + 1) * HEAD_DIM] = bv_ref[:, cols]
        ov_ref[:, (2 * h + 1) * HEAD_DIM:(2 * h + 2) * HEAD_DIM] = jnp.ones((bv_ref.shape[0], HEAD_DIM), BF16)
    qn_ref[...] = jnp.broadcast_to(jnp.max(q_ss, axis=0, keepdims=True), qn_ref.shape)
    kn_ref[...] = jnp.broadcast_to(jnp.max(k_ss, axis=0, keepdims=True), kn_ref.shape)
    for h in range(RET_HEADS):
        cols = slice(h * HEAD_DIM, (h + 1) * HEAD_DIM)
        ocq_ref[:, cols] = (_rope(cq_ref[:, cols].astype(F32), cos, sin) * scale).astype(BF16)
        ock_ref[:, cols] = _rope(ck_ref[:, cols].astype(F32), cos, sin).astype(BF16)


def _prep(z3, cos_t, sin_t, q_norm_g, k_norm_g, *, ts):
    b, s, _ = z3.shape
    ts = _tile(s, ts)
    n = s // ts

    def zspec(width, off):
        return pl.BlockSpec((None, ts, width), lambda bi, i: (bi, i, off // width))

    def ospec(width):
        return pl.BlockSpec((None, ts, width), lambda bi, i: (bi, i, 0))

    tab = pl.BlockSpec((ts, HEAD_DIM), lambda bi, i: (i, 0))
    gain = pl.BlockSpec((1, HEAD_DIM), lambda bi, i: (0, 0))
    nspec = pl.BlockSpec((None, None, 8, HEAD_DIM), lambda bi, i: (bi, i, 0, 0))
    widths = (ATTN_Q_W, ATTN_KV_W, 2 * ATTN_KV_W, RET_QK_W, RET_QK_W)
    return pl.pallas_call(
        _prep_kernel,
        grid=(b, n),
        in_specs=[zspec(ATTN_Q_W, OFF_B_Q), zspec(ATTN_KV_W, OFF_B_K), zspec(ATTN_KV_W, OFF_B_V),
                  zspec(RET_QK_W, OFF_C_Q), zspec(RET_QK_W, OFF_C_K), tab, tab, gain, gain],
        out_specs=[ospec(w) for w in widths] + [nspec, nspec],
        out_shape=[jax.ShapeDtypeStruct((b, s, w), BF16) for w in widths]
        + [jax.ShapeDtypeStruct((b, n, 8, HEAD_DIM), F32)] * 2,
        compiler_params=_params("parallel", "parallel"),
        name="qk_prep",
    )(z3, z3, z3, z3, z3, cos_t, sin_t, q_norm_g.reshape(1, HEAD_DIM), k_norm_g.reshape(1, HEAD_DIM))


def _flash_kernel(b2_ref, q_ref, k_ref, v_ref, o_ref, qc_ref, acc_ref, m_ref, *, tq, tk):
    for r in range(ATTN_REP):
        qc_ref[r * tq:(r + 1) * tq, :] = q_ref[:, r * HEAD_DIM:(r + 1) * HEAD_DIM]
    b2 = b2_ref[pl.program_id(0)]
    n_kv = k_ref.shape[0] // tk
    acc_ref[...] = jnp.zeros(acc_ref.shape, F32)

    def tiles(j):
        off = pl.multiple_of(j * tk, tk)
        k = k_ref[pl.ds(off, tk), :]
        return lax.dot_general(qc_ref[...], k, _NT, preferred_element_type=F32), v_ref[pl.ds(off, tk), :]

    @pl.when(b2 <= FIXED_SHIFT_LIMIT)
    def _():
        def body(j, carry):
            s, v = tiles(j)
            acc_ref[...] += jnp.dot(jnp.exp2(s - b2).astype(BF16), v, preferred_element_type=F32)
            return carry

        lax.fori_loop(0, n_kv, body, 0, unroll=4 if n_kv % 4 == 0 else 1)

    @pl.when(b2 > FIXED_SHIFT_LIMIT)
    def _():
        m_ref[...] = jnp.full(m_ref.shape, -jnp.inf, F32)

        def body(j, carry):
            s, v = tiles(j)
            m_old = m_ref[...]
            m_new = jnp.maximum(m_old, jnp.max(s, axis=-1, keepdims=True))
            p = jnp.exp2(s - m_new).astype(BF16)
            acc_ref[...] = jnp.exp2(m_old - m_new) * acc_ref[...] + jnp.dot(p, v, preferred_element_type=F32)
            m_ref[...] = m_new
            return carry

        lax.fori_loop(0, n_kv, body, 0)

    out = acc_ref[:, :HEAD_DIM] / acc_ref[:, HEAD_DIM:]
    for r in range(ATTN_REP):
        o_ref[:, r * HEAD_DIM:(r + 1) * HEAD_DIM] = out[r * tq:(r + 1) * tq, :].astype(o_ref.dtype)


def _flash(qr, kr, vaug, bound2, *, tq, tk):
    b, s, _ = qr.shape
    tq, tk = _tile(s, tq), _tile(s, tk)
    gw = ATTN_REP * HEAD_DIM
    grid_spec = pltpu.PrefetchScalarGridSpec(
        num_scalar_prefetch=1,
        grid=(b, ATTN_KV_HEADS, s // tq),
        in_specs=[
            pl.BlockSpec((None, tq, gw), lambda bi, g, i, b2: (bi, i, g)),
            pl.BlockSpec((None, s, HEAD_DIM), lambda bi, g, i, b2: (bi, 0, g)),
            pl.BlockSpec((None, s, 2 * HEAD_DIM), lambda bi, g, i, b2: (bi, 0, g)),
        ],
        out_specs=pl.BlockSpec((None, tq, gw), lambda bi, g, i, b2: (bi, i, g)),
        scratch_shapes=[
            pltpu.VMEM((ATTN_REP * tq, HEAD_DIM), BF16),
            pltpu.VMEM((ATTN_REP * tq, 2 * HEAD_DIM), F32),
            pltpu.VMEM((ATTN_REP * tq, 1), F32),
        ],
    )
    return pl.pallas_call(
        functools.partial(_flash_kernel, tq=tq, tk=tk),
        grid_spec=grid_spec,
        out_shape=jax.ShapeDtypeStruct((b, s, ATTN_Q_W), BF16),
        compiler_params=_params("parallel", "parallel", "arbitrary"),
        name="flash_attention",
    )(bound2, qr, kr, vaug)


def _score_bound(qn, kn):
    return jnp.sqrt(jnp.max(qn[:, :, 0, 0], axis=1) * jnp.max(kn[:, :, 0, 0], axis=1)) * 1.01


def _ret_chunk(q, k, v, st, dm, qd, kd, cd):
    s = lax.dot_general(q, k, _NT, preferred_element_type=F32) * dm
    o = jnp.dot(s.astype(BF16), v, preferred_element_type=F32)
    o = o + qd * jnp.dot(q, st.astype(BF16), preferred_element_type=F32)
    kt = (k.astype(F32).T * kd).astype(BF16)
    return o, cd * st + jnp.dot(kt, v, preferred_element_type=F32)


def _ret_fwd_kernel(q_ref, k_ref, v_ref, dm_ref, qd_ref, kd_ref, cd_ref, o_ref, st_ref, *, n_chunks):
    @pl.when(pl.program_id(2) == 0)
    def _():
        st_ref[...] = jnp.zeros(st_ref.shape, F32)

    for c in range(n_chunks):
        rows = slice(c * CHUNK, (c + 1) * CHUNK)
        o, st = _ret_chunk(q_ref[rows, :], k_ref[rows, :], v_ref[rows, :], st_ref[...],
                           dm_ref[...], qd_ref[...], kd_ref[...], cd_ref[...])
        o_ref[rows, :] = o
        st_ref[...] = st


def _ret_bwd_kernel(q_ref, k_ref, v_ref, g_ref, of_ref, dm_ref, qd_ref, kd_ref, cd_ref, ng_ref,
                    o_ref, st_ref, *, n_chunks):
    @pl.when(pl.program_id(2) == 0)
    def _():
        st_ref[...] = jnp.zeros(st_ref.shape, F32)

    for c in reversed(range(n_chunks)):
        rows = slice(c * CHUNK, (c + 1) * CHUNK)
        o, st = _ret_chunk(q_ref[rows, :], k_ref[rows, :], v_ref[rows, :], st_ref[...],
                           dm_ref[...], qd_ref[...], kd_ref[...], cd_ref[...])
        st_ref[...] = st
        o = o + of_ref[rows, :]
        oc = o - jnp.mean(o, axis=-1, keepdims=True)
        o = oc * lax.rsqrt(jnp.mean(oc * oc, axis=-1, keepdims=True) + NORM_EPS) * ng_ref[...]
        gate = g_ref[rows, :].astype(F32)
        o_ref[rows, :] = (o * (gate * jax.nn.sigmoid(gate))).astype(o_ref.dtype)


def _retention(cqr, ckr, z3, tabs, norm_g, *, ts):
    b, s, _ = cqr.shape
    ts = _tile(s, ts)
    n = s // ts
    dv = RET_V_DIM

    def seq(width, off, rev):
        if rev:
            return pl.BlockSpec((None, ts, width), lambda bi, h, i: (bi, n - 1 - i, off // width + h))
        return pl.BlockSpec((None, ts, width), lambda bi, h, i: (bi, i, off // width + h))

    def per_head(r, c):
        return pl.BlockSpec((None, r, c), lambda bi, h, i: (h, 0, 0))

    def tab_specs():
        return [per_head(CHUNK, CHUNK), per_head(CHUNK, 1), per_head(1, CHUNK), per_head(1, dv)]

    sem = _params("parallel", "parallel", "arbitrary")
    o_fwd = pl.pallas_call(
        functools.partial(_ret_fwd_kernel, n_chunks=ts // CHUNK),
        grid=(b, RET_HEADS, n),
        in_specs=[seq(HEAD_DIM, 0, False), seq(HEAD_DIM, 0, False), seq(dv, OFF_C_V, False)] + tab_specs(),
        out_specs=seq(dv, 0, False),
        out_shape=jax.ShapeDtypeStruct((b, s, RET_V_W), F32),
        scratch_shapes=[pltpu.VMEM((HEAD_DIM, dv), F32)],
        compiler_params=sem,
        name="retention_fwd",
    )(cqr, ckr, z3, *tabs[0])
    return pl.pallas_call(
        functools.partial(_ret_bwd_kernel, n_chunks=ts // CHUNK),
        grid=(b, RET_HEADS, n),
        in_specs=[seq(HEAD_DIM, 0, True), seq(HEAD_DIM, 0, True), seq(dv, OFF_C_V, True), seq(dv, OFF_C_G, True),
                  seq(dv, 0, True)] + tab_specs() + [pl.BlockSpec((1, dv), lambda bi, h, i: (0, h))],
        out_specs=seq(dv, 0, True),
        out_shape=jax.ShapeDtypeStruct((b, s, RET_V_W), BF16),
        scratch_shapes=[pltpu.VMEM((HEAD_DIM, dv), F32)],
        compiler_params=sem,
        name="retention_bwd",
    )(cqr, ckr, z3, z3, o_fwd, *tabs[1], norm_g.reshape(1, RET_V_W))


def _retention_tables(ret_decay):
    lg = -jnp.exp(ret_decay.astype(F32))
    idx = jnp.arange(CHUNK, dtype=F32)
    diff = idx[:, None] - idx[None, :]
    out = []
    for d, (mask, delta, qpow, kpow) in enumerate((
            (diff >= 0, diff, idx + 1.0, CHUNK - 1.0 - idx),
            (diff < 0, -diff, CHUNK - idx, idx))):
        l = lg[d][:, None, None]
        dm = jnp.where(mask, jnp.exp(l * jnp.where(mask, delta, 0.0)), 0.0)
        qd = jnp.exp(lg[d][:, None] * qpow)[:, :, None]
        kd = jnp.exp(lg[d][:, None] * kpow)[:, None, :]
        cd = jnp.broadcast_to(jnp.exp(lg[d] * CHUNK)[:, None, None], (RET_HEADS, 1, RET_V_DIM))
        out.append((dm, qd, kd, cd))
    return out


def _merge_kernel(a_ref, b_ref, c_ref, wa_ref, wb_ref, wc_ref, ga_ref, gb_ref, gc_ref, o_ref):
    acc = jax.nn.sigmoid(ga_ref[...].astype(F32)) * jnp.dot(a_ref[...], wa_ref[...], preferred_element_type=F32)
    acc = acc + jax.nn.sigmoid(gb_ref[...].astype(F32)) * jnp.dot(b_ref[...], wb_ref[...], preferred_element_type=F32)
    acc = acc + jax.nn.sigmoid(gc_ref[...].astype(F32)) * jnp.dot(c_ref[...], wc_ref[...], preferred_element_type=F32)
    o_ref[...] = acc.astype(o_ref.dtype)


def _merge(out_a, out_b, out_c, w_a, w_b, w_c, z, *, tm, tn):
    t, k = out_a.shape
    tm, tn = _tile(t, tm), _tile(D_MODEL, tn)
    lhs = pl.BlockSpec((tm, k), lambda i, j: (i, 0))
    rhs = pl.BlockSpec((k, tn), lambda i, j: (0, j))

    def gate(off):
        return pl.BlockSpec((tm, tn), lambda i, j: (i, off // tn + j))

    return pl.pallas_call(
        _merge_kernel,
        grid=(t // tm, D_MODEL // tn),
        in_specs=[lhs, lhs, lhs, rhs, rhs, rhs, gate(OFF_GATE_A), gate(OFF_GATE_B), gate(OFF_GATE_C)],
        out_specs=pl.BlockSpec((tm, tn), lambda i, j: (i, j)),
        out_shape=jax.ShapeDtypeStruct((t, D_MODEL), BF16),
        compiler_params=_params("parallel", "parallel"),
        name="branch_merge",
    )(out_a, out_b, out_c, w_a, w_b, w_c, z, z, z)


def _matmul_res_kernel(a_ref, w_ref, x_ref, o_ref):
    o_ref[...] = x_ref[...] + jnp.dot(a_ref[...], w_ref[...], preferred_element_type=F32)


def _matmul_res(a, w, x, *, tm, tn):
    t, k = a.shape
    n = w.shape[1]
    tm, tn = _tile(t, tm), _tile(n, tn)
    return pl.pallas_call(
        _matmul_res_kernel,
        grid=(t // tm, n // tn),
        in_specs=[
            pl.BlockSpec((tm, k), lambda i, j: (i, 0)),
            pl.BlockSpec((k, tn), lambda i, j: (0, j)),
            pl.BlockSpec((tm, tn), lambda i, j: (i, j)),
        ],
        out_specs=pl.BlockSpec((tm, tn), lambda i, j: (i, j)),
        out_shape=jax.ShapeDtypeStruct((t, n), F32),
        compiler_params=_params("parallel", "parallel"),
        name="out_proj_residual",
    )(a, w, x)


def _xattn_kernel(x_ref, g_ref, wq_ref, kv_ref, wo_ref, o_ref):
    x = x_ref[...]
    h = _rms(x, g_ref[...]).astype(BF16)
    q = (jnp.dot(h, wq_ref[...], preferred_element_type=F32) * (HEAD_DIM ** -0.5)).astype(BF16)
    heads = []
    for hh in range(XATTN_HEADS):
        cols = slice(hh * HEAD_DIM, (hh + 1) * HEAD_DIM)
        k = kv_ref[:, cols]
        v = kv_ref[:, XATTN_W + hh * HEAD_DIM:XATTN_W + (hh + 1) * HEAD_DIM]
        s = lax.dot_general(q[:, cols], k, _NT, preferred_element_type=F32)
        p = jnp.exp(s - jnp.max(s, axis=-1, keepdims=True))
        o = jnp.dot(p.astype(BF16), v, preferred_element_type=F32) / jnp.sum(p, axis=-1, keepdims=True)
        heads.append(o.astype(BF16))
    o_ref[...] = x + jnp.dot(jnp.concatenate(heads, axis=1), wo_ref[...], preferred_element_type=F32)


def _xattn(x3, norm_g, wq, kv3, wo, *, tm):
    b, s, d = x3.shape
    m = kv3.shape[1]
    tm = _tile(s, tm)
    return pl.pallas_call(
        _xattn_kernel,
        grid=(b, s // tm),
        in_specs=[
            pl.BlockSpec((None, tm, d), lambda bi, i: (bi, i, 0)),
            pl.BlockSpec((1, d), lambda bi, i: (0, 0)),
            pl.BlockSpec((d, XATTN_W), lambda bi, i: (0, 0)),
            pl.BlockSpec((None, m, 2 * XATTN_W), lambda bi, i: (bi, 0, 0)),
            pl.BlockSpec((XATTN_W, d), lambda bi, i: (0, 0)),
        ],
        out_specs=pl.BlockSpec((None, tm, d), lambda bi, i: (bi, i, 0)),
        out_shape=jax.ShapeDtypeStruct((b, s, d), F32),
        compiler_params=_params("parallel", "parallel"),
        name="memory_cross_attention",
    )(x3, norm_g.reshape(1, d), wq, kv3, wo)


def _router_kernel(x_ref, g_ref, wr_ref, br_ref, h_ref, ct_ref):
    h = _rms(x_ref[...], g_ref[...])
    h_ref[...] = h.astype(BF16)
    lt = lax.dot_general(wr_ref[...], h, _NT, precision=lax.Precision.HIGHEST,
                         preferred_element_type=F32) + br_ref[...]
    gl = [lt[i:i + 1, :] for i in range(N_GROUPS)]
    gmax = jnp.maximum(jnp.maximum(gl[0], gl[1]), jnp.maximum(gl[2], gl[3]))
    denom = sum(jnp.exp(v - gmax) for v in gl)
    g_prob = 1.0 / denom
    g_idx = jnp.where(gl[0] == gmax, 0, jnp.where(gl[1] == gmax, 1, jnp.where(gl[2] == gmax, 2, 3)))
    sel = []
    for j in range(EXPERTS_PER_GROUP):
        rows = [lt[N_GROUPS + g * EXPERTS_PER_GROUP + j:N_GROUPS + g * EXPERTS_PER_GROUP + j + 1, :]
                for g in range(N_GROUPS)]
        sel.append(jnp.where(g_idx == 0, rows[0], jnp.where(g_idx == 1, rows[1],
                                                            jnp.where(g_idx == 2, rows[2], rows[3]))))
    top1 = jnp.maximum(jnp.maximum(sel[0], sel[1]), jnp.maximum(sel[2], sel[3]))
    i1 = jnp.where(sel[0] == top1, 0, jnp.where(sel[1] == top1, 1, jnp.where(sel[2] == top1, 2, 3)))
    rest = [jnp.where(i1 == j, -jnp.inf, sel[j]) for j in range(EXPERTS_PER_GROUP)]
    top2 = jnp.maximum(jnp.maximum(rest[0], rest[1]), jnp.maximum(rest[2], rest[3]))
    i2 = jnp.where(rest[0] == top2, 0, jnp.where(rest[1] == top2, 1, jnp.where(rest[2] == top2, 2, 3)))
    e2 = jnp.exp(top2 - top1)
    w1 = g_prob / (1.0 + e2)
    w2 = g_prob * e2 / (1.0 + e2)
    id1 = g_idx * EXPERTS_PER_GROUP + i1
    id2 = g_idx * EXPERTS_PER_GROUP + i2
    eid = lax.broadcasted_iota(jnp.int32, ct_ref.shape, 0)
    ct_ref[...] = jnp.where(eid == id1, w1, 0.0) + jnp.where(eid == id2, w2, 0.0)


def _router(x, norm_g, wr_t, br, *, tm):
    t, d = x.shape
    tm = _tile(t, tm)
    return pl.pallas_call(
        _router_kernel,
        grid=(t // tm,),
        in_specs=[
            pl.BlockSpec((tm, d), lambda i: (i, 0)),
            pl.BlockSpec((1, d), lambda i: (0, 0)),
            pl.BlockSpec((ROUTER_ROWS, d), lambda i: (0, 0)),
            pl.BlockSpec((ROUTER_ROWS, 1), lambda i: (0, 0)),
        ],
        out_specs=[pl.BlockSpec((tm, d), lambda i: (i, 0)), pl.BlockSpec((N_EXPERTS, tm), lambda i: (0, i))],
        out_shape=[jax.ShapeDtypeStruct((t, d), BF16), jax.ShapeDtypeStruct((N_EXPERTS, t), F32)],
        compiler_params=_params("parallel"),
        name="router",
    )(x, norm_g.reshape(1, d), wr_t, br)


def _moe_kernel(h_ref, c_ref, x_ref, wg_ref, wu_ref, wd_ref, fg_ref, o_ref, acc_ref, *, final_norm):
    e = pl.program_id(1)

    @pl.when(e == 0)
    def _():
        acc_ref[...] = jnp.zeros(acc_ref.shape, F32)

    h = h_ref[...]
    a = jnp.dot(h, wg_ref[...], preferred_element_type=F32)
    hid = (a * jax.nn.sigmoid(a)) * jnp.dot(h, wu_ref[...], preferred_element_type=F32)
    c = c_ref[...]
    lane = lax.broadcasted_iota(jnp.int32, c.shape, 1)
    ce = jnp.sum(jnp.where(lane == e, c, 0.0), axis=-1, keepdims=True)
    acc_ref[...] += jnp.dot(hid.astype(BF16), wd_ref[...], preferred_element_type=F32) * ce

    @pl.when(e == pl.num_programs(1) - 1)
    def _():
        y = x_ref[...] + acc_ref[...]
        if final_norm:
            y = _rms(y, fg_ref[...])
        o_ref[...] = y


def _moe(h, comb, x, w_gate, w_up, w_down, final_g, *, tm, final_norm):
    t, d = x.shape
    tm = _tile(t, tm)
    return pl.pallas_call(
        functools.partial(_moe_kernel, final_norm=final_norm),
        grid=(t // tm, N_EXPERTS),
        in_specs=[
            pl.BlockSpec((tm, d), lambda i, e: (i, 0)),
            pl.BlockSpec((tm, N_EXPERTS), lambda i, e: (i, 0)),
            pl.BlockSpec((tm, d), lambda i, e: (i, 0)),
            pl.BlockSpec((None, d, EXPERT_FF), lambda i, e: (e, 0, 0)),
            pl.BlockSpec((None, d, EXPERT_FF), lambda i, e: (e, 0, 0)),
            pl.BlockSpec((None, EXPERT_FF, d), lambda i, e: (e, 0, 0)),
            pl.BlockSpec((1, d), lambda i, e: (0, 0)),
        ],
        out_specs=pl.BlockSpec((tm, d), lambda i, e: (i, 0)),
        out_shape=jax.ShapeDtypeStruct((t, d), F32),
        scratch_shapes=[pltpu.VMEM((tm, d), F32)],
        compiler_params=_params("parallel", "arbitrary"),
        name="experts",
    )(h, comb, x, w_gate, w_up, w_down, final_g.reshape(1, d))


def _rope_tables(seq_len):
    rows = seq_len // GRID_W
    row = jnp.repeat(jnp.arange(rows, dtype=F32), GRID_W)
    col = jnp.tile(jnp.arange(GRID_W, dtype=F32), rows)
    n_freq = HEAD_DIM // 4
    inv = ROPE_THETA ** (-jnp.arange(n_freq, dtype=F32) / n_freq)
    ar = row[:, None] * inv
    ac = col[:, None] * inv
    cos = jnp.concatenate([jnp.cos(ar), jnp.cos(ar), jnp.cos(ac), jnp.cos(ac)], axis=1)
    sin = jnp.concatenate([-jnp.sin(ar), jnp.sin(ar), -jnp.sin(ac), jnp.sin(ac)], axis=1)
    return cos, sin


def _prepare_layer(l, p):
    wr_t = jnp.concatenate([p["router_group_w"][l].T, p["router_expert_w"][l].T,
                            jnp.zeros((ROUTER_ROWS - N_GROUPS - N_EXPERTS, D_MODEL), F32)], axis=0)
    br = jnp.concatenate([p["router_group_b"][l], p["router_expert_b"][l].reshape(-1),
                          jnp.zeros((ROUTER_ROWS - N_GROUPS - N_EXPERTS,), F32)]).reshape(ROUTER_ROWS, 1)
    return dict(
        norm_mix_g=p["norm_mix_g"][l], w_in=p["w_in"][l].astype(BF16),
        sgu_norm_g=p["sgu_norm_g"][l], sgu_w=p["sgu_w"][l].astype(BF16), sgu_b_t=p["sgu_b"][l].T,
        q_norm_g=p["attn_q_norm_g"][l], k_norm_g=p["attn_k_norm_g"][l],
        ret_tabs=_retention_tables(p["ret_decay"][l]), ret_norm_g=p["ret_norm_g"][l],
        w_a=p["w_branch_a"][l].astype(BF16), w_b=p["w_branch_b"][l].astype(BF16),
        w_c=p["w_branch_c"][l].astype(BF16), w_out=p["w_out"][l].astype(BF16),
        xattn_norm_g=p["xattn_norm_g"][l], mem_norm_g=p["mem_norm_g"][l],
        wq=p["xattn_wq"][l].astype(BF16), wkv=p["xattn_wkv"][l].astype(BF16), wo=p["xattn_wo"][l].astype(BF16),
        ffn_norm_g=p["ffn_norm_g"][l], wr_t=wr_t, br=br,
        w_gate=p["expert_w_gate"][l].astype(BF16), w_up=p["expert_w_up"][l].astype(BF16),
        w_down=p["expert_w_down"][l].astype(BF16),
    )


def _trunk(x, mem, layers, final_norm_g):
    b, s, d = x.shape
    t = b * s
    m = mem.shape[1]
    cos_t, sin_t = _rope_tables(s)
    xt = x.reshape(t, d)
    memt = mem.reshape(b * m, d)
    for l, w in enumerate(layers):
        z = _rms_matmul(xt, w["norm_mix_g"], w["w_in"], tm=1024, tn=512)
        z3 = z.reshape(b, s, IN_WIDTH)
        out_a = _sgu(z3, w["sgu_norm_g"], w["sgu_w"], w["sgu_b_t"], ts=512)
        qr, kr, vaug, cqr, ckr, qn, kn = _prep(z3, cos_t, sin_t, w["q_norm_g"], w["k_norm_g"], ts=512)
        out_b = _flash(qr, kr, vaug, _score_bound(qn, kn), tq=256, tk=512)
        out_c = _retention(cqr, ckr, z3, w["ret_tabs"], w["ret_norm_g"], ts=512)
        merged = _merge(out_a.reshape(t, SGU_WIDTH), out_b.reshape(t, ATTN_Q_W), out_c.reshape(t, RET_V_W),
                        w["w_a"], w["w_b"], w["w_c"], z, tm=512, tn=512)
        xt = _matmul_res(merged, w["w_out"], xt, tm=512, tn=512)
        kv = _rms_matmul(memt, w["mem_norm_g"], w["wkv"], tm=256, tn=512)
        xt = _xattn(xt.reshape(b, s, d), w["xattn_norm_g"], w["wq"], kv.reshape(b, m, 2 * XATTN_W), w["wo"],
                    tm=512).reshape(t, d)
        h, ct = _router(xt, w["ffn_norm_g"], w["wr_t"], w["br"], tm=512)
        xt = _moe(h, ct.T, xt, w["w_gate"], w["w_up"], w["w_down"], final_norm_g, tm=512,
                  final_norm=(l == len(layers) - 1))
    return xt.reshape(b, s, d)


def kernel(x_prompt, x_sample, mem_prompt, mem_sample, norm_mix_g, w_in, sgu_norm_g, sgu_w, sgu_b, attn_q_norm_g, attn_k_norm_g, ret_decay, ret_norm_g, w_branch_a, w_branch_b, w_branch_c, w_out, xattn_norm_g, mem_norm_g, xattn_wq, xattn_wkv, xattn_wo, ffn_norm_g, router_group_w, router_group_b, router_expert_w, router_expert_b, expert_w_gate, expert_w_up, expert_w_down, final_norm_g):
    p = dict(norm_mix_g=norm_mix_g, w_in=w_in, sgu_norm_g=sgu_norm_g, sgu_w=sgu_w, sgu_b=sgu_b,
             attn_q_norm_g=attn_q_norm_g, attn_k_norm_g=attn_k_norm_g, ret_decay=ret_decay, ret_norm_g=ret_norm_g,
             w_branch_a=w_branch_a, w_branch_b=w_branch_b, w_branch_c=w_branch_c, w_out=w_out,
             xattn_norm_g=xattn_norm_g, mem_norm_g=mem_norm_g, xattn_wq=xattn_wq, xattn_wkv=xattn_wkv,
             xattn_wo=xattn_wo, ffn_norm_g=ffn_norm_g, router_group_w=router_group_w, router_group_b=router_group_b,
             router_expert_w=router_expert_w, router_expert_b=router_expert_b, expert_w_gate=expert_w_gate,
             expert_w_up=expert_w_up, expert_w_down=expert_w_down)
    layers = [_prepare_layer(l, p) for l in range(DEPTH)]
    y_prompt = _trunk(x_prompt, mem_prompt, layers, final_norm_g)
    y_sample = _trunk(x_sample, mem_sample, layers, final_norm_g)
    return (y_prompt, y_sample)
```

```python
import functools

import jax
import jax.numpy as jnp
import numpy as np
from jax import lax
from jax.experimental import pallas as pl
from jax.experimental.pallas import tpu as pltpu

F32 = jnp.float32
BF16 = jnp.bfloat16

D_MODEL = 2048
DEPTH = 2
GRID_W = 64
ROPE_THETA = 10000.0
CHUNK = 128
HEAD_DIM = 128
NORM_EPS = 1e-6
SGU_WIDTH = 1024
SGU_GROUPS = 4
SGU_GROUP_W = SGU_WIDTH // SGU_GROUPS
ATTN_HEADS = 8
ATTN_KV_HEADS = 2
ATTN_REP = ATTN_HEADS // ATTN_KV_HEADS
ATTN_Q_W = ATTN_HEADS * HEAD_DIM
ATTN_KV_W = ATTN_KV_HEADS * HEAD_DIM
RET_HEADS = 4
RET_V_DIM = 256
RET_QK_W = RET_HEADS * HEAD_DIM
RET_V_W = RET_HEADS * RET_V_DIM
XATTN_HEADS = 4
XATTN_W = XATTN_HEADS * HEAD_DIM
N_GROUPS = 4
EXPERTS_PER_GROUP = 4
N_EXPERTS = N_GROUPS * EXPERTS_PER_GROUP
EXPERT_FF = 512
ROUTER_ROWS = 24
ROUTE_ROWS = 8

OFF_A_U = 0
OFF_A_V = OFF_A_U + SGU_WIDTH
OFF_B_Q = OFF_A_V + SGU_WIDTH
OFF_B_K = OFF_B_Q + ATTN_Q_W
OFF_B_V = OFF_B_K + ATTN_KV_W
OFF_C_Q = OFF_B_V + ATTN_KV_W
OFF_C_K = OFF_C_Q + RET_QK_W
OFF_C_V = OFF_C_K + RET_QK_W
OFF_C_G = OFF_C_V + RET_V_W
OFF_GATE_A = OFF_C_G + RET_V_W
OFF_GATE_B = OFF_GATE_A + D_MODEL
OFF_GATE_C = OFF_GATE_B + D_MODEL
IN_WIDTH = OFF_GATE_C + D_MODEL

V7X_VMEM_LIMIT_BYTES = 56 * 1024 * 1024

_NT = (((1,), (1,)), ((), ()))
LOG2_E = 1.4426950408889634
FIXED_SHIFT_LIMIT = 60.0


def _tile(n, pref):
    t = min(pref, n)
    while n % t:
        t //= 2
    return t


def _params(*sem):
    return pltpu.CompilerParams(dimension_semantics=sem, vmem_limit_bytes=V7X_VMEM_LIMIT_BYTES)


def _rms(x, g):
    return x * lax.rsqrt(jnp.mean(x * x, axis=-1, keepdims=True) + NORM_EPS) * g


def _rms_matmul_kernel(x_ref, g_ref, w_ref, o_ref, h_ref):
    @pl.when(pl.program_id(1) == 0)
    def _():
        h_ref[...] = _rms(x_ref[...], g_ref[...]).astype(BF16)

    o_ref[...] = jnp.dot(h_ref[...], w_ref[...], preferred_element_type=F32).astype(o_ref.dtype)


def _rms_matmul(x, g, w, *, tm, tn):
    t, d = x.shape
    n = w.shape[1]
    tm, tn = _tile(t, tm), _tile(n, tn)
    return pl.pallas_call(
        _rms_matmul_kernel,
        grid=(t // tm, n // tn),
        in_specs=[
            pl.BlockSpec((tm, d), lambda i, j: (i, 0)),
            pl.BlockSpec((1, d), lambda i, j: (0, 0)),
            pl.BlockSpec((d, tn), lambda i, j: (0, j)),
        ],
        out_specs=pl.BlockSpec((tm, tn), lambda i, j: (i, j)),
        out_shape=jax.ShapeDtypeStruct((t, n), BF16),
        scratch_shapes=[pltpu.VMEM((tm, d), BF16)],
        compiler_params=_params("parallel", "arbitrary"),
        name="rms_matmul",
    )(x, g.reshape(1, d), w)


def _gelu(x):
    return 0.5 * x * (1.0 + lax.erf(x * (2.0 ** -0.5)))


def _sgu_kernel(u_ref, v_ref, ng_ref, ws_ref, bs_ref, o_ref, *, n_chunks):
    for c in range(n_chunks):
        rows = slice(c * CHUNK, (c + 1) * CHUNK)
        u = _gelu(u_ref[rows, :].astype(F32))
        v = _gelu(v_ref[rows, :].astype(F32))
        vc = v - jnp.mean(v, axis=-1, keepdims=True)
        vn = vc * lax.rsqrt(jnp.mean(vc * vc, axis=-1, keepdims=True) + NORM_EPS) * ng_ref[...]
        vn = vn.astype(BF16)
        for g in range(SGU_GROUPS):
            cols = slice(g * SGU_GROUP_W, (g + 1) * SGU_GROUP_W)
            s = jnp.dot(ws_ref[g], vn[:, cols], preferred_element_type=F32) + bs_ref[:, g:g + 1]
            o_ref[rows, cols] = (u[:, cols] * s).astype(o_ref.dtype)


def _sgu(z3, norm_g, w_s, b_s_t, *, ts):
    b, s, _ = z3.shape
    ts = _tile(s, ts)
    return pl.pallas_call(
        functools.partial(_sgu_kernel, n_chunks=ts // CHUNK),
        grid=(b, s // ts),
        in_specs=[
            pl.BlockSpec((None, ts, SGU_WIDTH), lambda bi, i: (bi, i, OFF_A_U // SGU_WIDTH)),
            pl.BlockSpec((None, ts, SGU_WIDTH), lambda bi, i: (bi, i, OFF_A_V // SGU_WIDTH)),
            pl.BlockSpec((1, SGU_WIDTH), lambda bi, i: (0, 0)),
            pl.BlockSpec((SGU_GROUPS, CHUNK, CHUNK), lambda bi, i: (0, 0, 0)),
            pl.BlockSpec((CHUNK, SGU_GROUPS), lambda bi, i: (0, 0)),
        ],
        out_specs=pl.BlockSpec((None, ts, SGU_WIDTH), lambda bi, i: (bi, i, 0)),
        out_shape=jax.ShapeDtypeStruct((b, s, SGU_WIDTH), BF16),
        compiler_params=_params("parallel", "parallel"),
        name="sgu",
    )(z3, z3, norm_g.reshape(1, SGU_WIDTH), w_s, b_s_t)


def _rope(x, cos, sin_signed):
    lane = lax.broadcasted_iota(jnp.int32, x.shape, 1)
    first_half = (lane % (HEAD_DIM // 2)) < (HEAD_DIM // 4)
    partner = jnp.where(first_half, pltpu.roll(x, HEAD_DIM - HEAD_DIM // 4, 1), pltpu.roll(x, HEAD_DIM // 4, 1))
    return x * cos + partner * sin_signed


def _prep_kernel(bq_ref, bk_ref, bv_ref, cq_ref, ck_ref, cos_ref, sin_ref, qg_ref, kg_ref,
                 oq_ref, ok_ref, ov_ref, ocq_ref, ock_ref, qn_ref, kn_ref):
    cos = cos_ref[...]
    sin = sin_ref[...]
    scale = HEAD_DIM ** -0.5
    q_ss = None
    for h in range(ATTN_HEADS):
        cols = slice(h * HEAD_DIM, (h + 1) * HEAD_DIM)
        x = _rope(_rms(bq_ref[:, cols].astype(F32), qg_ref[...]), cos, sin) * (scale * LOG2_E)
        oq_ref[:, cols] = x.astype(BF16)
        ss = jnp.sum(x * x, axis=-1, keepdims=True)
        q_ss = ss if q_ss is None else jnp.maximum(q_ss, ss)
    k_ss = None
    for h in range(ATTN_KV_HEADS):
        cols = slice(h * HEAD_DIM, (h + 1) * HEAD_DIM)
        x = _rope(_rms(bk_ref[:, cols].astype(F32), kg_ref[...]), cos, sin)
        ok_ref[:, cols] = x.astype(BF16)
        ss = jnp.sum(x * x, axis=-1, keepdims=True)
        k_ss = ss if k_ss is None else jnp.maximum(k_ss, ss)
        ov_ref[:, 2 * h * HEAD_DIM:(2 * h + 1) * HEAD_DIM] = bv_ref[:, cols]
        ov_ref[:, (2 * h + 1) * HEAD_DIM:(2 * h + 2) * HEAD_DIM] = jnp.ones((bv_ref.shape[0], HEAD_DIM), BF16)
    qn_ref[...] = jnp.broadcast_to(jnp.max(q_ss, axis=0, keepdims=True), qn_ref.shape)
    kn_ref[...] = jnp.broadcast_to(jnp.max(k_ss, axis=0, keepdims=True), kn_ref.shape)
    for h in range(RET_HEADS):
        cols = slice(h * HEAD_DIM, (h + 1) * HEAD_DIM)
        ocq_ref[:, cols] = (_rope(cq_ref[:, cols].astype(F32), cos, sin) * scale).astype(BF16)
        ock_ref[:, cols] = _rope(ck_ref[:, cols].astype(F32), cos, sin).astype(BF16)


def _prep(z3, cos_t, sin_t, q_norm_g, k_norm_g, *, ts):
    b, s, _ = z3.shape
    ts = _tile(s, ts)
    n = s // ts

    def zspec(width, off):
        return pl.BlockSpec((None, ts, width), lambda bi, i: (bi, i, off // width))

    def ospec(width):
        return pl.BlockSpec((None, ts, width), lambda bi, i: (bi, i, 0))

    tab = pl.BlockSpec((ts, HEAD_DIM), lambda bi, i: (i, 0))
    gain = pl.BlockSpec((1, HEAD_DIM), lambda bi, i: (0, 0))
    nspec = pl.BlockSpec((None, None, 8, HEAD_DIM), lambda bi, i: (bi, i, 0, 0))
    widths = (ATTN_Q_W, ATTN_KV_W, 2 * ATTN_KV_W, RET_QK_W, RET_QK_W)
    return pl.pallas_call(
        _prep_kernel,
        grid=(b, n),
        in_specs=[zspec(ATTN_Q_W, OFF_B_Q), zspec(ATTN_KV_W, OFF_B_K), zspec(ATTN_KV_W, OFF_B_V),
                  zspec(RET_QK_W, OFF_C_Q), zspec(RET_QK_W, OFF_C_K), tab, tab, gain, gain],
        out_specs=[ospec(w) for w in widths] + [nspec, nspec],
        out_shape=[jax.ShapeDtypeStruct((b, s, w), BF16) for w in widths]
        + [jax.ShapeDtypeStruct((b, n, 8, HEAD_DIM), F32)] * 2,
        compiler_params=_params("parallel", "parallel"),
        name="qk_prep",
    )(z3, z3, z3, z3, z3, cos_t, sin_t, q_norm_g.reshape(1, HEAD_DIM), k_norm_g.reshape(1, HEAD_DIM))


def _flash_kernel(b2_ref, q_ref, k_ref, v_ref, o_ref, qc_ref, acc_ref, m_ref, *, tq, tk):
    for r in range(ATTN_REP):
        qc_ref[r * tq:(r + 1) * tq, :] = q_ref[:, r * HEAD_DIM:(r + 1) * HEAD_DIM]
    b2 = b2_ref[pl.program_id(0)]
    n_kv = k_ref.shape[0] // tk
    acc_ref[...] = jnp.zeros(acc_ref.shape, F32)

    def tiles(j):
        off = pl.multiple_of(j * tk, tk)
        k = k_ref[pl.ds(off, tk), :]
        return lax.dot_general(qc_ref[...], k, _NT, preferred_element_type=F32), v_ref[pl.ds(off, tk), :]

    @pl.when(b2 <= FIXED_SHIFT_LIMIT)
    def _():
        def body(j, carry):
            s, v = tiles(j)
            acc_ref[...] += jnp.dot(jnp.exp2(s - b2).astype(BF16), v, preferred_element_type=F32)
            return carry

        lax.fori_loop(0, n_kv, body, 0, unroll=4 if n_kv % 4 == 0 else 1)

    @pl.when(b2 > FIXED_SHIFT_LIMIT)
    def _():
        m_ref[...] = jnp.full(m_ref.shape, -jnp.inf, F32)

        def body(j, carry):
            s, v = tiles(j)
            m_old = m_ref[...]
            m_new = jnp.maximum(m_old, jnp.max(s, axis=-1, keepdims=True))
            p = jnp.exp2(s - m_new).astype(BF16)
            acc_ref[...] = jnp.exp2(m_old - m_new) * acc_ref[...] + jnp.dot(p, v, preferred_element_type=F32)
            m_ref[...] = m_new
            return carry

        lax.fori_loop(0, n_kv, body, 0)

    out = acc_ref[:, :HEAD_DIM] / acc_ref[:, HEAD_DIM:]
    for r in range(ATTN_REP):
        o_ref[:, r * HEAD_DIM:(r + 1) * HEAD_DIM] = out[r * tq:(r + 1) * tq, :].astype(o_ref.dtype)


def _flash(qr, kr, vaug, bound2, *, tq, tk):
    b, s, _ = qr.shape
    tq, tk = _tile(s, tq), _tile(s, tk)
    gw = ATTN_REP * HEAD_DIM
    grid_spec = pltpu.PrefetchScalarGridSpec(
        num_scalar_prefetch=1,
        grid=(b, ATTN_KV_HEADS, s // tq),
        in_specs=[
            pl.BlockSpec((None, tq, gw), lambda bi, g, i, b2: (bi, i, g)),
            pl.BlockSpec((None, s, HEAD_DIM), lambda bi, g, i, b2: (bi, 0, g)),
            pl.BlockSpec((None, s, 2 * HEAD_DIM), lambda bi, g, i, b2: (bi, 0, g)),
        ],
        out_specs=pl.BlockSpec((None, tq, gw), lambda bi, g, i, b2: (bi, i, g)),
        scratch_shapes=[
            pltpu.VMEM((ATTN_REP * tq, HEAD_DIM), BF16),
            pltpu.VMEM((ATTN_REP * tq, 2 * HEAD_DIM), F32),
            pltpu.VMEM((ATTN_REP * tq, 1), F32),
        ],
    )
    return pl.pallas_call(
        functools.partial(_flash_kernel, tq=tq, tk=tk),
        grid_spec=grid_spec,
        out_shape=jax.ShapeDtypeStruct((b, s, ATTN_Q_W), BF16),
        compiler_params=_params("parallel", "parallel", "arbitrary"),
        name="flash_attention",
    )(bound2, qr, kr, vaug)


def _score_bound(qn, kn):
    return jnp.sqrt(jnp.max(qn[:, :, 0, 0], axis=1) * jnp.max(kn[:, :, 0, 0], axis=1)) * 1.01


def _ret_chunk(q, k, v, st, dm, qd, kd, cd):
    s = lax.dot_general(q, k, _NT, preferred_element_type=F32) * dm
    o = jnp.dot(s.astype(BF16), v, preferred_element_type=F32)
    o = o + qd * jnp.dot(q, st.astype(BF16), preferred_element_type=F32)
    kt = (k.astype(F32).T * kd).astype(BF16)
    return o, cd * st + jnp.dot(kt, v, preferred_element_type=F32)


def _ret_fwd_kernel(q_ref, k_ref, v_ref, dm_ref, qd_ref, kd_ref, cd_ref, o_ref, st_ref, *, n_chunks):
    @pl.when(pl.program_id(2) == 0)
    def _():
        st_ref[...] = jnp.zeros(st_ref.shape, F32)

    for c in range(n_chunks):
        rows = slice(c * CHUNK, (c + 1) * CHUNK)
        o, st = _ret_chunk(q_ref[rows, :], k_ref[rows, :], v_ref[rows, :], st_ref[...],
                           dm_ref[...], qd_ref[...], kd_ref[...], cd_ref[...])
        o_ref[rows, :] = o
        st_ref[...] = st


def _ret_bwd_kernel(q_ref, k_ref, v_ref, g_ref, of_ref, dm_ref, qd_ref, kd_ref, cd_ref, ng_ref,
                    o_ref, st_ref, *, n_chunks):
    @pl.when(pl.program_id(2) == 0)
    def _():
        st_ref[...] = jnp.zeros(st_ref.shape, F32)

    for c in reversed(range(n_chunks)):
        rows = slice(c * CHUNK, (c + 1) * CHUNK)
        o, st = _ret_chunk(q_ref[rows, :], k_ref[rows, :], v_ref[rows, :], st_ref[...],
                           dm_ref[...], qd_ref[...], kd_ref[...], cd_ref[...])
        st_ref[...] = st
        o = o + of_ref[rows, :]
        oc = o - jnp.mean(o, axis=-1, keepdims=True)
        o = oc * lax.rsqrt(jnp.mean(oc * oc, axis=-1, keepdims=True) + NORM_EPS) * ng_ref[...]
        gate = g_ref[rows, :].astype(F32)
        o_ref[rows, :] = (o * (gate * jax.nn.sigmoid(gate))).astype(o_ref.dtype)


def _retention(cqr, ckr, z3, tabs, norm_g, *, ts):
    b, s, _ = cqr.shape
    ts = _tile(s, ts)
    n = s // ts
    dv = RET_V_DIM

    def seq(width, off, rev):
        if rev:
            return pl.BlockSpec((None, ts, width), lambda bi, h, i: (bi, n - 1 - i, off // width + h))
        return pl.BlockSpec((None, ts, width), lambda bi, h, i: (bi, i, off // width + h))

    def per_head(r, c):
        return pl.BlockSpec((None, r, c), lambda bi, h, i: (h, 0, 0))

    def tab_specs():
        return [per_head(CHUNK, CHUNK), per_head(CHUNK, 1), per_head(1, CHUNK), per_head(1, dv)]

    sem = _params("parallel", "parallel", "arbitrary")
    o_fwd = pl.pallas_call(
        functools.partial(_ret_fwd_kernel, n_chunks=ts // CHUNK),
        grid=(b, RET_HEADS, n),
        in_specs=[seq(HEAD_DIM, 0, False), seq(HEAD_DIM, 0, False), seq(dv, OFF_C_V, False)] + tab_specs(),
        out_specs=seq(dv, 0, False),
        out_shape=jax.ShapeDtypeStruct((b, s, RET_V_W), F32),
        scratch_shapes=[pltpu.VMEM((HEAD_DIM, dv), F32)],
        compiler_params=sem,
        name="retention_fwd",
    )(cqr, ckr, z3, *tabs[0])
    return pl.pallas_call(
        functools.partial(_ret_bwd_kernel, n_chunks=ts // CHUNK),
        grid=(b, RET_HEADS, n),
        in_specs=[seq(HEAD_DIM, 0, True), seq(HEAD_DIM, 0, True), seq(dv, OFF_C_V, True), seq(dv, OFF_C_G, True),
                  seq(dv, 0, True)] + tab_specs() + [pl.BlockSpec((1, dv), lambda bi, h, i: (0, h))],
        out_specs=seq(dv, 0, True),
        out_shape=jax.ShapeDtypeStruct((b, s, RET_V_W), BF16),
        scratch_shapes=[pltpu.VMEM((HEAD_DIM, dv), F32)],
        compiler_params=sem,
        name="retention_bwd",
    )(cqr, ckr, z3, z3, o_fwd, *tabs[1], norm_g.reshape(1, RET_V_W))


def _retention_tables(ret_decay):
    lg = -jnp.exp(ret_decay.astype(F32))
    idx = jnp.arange(CHUNK, dtype=F32)
    diff = idx[:, None] - idx[None, :]
    out = []
    for d, (mask, delta, qpow, kpow) in enumerate((
            (diff >= 0, diff, idx + 1.0, CHUNK - 1.0 - idx),
            (diff < 0, -diff, CHUNK - idx, idx))):
        l = lg[d][:, None, None]
        dm = jnp.where(mask, jnp.exp(l * jnp.where(mask, delta, 0.0)), 0.0)
        qd = jnp.exp(lg[d][:, None] * qpow)[:, :, None]
        kd = jnp.exp(lg[d][:, None] * kpow)[:, None, :]
        cd = jnp.broadcast_to(jnp.exp(lg[d] * CHUNK)[:, None, None], (RET_HEADS, 1, RET_V_DIM))
        out.append((dm, qd, kd, cd))
    return out


def _merge_kernel(a_ref, b_ref, c_ref, wa_ref, wb_ref, wc_ref, ga_ref, gb_ref, gc_ref, o_ref):
    acc = jax.nn.sigmoid(ga_ref[...].astype(F32)) * jnp.dot(a_ref[...], wa_ref[...], preferred_element_type=F32)
    acc = acc + jax.nn.sigmoid(gb_ref[...].astype(F32)) * jnp.dot(b_ref[...], wb_ref[...], preferred_element_type=F32)
    acc = acc + jax.nn.sigmoid(gc_ref[...].astype(F32)) * jnp.dot(c_ref[...], wc_ref[...], preferred_element_type=F32)
    o_ref[...] = acc.astype(o_ref.dtype)


def _merge(out_a, out_b, out_c, w_a, w_b, w_c, z, *, tm, tn):
    t, k = out_a.shape
    tm, tn = _tile(t, tm), _tile(D_MODEL, tn)
    lhs = pl.BlockSpec((tm, k), lambda i, j: (i, 0))
    rhs = pl.BlockSpec((k, tn), lambda i, j: (0, j))

    def gate(off):
        return pl.BlockSpec((tm, tn), lambda i, j: (i, off // tn + j))

    return pl.pallas_call(
        _merge_kernel,
        grid=(t // tm, D_MODEL // tn),
        in_specs=[lhs, lhs, lhs, rhs, rhs, rhs, gate(OFF_GATE_A), gate(OFF_GATE_B), gate(OFF_GATE_C)],
        out_specs=pl.BlockSpec((tm, tn), lambda i, j: (i, j)),
        out_shape=jax.ShapeDtypeStruct((t, D_MODEL), BF16),
        compiler_params=_params("parallel", "parallel"),
        name="branch_merge",
    )(out_a, out_b, out_c, w_a, w_b, w_c, z, z, z)


def _matmul_res_kernel(a_ref, w_ref, x_ref, o_ref):
    o_ref[...] = x_ref[...] + jnp.dot(a_ref[...], w_ref[...], preferred_element_type=F32)


def _matmul_res(a, w, x, *, tm, tn):
    t, k = a.shape
    n = w.shape[1]
    tm, tn = _tile(t, tm), _tile(n, tn)
    return pl.pallas_call(
        _matmul_res_kernel,
        grid=(t // tm, n // tn),
        in_specs=[
            pl.BlockSpec((tm, k), lambda i, j: (i, 0)),
            pl.BlockSpec((k, tn), lambda i, j: (0, j)),
            pl.BlockSpec((tm, tn), lambda i, j: (i, j)),
        ],
        out_specs=pl.BlockSpec((tm, tn), lambda i, j: (i, j)),
        out_shape=jax.ShapeDtypeStruct((t, n), F32),
        compiler_params=_params("parallel", "parallel"),
        name="out_proj_residual",
    )(a, w, x)


def _xattn_kernel(x_ref, g_ref, wq_ref, kv_ref, wo_ref, o_ref):
    x = x_ref[...]
    h = _rms(x, g_ref[...]).astype(BF16)
    q = (jnp.dot(h, wq_ref[...], preferred_element_type=F32) * (HEAD_DIM ** -0.5)).astype(BF16)
    heads = []
    for hh in range(XATTN_HEADS):
        cols = slice(hh * HEAD_DIM, (hh + 1) * HEAD_DIM)
        k = kv_ref[:, cols]
        v = kv_ref[:, XATTN_W + hh * HEAD_DIM:XATTN_W + (hh + 1) * HEAD_DIM]
        s = lax.dot_general(q[:, cols], k, _NT, preferred_element_type=F32)
        p = jnp.exp(s - jnp.max(s, axis=-1, keepdims=True))
        o = jnp.dot(p.astype(BF16), v, preferred_element_type=F32) / jnp.sum(p, axis=-1, keepdims=True)
        heads.append(o.astype(BF16))
    o_ref[...] = x + jnp.dot(jnp.concatenate(heads, axis=1), wo_ref[...], preferred_element_type=F32)


def _xattn(x3, norm_g, wq, kv3, wo, *, tm):
    b, s, d = x3.shape
    m = kv3.shape[1]
    tm = _tile(s, tm)
    return pl.pallas_call(
        _xattn_kernel,
        grid=(b, s // tm),
        in_specs=[
            pl.BlockSpec((None, tm, d), lambda bi, i: (bi, i, 0)),
            pl.BlockSpec((1, d), lambda bi, i: (0, 0)),
            pl.BlockSpec((d, XATTN_W), lambda bi, i: (0, 0)),
            pl.BlockSpec((None, m, 2 * XATTN_W), lambda bi, i: (bi, 0, 0)),
            pl.BlockSpec((XATTN_W, d), lambda bi, i: (0, 0)),
        ],
        out_specs=pl.BlockSpec((None, tm, d), lambda bi, i: (bi, i, 0)),
        out_shape=jax.ShapeDtypeStruct((b, s, d), F32),
        compiler_params=_params("parallel", "parallel"),
        name="memory_cross_attention",
    )(x3, norm_g.reshape(1, d), wq, kv3, wo)


def _router_kernel(x_ref, g_ref, wr_ref, br_ref, rt_ref):
    h = _rms(x_ref[...], g_ref[...])
    lt = lax.dot_general(wr_ref[...], h, _NT, precision=lax.Precision.HIGHEST,
                         preferred_element_type=F32) + br_ref[...]
    gl = [lt[i:i + 1, :] for i in range(N_GROUPS)]
    gmax = jnp.maximum(jnp.maximum(gl[0], gl[1]), jnp.maximum(gl[2], gl[3]))
    denom = sum(jnp.exp(v - gmax) for v in gl)
    g_prob = 1.0 / denom
    g_idx = jnp.where(gl[0] == gmax, 0, jnp.where(gl[1] == gmax, 1, jnp.where(gl[2] == gmax, 2, 3)))
    sel = []
    for j in range(EXPERTS_PER_GROUP):
        rows = [lt[N_GROUPS + g * EXPERTS_PER_GROUP + j:N_GROUPS + g * EXPERTS_PER_GROUP + j + 1, :]
                for g in range(N_GROUPS)]
        sel.append(jnp.where(g_idx == 0, rows[0], jnp.where(g_idx == 1, rows[1],
                                                            jnp.where(g_idx == 2, rows[2], rows[3]))))
    top1 = jnp.maximum(jnp.maximum(sel[0], sel[1]), jnp.maximum(sel[2], sel[3]))
    i1 = jnp.where(sel[0] == top1, 0, jnp.where(sel[1] == top1, 1, jnp.where(sel[2] == top1, 2, 3)))
    rest = [jnp.where(i1 == j, -jnp.inf, sel[j]) for j in range(EXPERTS_PER_GROUP)]
    top2 = jnp.maximum(jnp.maximum(rest[0], rest[1]), jnp.maximum(rest[2], rest[3]))
    i2 = jnp.where(rest[0] == top2, 0, jnp.where(rest[1] == top2, 1, jnp.where(rest[2] == top2, 2, 3)))
    e2 = jnp.exp(top2 - top1)
    w1 = g_prob / (1.0 + e2)
    w2 = g_prob * e2 / (1.0 + e2)
    row = lax.broadcasted_iota(jnp.int32, rt_ref.shape, 0)
    rt_ref[...] = (jnp.where(row == i1, w1, 0.0) + jnp.where(row == i2, w2, 0.0)
                   + jnp.where(row == EXPERTS_PER_GROUP, g_idx.astype(F32), 0.0))


def _router(x, norm_g, wr_t, br, *, tm):
    t, d = x.shape
    tm = _tile(t, tm)
    return pl.pallas_call(
        _router_kernel,
        grid=(t // tm,),
        in_specs=[
            pl.BlockSpec((tm, d), lambda i: (i, 0)),
            pl.BlockSpec((1, d), lambda i: (0, 0)),
            pl.BlockSpec((ROUTER_ROWS, d), lambda i: (0, 0)),
            pl.BlockSpec((ROUTER_ROWS, 1), lambda i: (0, 0)),
        ],
        out_specs=pl.BlockSpec((ROUTE_ROWS, tm), lambda i: (0, i)),
        out_shape=jax.ShapeDtypeStruct((ROUTE_ROWS, t), F32),
        compiler_params=_params("parallel"),
        name="router",
    )(x, norm_g.reshape(1, d), wr_t, br)


def _row_gather_kernel(idx_ref, src_ref, dst_ref, sem, *, rows):
    base = pl.program_id(0) * rows

    def start(r, carry):
        pltpu.make_async_copy(src_ref.at[pl.ds(idx_ref[0, r], 1), :], dst_ref.at[pl.ds(base + r, 1), :], sem).start()
        return carry

    lax.fori_loop(0, rows, start, 0, unroll=8)
    pltpu.make_async_copy(src_ref.at[pl.ds(0, rows), :], dst_ref.at[pl.ds(base, rows), :], sem).wait()


def _row_gather(src, idx, *, rows):
    n = idx.shape[0]
    d = src.shape[1]
    rows = _tile(n, rows)
    return pl.pallas_call(
        functools.partial(_row_gather_kernel, rows=rows),
        grid=(n // rows,),
        in_specs=[
            pl.BlockSpec((None, 1, rows), lambda i: (i, 0, 0), memory_space=pltpu.SMEM),
            pl.BlockSpec(memory_space=pl.ANY),
        ],
        out_specs=pl.BlockSpec(memory_space=pl.ANY),
        out_shape=jax.ShapeDtypeStruct((n, d), src.dtype),
        scratch_shapes=[pltpu.SemaphoreType.DMA(())],
        compiler_params=_params("arbitrary"),
        name="row_gather",
    )(idx.reshape(n // rows, 1, rows), src)


def _moe_kernel(tg_ref, nu_ref, x_ref, w_ref, ng_ref, wg_ref, wu_ref, wd_ref, fg_ref, o_ref, h_ref, acc_ref, *,
                final_norm):
    i = pl.program_id(0)
    j = pl.program_id(1)
    used = i < nu_ref[0]

    @pl.when(j == 0)
    def _():
        acc_ref[...] = jnp.zeros(acc_ref.shape, F32)
        h_ref[...] = _rms(x_ref[...], ng_ref[...]).astype(BF16)

    @pl.when(used)
    def _():
        h = h_ref[...]
        a = jnp.dot(h, wg_ref[...], preferred_element_type=F32)
        hid = (a * jax.nn.sigmoid(a)) * jnp.dot(h, wu_ref[...], preferred_element_type=F32)
        w = w_ref[...]
        lane = lax.broadcasted_iota(jnp.int32, w.shape, 1)
        wj = jnp.sum(jnp.where(lane == j, w, 0.0), axis=-1, keepdims=True)
        acc_ref[...] += jnp.dot(hid.astype(BF16), wd_ref[...], preferred_element_type=F32) * wj

    @pl.when(j == pl.num_programs(1) - 1)
    def _():
        y = x_ref[...] + acc_ref[...]
        if final_norm:
            y = _rms(y, fg_ref[...])
        o_ref[...] = y


def _moe_grouped(xp, wp, tile_group, n_used, norm_g, w_gate, w_up, w_down, final_g, *, tm, final_norm):
    p, d = xp.shape

    def expert(i, j, tg, nu):
        return (tg[i] * EXPERTS_PER_GROUP + j, 0, 0)

    grid_spec = pltpu.PrefetchScalarGridSpec(
        num_scalar_prefetch=2,
        grid=(p // tm, EXPERTS_PER_GROUP),
        in_specs=[
            pl.BlockSpec((tm, d), lambda i, j, tg, nu: (i, 0)),
            pl.BlockSpec((tm, EXPERTS_PER_GROUP), lambda i, j, tg, nu: (i, 0)),
            pl.BlockSpec((1, d), lambda i, j, tg, nu: (0, 0)),
            pl.BlockSpec((None, d, EXPERT_FF), expert),
            pl.BlockSpec((None, d, EXPERT_FF), expert),
            pl.BlockSpec((None, EXPERT_FF, d), expert),
            pl.BlockSpec((1, d), lambda i, j, tg, nu: (0, 0)),
        ],
        out_specs=pl.BlockSpec((tm, d), lambda i, j, tg, nu: (i, 0)),
        scratch_shapes=[pltpu.VMEM((tm, d), BF16), pltpu.VMEM((tm, d), F32)],
    )
    return pl.pallas_call(
        functools.partial(_moe_kernel, final_norm=final_norm),
        grid_spec=grid_spec,
        out_shape=jax.ShapeDtypeStruct((p, d), F32),
        compiler_params=_params("parallel", "arbitrary"),
        name="experts",
    )(tile_group, n_used, xp, wp, norm_g.reshape(1, d), w_gate, w_up, w_down, final_g.reshape(1, d))


def _group_plan(route, tm):
    t = route.shape[1]
    gid = route[EXPERTS_PER_GROUP].astype(jnp.int32)
    onehot = (gid[:, None] == jnp.arange(N_GROUPS, dtype=jnp.int32)[None, :]).astype(jnp.int32)
    csum = jnp.cumsum(onehot, axis=0)
    count = csum[-1]
    rank = jnp.take_along_axis(csum, gid[:, None], axis=1)[:, 0] - 1
    padded = (count + tm - 1) // tm * tm
    end = jnp.cumsum(padded)
    pos = (end - padded)[gid] + rank
    n_tiles = t // tm + N_GROUPS
    src = jnp.zeros((n_tiles * tm,), jnp.int32).at[pos].set(jnp.arange(t, dtype=jnp.int32))
    tile_start = jnp.arange(n_tiles, dtype=jnp.int32) * tm
    tile_group = jnp.minimum(jnp.sum((tile_start[:, None] >= end[None, :]).astype(jnp.int32), axis=1), N_GROUPS - 1)
    n_used = (end[-1] // tm).reshape(1)
    return pos, src, tile_group, n_used


def _moe(x, route, norm_g, w_gate, w_up, w_down, final_g, *, tm, final_norm):
    pos, src, tile_group, n_used = _group_plan(route, tm)
    xp = _row_gather(x, src, rows=1024)
    wp = route[:EXPERTS_PER_GROUP].T[src]
    yp = _moe_grouped(xp, wp, tile_group, n_used, norm_g, w_gate, w_up, w_down, final_g, tm=tm,
                      final_norm=final_norm)
    return _row_gather(yp, pos, rows=1024)


def _rope_tables(seq_len):
    rows = seq_len // GRID_W
    row = jnp.repeat(jnp.arange(rows, dtype=F32), GRID_W)
    col = jnp.tile(jnp.arange(GRID_W, dtype=F32), rows)
    n_freq = HEAD_DIM // 4
    inv = ROPE_THETA ** (-jnp.arange(n_freq, dtype=F32) / n_freq)
    ar = row[:, None] * inv
    ac = col[:, None] * inv
    cos = jnp.concatenate([jnp.cos(ar), jnp.cos(ar), jnp.cos(ac), jnp.cos(ac)], axis=1)
    sin = jnp.concatenate([-jnp.sin(ar), jnp.sin(ar), -jnp.sin(ac), jnp.sin(ac)], axis=1)
    return cos, sin


def _prepare_layer(l, p):
    wr_t = jnp.concatenate([p["router_group_w"][l].T, p["router_expert_w"][l].T,
                            jnp.zeros((ROUTER_ROWS - N_GROUPS - N_EXPERTS, D_MODEL), F32)], axis=0)
    br = jnp.concatenate([p["router_group_b"][l], p["router_expert_b"][l].reshape(-1),
                          jnp.zeros((ROUTER_ROWS - N_GROUPS - N_EXPERTS,), F32)]).reshape(ROUTER_ROWS, 1)
    return dict(
        norm_mix_g=p["norm_mix_g"][l], w_in=p["w_in"][l].astype(BF16),
        sgu_norm_g=p["sgu_norm_g"][l], sgu_w=p["sgu_w"][l].astype(BF16), sgu_b_t=p["sgu_b"][l].T,
        q_norm_g=p["attn_q_norm_g"][l], k_norm_g=p["attn_k_norm_g"][l],
        ret_tabs=_retention_tables(p["ret_decay"][l]), ret_norm_g=p["ret_norm_g"][l],
        w_a=p["w_branch_a"][l].astype(BF16), w_b=p["w_branch_b"][l].astype(BF16),
        w_c=p["w_branch_c"][l].astype(BF16), w_out=p["w_out"][l].astype(BF16),
        xattn_norm_g=p["xattn_norm_g"][l], mem_norm_g=p["mem_norm_g"][l],
        wq=p["xattn_wq"][l].astype(BF16), wkv=p["xattn_wkv"][l].astype(BF16), wo=p["xattn_wo"][l].astype(BF16),
        ffn_norm_g=p["ffn_norm_g"][l], wr_t=wr_t, br=br,
        w_gate=p["expert_w_gate"][l].astype(BF16), w_up=p["expert_w_up"][l].astype(BF16),
        w_down=p["expert_w_down"][l].astype(BF16),
    )


def _trunk(x, mem, layers, final_norm_g):
    b, s, d = x.shape
    t = b * s
    m = mem.shape[1]
    cos_t, sin_t = _rope_tables(s)
    xt = x.reshape(t, d)
    memt = mem.reshape(b * m, d)
    for l, w in enumerate(layers):
        z = _rms_matmul(xt, w["norm_mix_g"], w["w_in"], tm=1024, tn=512)
        z3 = z.reshape(b, s, IN_WIDTH)
        out_a = _sgu(z3, w["sgu_norm_g"], w["sgu_w"], w["sgu_b_t"], ts=512)
        qr, kr, vaug, cqr, ckr, qn, kn = _prep(z3, cos_t, sin_t, w["q_norm_g"], w["k_norm_g"], ts=512)
        out_b = _flash(qr, kr, vaug, _score_bound(qn, kn), tq=256, tk=512)
        out_c = _retention(cqr, ckr, z3, w["ret_tabs"], w["ret_norm_g"], ts=512)
        merged = _merge(out_a.reshape(t, SGU_WIDTH), out_b.reshape(t, ATTN_Q_W), out_c.reshape(t, RET_V_W),
                        w["w_a"], w["w_b"], w["w_c"], z, tm=512, tn=512)
        xt = _matmul_res(merged, w["w_out"], xt, tm=512, tn=512)
        kv = _rms_matmul(memt, w["mem_norm_g"], w["wkv"], tm=256, tn=512)
        xt = _xattn(xt.reshape(b, s, d), w["xattn_norm_g"], w["wq"], kv.reshape(b, m, 2 * XATTN_W), w["wo"],
                    tm=512).reshape(t, d)
        route = _router(xt, w["ffn_norm_g"], w["wr_t"], w["br"], tm=512)
        xt = _moe(xt, route, w["ffn_norm_g"], w["w_gate"], w["w_up"], w["w_down"], final_norm_g, tm=512,
                  final_norm=(l == len(layers) - 1))
    return xt.reshape(b, s, d)


def kernel(x_prompt, x_sample, mem_prompt, mem_sample, norm_mix_g, w_in, sgu_norm_g, sgu_w, sgu_b, attn_q_norm_g, attn_k_norm_g, ret_decay, ret_norm_g, w_branch_a, w_branch_b, w_branch_c, w_out, xattn_norm_g, mem_norm_g, xattn_wq, xattn_wkv, xattn_wo, ffn_norm_g, router_group_w, router_group_b, router_expert_w, router_expert_b, expert_w_gate, expert_w_up, expert_w_down, final_norm_g):
    p = dict(norm_mix_g=norm_mix_g, w_in=w_in, sgu_norm_g=sgu_norm_g, sgu_w=sgu_w, sgu_b=sgu_b,
             attn_q_norm_g=attn_q_norm_g, attn_k_norm_g=attn_k_norm_g, ret_decay=ret_decay, ret_norm_g=ret_norm_g,
             w_branch_a=w_branch_a, w_branch_b=w_branch_b, w_branch_c=w_branch_c, w_out=w_out,
             xattn_norm_g=xattn_norm_g, mem_norm_g=mem_norm_g, xattn_wq=xattn_wq, xattn_wkv=xattn_wkv,
             xattn_wo=xattn_wo, ffn_norm_g=ffn_norm_g, router_group_w=router_group_w, router_group_b=router_group_b,
             router_expert_w=router_expert_w, router_expert_b=router_expert_b, expert_w_gate=expert_w_gate,
             expert_w_up=expert_w_up, expert_w_down=expert_w_down)
    layers = [_prepare_layer(l, p) for l in range(DEPTH)]
    y_prompt = _trunk(x_prompt, mem_prompt, layers, final_norm_g)
    y_sample = _trunk(x_sample, mem_sample, layers, final_norm_g)
    return (y_prompt, y_sample)
```

```python
import functools

import jax
import jax.numpy as jnp
import numpy as np
from jax import lax
from jax.experimental import pallas as pl
from jax.experimental.pallas import tpu as pltpu

F32 = jnp.float32
BF16 = jnp.bfloat16

D_MODEL = 2048
DEPTH = 2
GRID_W = 64
ROPE_THETA = 10000.0
CHUNK = 128
HEAD_DIM = 128
NORM_EPS = 1e-6
SGU_WIDTH = 1024
SGU_GROUPS = 4
SGU_GROUP_W = SGU_WIDTH // SGU_GROUPS
ATTN_HEADS = 8
ATTN_KV_HEADS = 2
ATTN_REP = ATTN_HEADS // ATTN_KV_HEADS
ATTN_Q_W = ATTN_HEADS * HEAD_DIM
ATTN_KV_W = ATTN_KV_HEADS * HEAD_DIM
RET_HEADS = 4
RET_V_DIM = 256
RET_QK_W = RET_HEADS * HEAD_DIM
RET_V_W = RET_HEADS * RET_V_DIM
XATTN_HEADS = 4
XATTN_W = XATTN_HEADS * HEAD_DIM
N_GROUPS = 4
EXPERTS_PER_GROUP = 4
N_EXPERTS = N_GROUPS * EXPERTS_PER_GROUP
EXPERT_FF = 512
ROUTER_ROWS = 24
ROUTE_ROWS = 8

OFF_A_U = 0
OFF_A_V = OFF_A_U + SGU_WIDTH
OFF_B_Q = OFF_A_V + SGU_WIDTH
OFF_B_K = OFF_B_Q + ATTN_Q_W
OFF_B_V = OFF_B_K + ATTN_KV_W
OFF_C_Q = OFF_B_V + ATTN_KV_W
OFF_C_K = OFF_C_Q + RET_QK_W
OFF_C_V = OFF_C_K + RET_QK_W
OFF_C_G = OFF_C_V + RET_V_W
OFF_GATE_A = OFF_C_G + RET_V_W
OFF_GATE_B = OFF_GATE_A + D_MODEL
OFF_GATE_C = OFF_GATE_B + D_MODEL
IN_WIDTH = OFF_GATE_C + D_MODEL

V7X_VMEM_LIMIT_BYTES = 56 * 1024 * 1024

_NT = (((1,), (1,)), ((), ()))
LOG2_E = 1.4426950408889634
FIXED_SHIFT_LIMIT = 60.0


def _tile(n, pref):
    t = min(pref, n)
    while n % t:
        t //= 2
    return t


def _params(*sem):
    return pltpu.CompilerParams(dimension_semantics=sem, vmem_limit_bytes=V7X_VMEM_LIMIT_BYTES)


def _rms(x, g):
    return x * lax.rsqrt(jnp.mean(x * x, axis=-1, keepdims=True) + NORM_EPS) * g


def _rms_matmul_kernel(x_ref, g_ref, w_ref, o_ref, h_ref):
    @pl.when(pl.program_id(1) == 0)
    def _():
        h_ref[...] = _rms(x_ref[...], g_ref[...]).astype(BF16)

    o_ref[...] = jnp.dot(h_ref[...], w_ref[...], preferred_element_type=F32).astype(o_ref.dtype)


def _rms_matmul(x, g, w, *, tm, tn):
    t, d = x.shape
    n = w.shape[1]
    tm, tn = _tile(t, tm), _tile(n, tn)
    return pl.pallas_call(
        _rms_matmul_kernel,
        grid=(t // tm, n // tn),
        in_specs=[
            pl.BlockSpec((tm, d), lambda i, j: (i, 0)),
            pl.BlockSpec((1, d), lambda i, j: (0, 0)),
            pl.BlockSpec((d, tn), lambda i, j: (0, j)),
        ],
        out_specs=pl.BlockSpec((tm, tn), lambda i, j: (i, j)),
        out_shape=jax.ShapeDtypeStruct((t, n), BF16),
        scratch_shapes=[pltpu.VMEM((tm, d), BF16)],
        compiler_params=_params("parallel", "arbitrary"),
        name="rms_matmul",
    )(x, g.reshape(1, d), w)


def _gelu(x):
    return 0.5 * x * (1.0 + lax.erf(x * (2.0 ** -0.5)))


def _sgu_kernel(u_ref, v_ref, ng_ref, ws_ref, bs_ref, o_ref, *, n_chunks):
    for c in range(n_chunks):
        rows = slice(c * CHUNK, (c + 1) * CHUNK)
        u = _gelu(u_ref[rows, :].astype(F32))
        v = _gelu(v_ref[rows, :].astype(F32))
        vc = v - jnp.mean(v, axis=-1, keepdims=True)
        vn = vc * lax.rsqrt(jnp.mean(vc * vc, axis=-1, keepdims=True) + NORM_EPS) * ng_ref[...]
        vn = vn.astype(BF16)
        for g in range(SGU_GROUPS):
            cols = slice(g * SGU_GROUP_W, (g + 1) * SGU_GROUP_W)
            s = jnp.dot(ws_ref[g], vn[:, cols], preferred_element_type=F32) + bs_ref[:, g:g + 1]
            o_ref[rows, cols] = (u[:, cols] * s).astype(o_ref.dtype)


def _sgu(z3, norm_g, w_s, b_s_t, *, ts):
    b, s, _ = z3.shape
    ts = _tile(s, ts)
    return pl.pallas_call(
        functools.partial(_sgu_kernel, n_chunks=ts // CHUNK),
        grid=(b, s // ts),
        in_specs=[
            pl.BlockSpec((None, ts, SGU_WIDTH), lambda bi, i: (bi, i, OFF_A_U // SGU_WIDTH)),
            pl.BlockSpec((None, ts, SGU_WIDTH), lambda bi, i: (bi, i, OFF_A_V // SGU_WIDTH)),
            pl.BlockSpec((1, SGU_WIDTH), lambda bi, i: (0, 0)),
            pl.BlockSpec((SGU_GROUPS, CHUNK, CHUNK), lambda bi, i: (0, 0, 0)),
            pl.BlockSpec((CHUNK, SGU_GROUPS), lambda bi, i: (0, 0)),
        ],
        out_specs=pl.BlockSpec((None, ts, SGU_WIDTH), lambda bi, i: (bi, i, 0)),
        out_shape=jax.ShapeDtypeStruct((b, s, SGU_WIDTH), BF16),
        compiler_params=_params("parallel", "parallel"),
        name="sgu",
    )(z3, z3, norm_g.reshape(1, SGU_WIDTH), w_s, b_s_t)


def _rope(x, cos, sin_signed):
    lane = lax.broadcasted_iota(jnp.int32, x.shape, 1)
    first_half = (lane % (HEAD_DIM // 2)) < (HEAD_DIM // 4)
    partner = jnp.where(first_half, pltpu.roll(x, HEAD_DIM - HEAD_DIM // 4, 1), pltpu.roll(x, HEAD_DIM // 4, 1))
    return x * cos + partner * sin_signed


def _prep_kernel(bq_ref, bk_ref, bv_ref, cq_ref, ck_ref, cos_ref, sin_ref, qg_ref, kg_ref,
                 oq_ref, ok_ref, ov_ref, ocq_ref, ock_ref, qn_ref, kn_ref):
    cos = cos_ref[...]
    sin = sin_ref[...]
    scale = HEAD_DIM ** -0.5
    q_ss = None
    for h in range(ATTN_HEADS):
        cols = slice(h * HEAD_DIM, (h + 1) * HEAD_DIM)
        x = _rope(_rms(bq_ref[:, cols].astype(F32), qg_ref[...]), cos, sin) * (scale * LOG2_E)
        oq_ref[:, cols] = x.astype(BF16)
        ss = jnp.sum(x * x, axis=-1, keepdims=True)
        q_ss = ss if q_ss is None else jnp.maximum(q_ss, ss)
    k_ss = None
    for h in range(ATTN_KV_HEADS):
        cols = slice(h * HEAD_DIM, (h + 1) * HEAD_DIM)
        x = _rope(_rms(bk_ref[:, cols].astype(F32), kg_ref[...]), cos, sin)
        ok_ref[:, cols] = x.astype(BF16)
        ss = jnp.sum(x * x, axis=-1, keepdims=True)
        k_ss = ss if k_ss is None else jnp.maximum(k_ss, ss)
        ov_ref[:, 2 * h * HEAD_DIM:(2 * h + 1) * HEAD_DIM] = bv_ref[:, cols]
        ov_ref[:, (2 * h + 1) * HEAD_DIM:(2 * h + 2) * HEAD_DIM] = jnp.ones((bv_ref.shape[0], HEAD_DIM), BF16)
    qn_ref[...] = jnp.broadcast_to(jnp.max(q_ss, axis=0, keepdims=True), qn_ref.shape)
    kn_ref[...] = jnp.broadcast_to(jnp.max(k_ss, axis=0, keepdims=True), kn_ref.shape)
    for h in range(RET_HEADS):
        cols = slice(h * HEAD_DIM, (h + 1) * HEAD_DIM)
        ocq_ref[:, cols] = (_rope(cq_ref[:, cols].astype(F32), cos, sin) * scale).astype(BF16)
        ock_ref[:, cols] = _rope(ck_ref[:, cols].astype(F32), cos, sin).astype(BF16)


def _prep(z3, cos_t, sin_t, q_norm_g, k_norm_g, *, ts):
    b, s, _ = z3.shape
    ts = _tile(s, ts)
    n = s // ts

    def zspec(width, off):
        return pl.BlockSpec((None, ts, width), lambda bi, i: (bi, i, off // width))

    def ospec(width):
        return pl.BlockSpec((None, ts, width), lambda bi, i: (bi, i, 0))

    tab = pl.BlockSpec((ts, HEAD_DIM), lambda bi, i: (i, 0))
    gain = pl.BlockSpec((1, HEAD_DIM), lambda bi, i: (0, 0))
    nspec = pl.BlockSpec((None, None, 8, HEAD_DIM), lambda bi, i: (bi, i, 0, 0))
    widths = (ATTN_Q_W, ATTN_KV_W, 2 * ATTN_KV_W, RET_QK_W, RET_QK_W)
    return pl.pallas_call(
        _prep_kernel,
        grid=(b, n),
        in_specs=[zspec(ATTN_Q_W, OFF_B_Q), zspec(ATTN_KV_W, OFF_B_K), zspec(ATTN_KV_W, OFF_B_V),
                  zspec(RET_QK_W, OFF_C_Q), zspec(RET_QK_W, OFF_C_K), tab, tab, gain, gain],
        out_specs=[ospec(w) for w in widths] + [nspec, nspec],
        out_shape=[jax.ShapeDtypeStruct((b, s, w), BF16) for w in widths]
        + [jax.ShapeDtypeStruct((b, n, 8, HEAD_DIM), F32)] * 2,
        compiler_params=_params("parallel", "parallel"),
        name="qk_prep",
    )(z3, z3, z3, z3, z3, cos_t, sin_t, q_norm_g.reshape(1, HEAD_DIM), k_norm_g.reshape(1, HEAD_DIM))


def _flash_kernel(b2_ref, q_ref, k_ref, v_ref, o_ref, qc_ref, acc_ref, m_ref, *, tq, tk):
    for r in range(ATTN_REP):
        qc_ref[r * tq:(r + 1) * tq, :] = q_ref[:, r * HEAD_DIM:(r + 1) * HEAD_DIM]
    b2 = b2_ref[pl.program_id(0)]
    n_kv = k_ref.shape[0] // tk
    acc_ref[...] = jnp.zeros(acc_ref.shape, F32)

    def tiles(j):
        off = pl.multiple_of(j * tk, tk)
        k = k_ref[pl.ds(off, tk), :]
        return lax.dot_general(qc_ref[...], k, _NT, preferred_element_type=F32), v_ref[pl.ds(off, tk), :]

    @pl.when(b2 <= FIXED_SHIFT_LIMIT)
    def _():
        def body(j, carry):
            s, v = tiles(j)
            acc_ref[...] += jnp.dot(jnp.exp2(s - b2).astype(BF16), v, preferred_element_type=F32)
            return carry

        lax.fori_loop(0, n_kv, body, 0, unroll=4 if n_kv % 4 == 0 else 1)

    @pl.when(b2 > FIXED_SHIFT_LIMIT)
    def _():
        m_ref[...] = jnp.full(m_ref.shape, -jnp.inf, F32)

        def body(j, carry):
            s, v = tiles(j)
            m_old = m_ref[...]
            m_new = jnp.maximum(m_old, jnp.max(s, axis=-1, keepdims=True))
            p = jnp.exp2(s - m_new).astype(BF16)
            acc_ref[...] = jnp.exp2(m_old - m_new) * acc_ref[...] + jnp.dot(p, v, preferred_element_type=F32)
            m_ref[...] = m_new
            return carry

        lax.fori_loop(0, n_kv, body, 0)

    out = acc_ref[:, :HEAD_DIM] / acc_ref[:, HEAD_DIM:]
    for r in range(ATTN_REP):
        o_ref[:, r * HEAD_DIM:(r + 1) * HEAD_DIM] = out[r * tq:(r + 1) * tq, :].astype(o_ref.dtype)


def _flash(qr, kr, vaug, bound2, *, tq, tk):
    b, s, _ = qr.shape
    tq, tk = _tile(s, tq), _tile(s, tk)
    gw = ATTN_REP * HEAD_DIM
    grid_spec = pltpu.PrefetchScalarGridSpec(
        num_scalar_prefetch=1,
        grid=(b, ATTN_KV_HEADS, s // tq),
        in_specs=[
            pl.BlockSpec((None, tq, gw), lambda bi, g, i, b2: (bi, i, g)),
            pl.BlockSpec((None, s, HEAD_DIM), lambda bi, g, i, b2: (bi, 0, g)),
            pl.BlockSpec((None, s, 2 * HEAD_DIM), lambda bi, g, i, b2: (bi, 0, g)),
        ],
        out_specs=pl.BlockSpec((None, tq, gw), lambda bi, g, i, b2: (bi, i, g)),
        scratch_shapes=[
            pltpu.VMEM((ATTN_REP * tq, HEAD_DIM), BF16),
            pltpu.VMEM((ATTN_REP * tq, 2 * HEAD_DIM), F32),
            pltpu.VMEM((ATTN_REP * tq, 1), F32),
        ],
    )
    return pl.pallas_call(
        functools.partial(_flash_kernel, tq=tq, tk=tk),
        grid_spec=grid_spec,
        out_shape=jax.ShapeDtypeStruct((b, s, ATTN_Q_W), BF16),
        compiler_params=_params("parallel", "parallel", "arbitrary"),
        name="flash_attention",
    )(bound2, qr, kr, vaug)


def _score_bound(qn, kn):
    return jnp.sqrt(jnp.max(qn[:, :, 0, 0], axis=1) * jnp.max(kn[:, :, 0, 0], axis=1)) * 1.01


def _ret_chunk(q, k, v, st, dm, qd, kd, cd):
    s = lax.dot_general(q, k, _NT, preferred_element_type=F32) * dm
    o = jnp.dot(s.astype(BF16), v, preferred_element_type=F32)
    o = o + qd * jnp.dot(q, st.astype(BF16), preferred_element_type=F32)
    kt = (k.astype(F32).T * kd).astype(BF16)
    return o, cd * st + jnp.dot(kt, v, preferred_element_type=F32)


def _ret_fwd_kernel(q_ref, k_ref, v_ref, dm_ref, qd_ref, kd_ref, cd_ref, o_ref, st_ref, *, n_chunks):
    @pl.when(pl.program_id(2) == 0)
    def _():
        st_ref[...] = jnp.zeros(st_ref.shape, F32)

    for c in range(n_chunks):
        rows = slice(c * CHUNK, (c + 1) * CHUNK)
        o, st = _ret_chunk(q_ref[rows, :], k_ref[rows, :], v_ref[rows, :], st_ref[...],
                           dm_ref[...], qd_ref[...], kd_ref[...], cd_ref[...])
        o_ref[rows, :] = o
        st_ref[...] = st


def _ret_bwd_kernel(q_ref, k_ref, v_ref, g_ref, of_ref, dm_ref, qd_ref, kd_ref, cd_ref, ng_ref,
                    o_ref, st_ref, *, n_chunks):
    @pl.when(pl.program_id(2) == 0)
    def _():
        st_ref[...] = jnp.zeros(st_ref.shape, F32)

    for c in reversed(range(n_chunks)):
        rows = slice(c * CHUNK, (c + 1) * CHUNK)
        o, st = _ret_chunk(q_ref[rows, :], k_ref[rows, :], v_ref[rows, :], st_ref[...],
                           dm_ref[...], qd_ref[...], kd_ref[...], cd_ref[...])
        st_ref[...] = st
        o = o + of_ref[rows, :]
        oc = o - jnp.mean(o, axis=-1, keepdims=True)
        o = oc * lax.rsqrt(jnp.mean(oc * oc, axis=-1, keepdims=True) + NORM_EPS) * ng_ref[...]
        gate = g_ref[rows, :].astype(F32)
        o_ref[rows, :] = (o * (gate * jax.nn.sigmoid(gate))).astype(o_ref.dtype)


def _retention(cqr, ckr, z3, tabs, norm_g, *, ts):
    b, s, _ = cqr.shape
    ts = _tile(s, ts)
    n = s // ts
    dv = RET_V_DIM

    def seq(width, off, rev):
        if rev:
            return pl.BlockSpec((None, ts, width), lambda bi, h, i: (bi, n - 1 - i, off // width + h))
        return pl.BlockSpec((None, ts, width), lambda bi, h, i: (bi, i, off // width + h))

    def per_head(r, c):
        return pl.BlockSpec((None, r, c), lambda bi, h, i: (h, 0, 0))

    def tab_specs():
        return [per_head(CHUNK, CHUNK), per_head(CHUNK, 1), per_head(1, CHUNK), per_head(1, dv)]

    sem = _params("parallel", "parallel", "arbitrary")
    o_fwd = pl.pallas_call(
        functools.partial(_ret_fwd_kernel, n_chunks=ts // CHUNK),
        grid=(b, RET_HEADS, n),
        in_specs=[seq(HEAD_DIM, 0, False), seq(HEAD_DIM, 0, False), seq(dv, OFF_C_V, False)] + tab_specs(),
        out_specs=seq(dv, 0, False),
        out_shape=jax.ShapeDtypeStruct((b, s, RET_V_W), F32),
        scratch_shapes=[pltpu.VMEM((HEAD_DIM, dv), F32)],
        compiler_params=sem,
        name="retention_fwd",
    )(cqr, ckr, z3, *tabs[0])
    return pl.pallas_call(
        functools.partial(_ret_bwd_kernel, n_chunks=ts // CHUNK),
        grid=(b, RET_HEADS, n),
        in_specs=[seq(HEAD_DIM, 0, True), seq(HEAD_DIM, 0, True), seq(dv, OFF_C_V, True), seq(dv, OFF_C_G, True),
                  seq(dv, 0, True)] + tab_specs() + [pl.BlockSpec((1, dv), lambda bi, h, i: (0, h))],
        out_specs=seq(dv, 0, True),
        out_shape=jax.ShapeDtypeStruct((b, s, RET_V_W), BF16),
        scratch_shapes=[pltpu.VMEM((HEAD_DIM, dv), F32)],
        compiler_params=sem,
        name="retention_bwd",
    )(cqr, ckr, z3, z3, o_fwd, *tabs[1], norm_g.reshape(1, RET_V_W))


def _retention_tables(ret_decay):
    lg = -jnp.exp(ret_decay.astype(F32))
    idx = jnp.arange(CHUNK, dtype=F32)
    diff = idx[:, None] - idx[None, :]
    out = []
    for d, (mask, delta, qpow, kpow) in enumerate((
            (diff >= 0, diff, idx + 1.0, CHUNK - 1.0 - idx),
            (diff < 0, -diff, CHUNK - idx, idx))):
        l = lg[d][:, None, None]
        dm = jnp.where(mask, jnp.exp(l * jnp.where(mask, delta, 0.0)), 0.0)
        qd = jnp.exp(lg[d][:, None] * qpow)[:, :, None]
        kd = jnp.exp(lg[d][:, None] * kpow)[:, None, :]
        cd = jnp.broadcast_to(jnp.exp(lg[d] * CHUNK)[:, None, None], (RET_HEADS, 1, RET_V_DIM))
        out.append((dm, qd, kd, cd))
    return out


def _merge_kernel(a_ref, b_ref, c_ref, wa_ref, wb_ref, wc_ref, ga_ref, gb_ref, gc_ref, o_ref):
    acc = jax.nn.sigmoid(ga_ref[...].astype(F32)) * jnp.dot(a_ref[...], wa_ref[...], preferred_element_type=F32)
    acc = acc + jax.nn.sigmoid(gb_ref[...].astype(F32)) * jnp.dot(b_ref[...], wb_ref[...], preferred_element_type=F32)
    acc = acc + jax.nn.sigmoid(gc_ref[...].astype(F32)) * jnp.dot(c_ref[...], wc_ref[...], preferred_element_type=F32)
    o_ref[...] = acc.astype(o_ref.dtype)


def _merge(out_a, out_b, out_c, w_a, w_b, w_c, z, *, tm, tn):
    t, k = out_a.shape
    tm, tn = _tile(t, tm), _tile(D_MODEL, tn)
    lhs = pl.BlockSpec((tm, k), lambda i, j: (i, 0))
    rhs = pl.BlockSpec((k, tn), lambda i, j: (0, j))

    def gate(off):
        return pl.BlockSpec((tm, tn), lambda i, j: (i, off // tn + j))

    return pl.pallas_call(
        _merge_kernel,
        grid=(t // tm, D_MODEL // tn),
        in_specs=[lhs, lhs, lhs, rhs, rhs, rhs, gate(OFF_GATE_A), gate(OFF_GATE_B), gate(OFF_GATE_C)],
        out_specs=pl.BlockSpec((tm, tn), lambda i, j: (i, j)),
        out_shape=jax.ShapeDtypeStruct((t, D_MODEL), BF16),
        compiler_params=_params("parallel", "parallel"),
        name="branch_merge",
    )(out_a, out_b, out_c, w_a, w_b, w_c, z, z, z)


def _matmul_res_kernel(a_ref, w_ref, x_ref, o_ref):
    o_ref[...] = x_ref[...] + jnp.dot(a_ref[...], w_ref[...], preferred_element_type=F32)


def _matmul_res(a, w, x, *, tm, tn):
    t, k = a.shape
    n = w.shape[1]
    tm, tn = _tile(t, tm), _tile(n, tn)
    return pl.pallas_call(
        _matmul_res_kernel,
        grid=(t // tm, n // tn),
        in_specs=[
            pl.BlockSpec((tm, k), lambda i, j: (i, 0)),
            pl.BlockSpec((k, tn), lambda i, j: (0, j)),
            pl.BlockSpec((tm, tn), lambda i, j: (i, j)),
        ],
        out_specs=pl.BlockSpec((tm, tn), lambda i, j: (i, j)),
        out_shape=jax.ShapeDtypeStruct((t, n), F32),
        compiler_params=_params("parallel", "parallel"),
        name="out_proj_residual",
    )(a, w, x)


def _xattn_kernel(x_ref, g_ref, wq_ref, kv_ref, wo_ref, o_ref):
    x = x_ref[...]
    h = _rms(x, g_ref[...]).astype(BF16)
    q = (jnp.dot(h, wq_ref[...], preferred_element_type=F32) * (HEAD_DIM ** -0.5)).astype(BF16)
    heads = []
    for hh in range(XATTN_HEADS):
        cols = slice(hh * HEAD_DIM, (hh + 1) * HEAD_DIM)
        k = kv_ref[:, cols]
        v = kv_ref[:, XATTN_W + hh * HEAD_DIM:XATTN_W + (hh + 1) * HEAD_DIM]
        s = lax.dot_general(q[:, cols], k, _NT, preferred_element_type=F32)
        p = jnp.exp(s - jnp.max(s, axis=-1, keepdims=True))
        o = jnp.dot(p.astype(BF16), v, preferred_element_type=F32) / jnp.sum(p, axis=-1, keepdims=True)
        heads.append(o.astype(BF16))
    o_ref[...] = x + jnp.dot(jnp.concatenate(heads, axis=1), wo_ref[...], preferred_element_type=F32)


def _xattn(x3, norm_g, wq, kv3, wo, *, tm):
    b, s, d = x3.shape
    m = kv3.shape[1]
    tm = _tile(s, tm)
    return pl.pallas_call(
        _xattn_kernel,
        grid=(b, s // tm),
        in_specs=[
            pl.BlockSpec((None, tm, d), lambda bi, i: (bi, i, 0)),
            pl.BlockSpec((1, d), lambda bi, i: (0, 0)),
            pl.BlockSpec((d, XATTN_W), lambda bi, i: (0, 0)),
            pl.BlockSpec((None, m, 2 * XATTN_W), lambda bi, i: (bi, 0, 0)),
            pl.BlockSpec((XATTN_W, d), lambda bi, i: (0, 0)),
        ],
        out_specs=pl.BlockSpec((None, tm, d), lambda bi, i: (bi, i, 0)),
        out_shape=jax.ShapeDtypeStruct((b, s, d), F32),
        compiler_params=_params("parallel", "parallel"),
        name="memory_cross_attention",
    )(x3, norm_g.reshape(1, d), wq, kv3, wo)


def _router_kernel(x_ref, g_ref, wr_ref, br_ref, rt_ref):
    h = _rms(x_ref[...], g_ref[...])
    lt = lax.dot_general(wr_ref[...], h, _NT, precision=lax.Precision.HIGHEST,
                         preferred_element_type=F32) + br_ref[...]
    gl = [lt[i:i + 1, :] for i in range(N_GROUPS)]
    gmax = jnp.maximum(jnp.maximum(gl[0], gl[1]), jnp.maximum(gl[2], gl[3]))
    denom = sum(jnp.exp(v - gmax) for v in gl)
    g_prob = 1.0 / denom
    g_idx = jnp.where(gl[0] == gmax, 0, jnp.where(gl[1] == gmax, 1, jnp.where(gl[2] == gmax, 2, 3)))
    sel = []
    for j in range(EXPERTS_PER_GROUP):
        rows = [lt[N_GROUPS + g * EXPERTS_PER_GROUP + j:N_GROUPS + g * EXPERTS_PER_GROUP + j + 1, :]
                for g in range(N_GROUPS)]
        sel.append(jnp.where(g_idx == 0, rows[0], jnp.where(g_idx == 1, rows[1],
                                                            jnp.where(g_idx == 2, rows[2], rows[3]))))
    top1 = jnp.maximum(jnp.maximum(sel[0], sel[1]), jnp.maximum(sel[2], sel[3]))
    i1 = jnp.where(sel[0] == top1, 0, jnp.where(sel[1] == top1, 1, jnp.where(sel[2] == top1, 2, 3)))
    rest = [jnp.where(i1 == j, -jnp.inf, sel[j]) for j in range(EXPERTS_PER_GROUP)]
    top2 = jnp.maximum(jnp.maximum(rest[0], rest[1]), jnp.maximum(rest[2], rest[3]))
    i2 = jnp.where(rest[0] == top2, 0, jnp.where(rest[1] == top2, 1, jnp.where(rest[2] == top2, 2, 3)))
    e2 = jnp.exp(top2 - top1)
    w1 = g_prob / (1.0 + e2)
    w2 = g_prob * e2 / (1.0 + e2)
    row = lax.broadcasted_iota(jnp.int32, rt_ref.shape, 0)
    rt_ref[...] = (jnp.where(row == i1, w1, 0.0) + jnp.where(row == i2, w2, 0.0)
                   + jnp.where(row == EXPERTS_PER_GROUP, g_idx.astype(F32), 0.0))


def _router(x, norm_g, wr_t, br, *, tm):
    t, d = x.shape
    tm = _tile(t, tm)
    return pl.pallas_call(
        _router_kernel,
        grid=(t // tm,),
        in_specs=[
            pl.BlockSpec((tm, d), lambda i: (i, 0)),
            pl.BlockSpec((1, d), lambda i: (0, 0)),
            pl.BlockSpec((ROUTER_ROWS, d), lambda i: (0, 0)),
            pl.BlockSpec((ROUTER_ROWS, 1), lambda i: (0, 0)),
        ],
        out_specs=pl.BlockSpec((ROUTE_ROWS, tm), lambda i: (0, i)),
        out_shape=jax.ShapeDtypeStruct((ROUTE_ROWS, t), F32),
        compiler_params=_params("parallel"),
        name="router",
    )(x, norm_g.reshape(1, d), wr_t, br)


def _row_gather_kernel(idx_ref, src_ref, dst_ref, sem, *, rows):
    def start(r, carry):
        pltpu.make_async_copy(src_ref.at[pl.ds(idx_ref[0, r], 1), :], dst_ref.at[pl.ds(r, 1), :], sem).start()
        return carry

    lax.fori_loop(0, rows, start, 0, unroll=8)
    pltpu.make_async_copy(src_ref.at[pl.ds(0, rows), :], dst_ref, sem).wait()


def _row_gather(src, idx, *, rows):
    n = idx.shape[0]
    d = src.shape[1]
    rows = _tile(n, rows)
    return pl.pallas_call(
        functools.partial(_row_gather_kernel, rows=rows),
        grid=(n // rows,),
        in_specs=[
            pl.BlockSpec((None, 1, rows), lambda i: (i, 0, 0), memory_space=pltpu.SMEM),
            pl.BlockSpec(memory_space=pl.ANY),
        ],
        out_specs=pl.BlockSpec((rows, d), lambda i: (i, 0)),
        out_shape=jax.ShapeDtypeStruct((n, d), src.dtype),
        scratch_shapes=[pltpu.SemaphoreType.DMA(())],
        compiler_params=_params("arbitrary"),
        name="row_gather",
    )(idx.reshape(n // rows, 1, rows), src)


def _moe_kernel(tg_ref, nu_ref, x_ref, w_ref, ng_ref, wg_ref, wu_ref, wd_ref, fg_ref, o_ref, h_ref, acc_ref, *,
                final_norm):
    i = pl.program_id(0)
    j = pl.program_id(1)
    used = i < nu_ref[0]

    @pl.when(j == 0)
    def _():
        acc_ref[...] = jnp.zeros(acc_ref.shape, F32)
        h_ref[...] = _rms(x_ref[...], ng_ref[...]).astype(BF16)

    @pl.when(used)
    def _():
        h = h_ref[...]
        a = jnp.dot(h, wg_ref[...], preferred_element_type=F32)
        hid = (a * jax.nn.sigmoid(a)) * jnp.dot(h, wu_ref[...], preferred_element_type=F32)
        w = w_ref[...]
        lane = lax.broadcasted_iota(jnp.int32, w.shape, 1)
        wj = jnp.sum(jnp.where(lane == j, w, 0.0), axis=-1, keepdims=True)
        acc_ref[...] += jnp.dot(hid.astype(BF16), wd_ref[...], preferred_element_type=F32) * wj

    @pl.when(j == pl.num_programs(1) - 1)
    def _():
        y = x_ref[...] + acc_ref[...]
        if final_norm:
            y = _rms(y, fg_ref[...])
        o_ref[...] = y


def _moe_grouped(xp, wp, tile_group, n_used, norm_g, w_gate, w_up, w_down, final_g, *, tm, final_norm):
    p, d = xp.shape

    def expert(i, j, tg, nu):
        return (tg[i] * EXPERTS_PER_GROUP + j, 0, 0)

    grid_spec = pltpu.PrefetchScalarGridSpec(
        num_scalar_prefetch=2,
        grid=(p // tm, EXPERTS_PER_GROUP),
        in_specs=[
            pl.BlockSpec((tm, d), lambda i, j, tg, nu: (i, 0)),
            pl.BlockSpec((tm, EXPERTS_PER_GROUP), lambda i, j, tg, nu: (i, 0)),
            pl.BlockSpec((1, d), lambda i, j, tg, nu: (0, 0)),
            pl.BlockSpec((None, d, EXPERT_FF), expert),
            pl.BlockSpec((None, d, EXPERT_FF), expert),
            pl.BlockSpec((None, EXPERT_FF, d), expert),
            pl.BlockSpec((1, d), lambda i, j, tg, nu: (0, 0)),
        ],
        out_specs=pl.BlockSpec((tm, d), lambda i, j, tg, nu: (i, 0)),
        scratch_shapes=[pltpu.VMEM((tm, d), BF16), pltpu.VMEM((tm, d), F32)],
    )
    return pl.pallas_call(
        functools.partial(_moe_kernel, final_norm=final_norm),
        grid_spec=grid_spec,
        out_shape=jax.ShapeDtypeStruct((p, d), F32),
        compiler_params=_params("parallel", "arbitrary"),
        name="experts",
    )(tile_group, n_used, xp, wp, norm_g.reshape(1, d), w_gate, w_up, w_down, final_g.reshape(1, d))


def _group_plan(route, tm):
    t = route.shape[1]
    gid = route[EXPERTS_PER_GROUP].astype(jnp.int32)
    onehot = (gid[:, None] == jnp.arange(N_GROUPS, dtype=jnp.int32)[None, :]).astype(jnp.int32)
    csum = jnp.cumsum(onehot, axis=0)
    count = csum[-1]
    rank = jnp.take_along_axis(csum, gid[:, None], axis=1)[:, 0] - 1
    padded = (count + tm - 1) // tm * tm
    end = jnp.cumsum(padded)
    pos = (end - padded)[gid] + rank
    n_tiles = t // tm + N_GROUPS
    src = jnp.zeros((n_tiles * tm,), jnp.int32).at[pos].set(jnp.arange(t, dtype=jnp.int32))
    tile_start = jnp.arange(n_tiles, dtype=jnp.int32) * tm
    tile_group = jnp.minimum(jnp.sum((tile_start[:, None] >= end[None, :]).astype(jnp.int32), axis=1), N_GROUPS - 1)
    n_used = (end[-1] // tm).reshape(1)
    return pos, src, tile_group, n_used


def _moe(x, route, norm_g, w_gate, w_up, w_down, final_g, *, tm, final_norm):
    pos, src, tile_group, n_used = _group_plan(route, tm)
    xp = _row_gather(x, src, rows=512)
    wp = route[:EXPERTS_PER_GROUP].T[src]
    yp = _moe_grouped(xp, wp, tile_group, n_used, norm_g, w_gate, w_up, w_down, final_g, tm=tm,
                      final_norm=final_norm)
    return _row_gather(yp, pos, rows=512)


def _rope_tables(seq_len):
    rows = seq_len // GRID_W
    row = jnp.repeat(jnp.arange(rows, dtype=F32), GRID_W)
    col = jnp.tile(jnp.arange(GRID_W, dtype=F32), rows)
    n_freq = HEAD_DIM // 4
    inv = ROPE_THETA ** (-jnp.arange(n_freq, dtype=F32) / n_freq)
    ar = row[:, None] * inv
    ac = col[:, None] * inv
    cos = jnp.concatenate([jnp.cos(ar), jnp.cos(ar), jnp.cos(ac), jnp.cos(ac)], axis=1)
    sin = jnp.concatenate([-jnp.sin(ar), jnp.sin(ar), -jnp.sin(ac), jnp.sin(ac)], axis=1)
    return cos, sin


def _prepare_layer(l, p):
    wr_t = jnp.concatenate([p["router_group_w"][l].T, p["router_expert_w"][l].T,
                            jnp.zeros((ROUTER_ROWS - N_GROUPS - N_EXPERTS, D_MODEL), F32)], axis=0)
    br = jnp.concatenate([p["router_group_b"][l], p["router_expert_b"][l].reshape(-1),
                          jnp.zeros((ROUTER_ROWS - N_GROUPS - N_EXPERTS,), F32)]).reshape(ROUTER_ROWS, 1)
    return dict(
        norm_mix_g=p["norm_mix_g"][l], w_in=p["w_in"][l].astype(BF16),
        sgu_norm_g=p["sgu_norm_g"][l], sgu_w=p["sgu_w"][l].astype(BF16), sgu_b_t=p["sgu_b"][l].T,
        q_norm_g=p["attn_q_norm_g"][l], k_norm_g=p["attn_k_norm_g"][l],
        ret_tabs=_retention_tables(p["ret_decay"][l]), ret_norm_g=p["ret_norm_g"][l],
        w_a=p["w_branch_a"][l].astype(BF16), w_b=p["w_branch_b"][l].astype(BF16),
        w_c=p["w_branch_c"][l].astype(BF16), w_out=p["w_out"][l].astype(BF16),
        xattn_norm_g=p["xattn_norm_g"][l], mem_norm_g=p["mem_norm_g"][l],
        wq=p["xattn_wq"][l].astype(BF16), wkv=p["xattn_wkv"][l].astype(BF16), wo=p["xattn_wo"][l].astype(BF16),
        ffn_norm_g=p["ffn_norm_g"][l], wr_t=wr_t, br=br,
        w_gate=p["expert_w_gate"][l].astype(BF16), w_up=p["expert_w_up"][l].astype(BF16),
        w_down=p["expert_w_down"][l].astype(BF16),
    )


def _trunk(x, mem, layers, final_norm_g):
    b, s, d = x.shape
    t = b * s
    m = mem.shape[1]
    cos_t, sin_t = _rope_tables(s)
    xt = x.reshape(t, d)
    memt = mem.reshape(b * m, d)
    for l, w in enumerate(layers):
        z = _rms_matmul(xt, w["norm_mix_g"], w["w_in"], tm=1024, tn=1280)
        z3 = z.reshape(b, s, IN_WIDTH)
        out_a = _sgu(z3, w["sgu_norm_g"], w["sgu_w"], w["sgu_b_t"], ts=512)
        qr, kr, vaug, cqr, ckr, qn, kn = _prep(z3, cos_t, sin_t, w["q_norm_g"], w["k_norm_g"], ts=512)
        out_b = _flash(qr, kr, vaug, _score_bound(qn, kn), tq=256, tk=512)
        out_c = _retention(cqr, ckr, z3, w["ret_tabs"], w["ret_norm_g"], ts=512)
        merged = _merge(out_a.reshape(t, SGU_WIDTH), out_b.reshape(t, ATTN_Q_W), out_c.reshape(t, RET_V_W),
                        w["w_a"], w["w_b"], w["w_c"], z, tm=1024, tn=512)
        xt = _matmul_res(merged, w["w_out"], xt, tm=1024, tn=1024)
        kv = _rms_matmul(memt, w["mem_norm_g"], w["wkv"], tm=256, tn=512)
        xt = _xattn(xt.reshape(b, s, d), w["xattn_norm_g"], w["wq"], kv.reshape(b, m, 2 * XATTN_W), w["wo"],
                    tm=512).reshape(t, d)
        route = _router(xt, w["ffn_norm_g"], w["wr_t"], w["br"], tm=512)
        xt = _moe(xt, route, w["ffn_norm_g"], w["w_gate"], w["w_up"], w["w_down"], final_norm_g, tm=512,
                  final_norm=(l == len(layers) - 1))
    return xt.reshape(b, s, d)


def kernel(x_prompt, x_sample, mem_prompt, mem_sample, norm_mix_g, w_in, sgu_norm_g, sgu_w, sgu_b, attn_q_norm_g, attn_k_norm_g, ret_decay, ret_norm_g, w_branch_a, w_branch_b, w_branch_c, w_out, xattn_norm_g, mem_norm_g, xattn_wq, xattn_wkv, xattn_wo, ffn_norm_g, router_group_w, router_group_b, router_expert_w, router_expert_b, expert_w_gate, expert_w_up, expert_w_down, final_norm_g):
    p = dict(norm_mix_g=norm_mix_g, w_in=w_in, sgu_norm_g=sgu_norm_g, sgu_w=sgu_w, sgu_b=sgu_b,
             attn_q_norm_g=attn_q_norm_g, attn_k_norm_g=attn_k_norm_g, ret_decay=ret_decay, ret_norm_g=ret_norm_g,
             w_branch_a=w_branch_a, w_branch_b=w_branch_b, w_branch_c=w_branch_c, w_out=w_out,
             xattn_norm_g=xattn_norm_g, mem_norm_g=mem_norm_g, xattn_wq=xattn_wq, xattn_wkv=xattn_wkv,
             xattn_wo=xattn_wo, ffn_norm_g=ffn_norm_g, router_group_w=router_group_w, router_group_b=router_group_b,
             router_expert_w=router_expert_w, router_expert_b=router_expert_b, expert_w_gate=expert_w_gate,
             expert_w_up=expert_w_up, expert_w_down=expert_w_down)
    layers = [_prepare_layer(l, p) for l in range(DEPTH)]
    y_prompt = _trunk(x_prompt, mem_prompt, layers, final_norm_g)
    y_sample = _trunk(x_sample, mem_sample, layers, final_norm_g)
    return (y_prompt, y_sample)
```

```python
import functools

import jax
import jax.numpy as jnp
import numpy as np
from jax import lax
from jax.experimental import pallas as pl
from jax.experimental.pallas import tpu as pltpu

F32 = jnp.float32
BF16 = jnp.bfloat16

D_MODEL = 2048
DEPTH = 2
GRID_W = 64
ROPE_THETA = 10000.0
CHUNK = 128
HEAD_DIM = 128
NORM_EPS = 1e-6
SGU_WIDTH = 1024
SGU_GROUPS = 4
SGU_GROUP_W = SGU_WIDTH // SGU_GROUPS
ATTN_HEADS = 8
ATTN_KV_HEADS = 2
ATTN_REP = ATTN_HEADS // ATTN_KV_HEADS
ATTN_Q_W = ATTN_HEADS * HEAD_DIM
ATTN_KV_W = ATTN_KV_HEADS * HEAD_DIM
RET_HEADS = 4
RET_V_DIM = 256
RET_QK_W = RET_HEADS * HEAD_DIM
RET_V_W = RET_HEADS * RET_V_DIM
XATTN_HEADS = 4
XATTN_W = XATTN_HEADS * HEAD_DIM
N_GROUPS = 4
EXPERTS_PER_GROUP = 4
N_EXPERTS = N_GROUPS * EXPERTS_PER_GROUP
EXPERT_FF = 512
ROUTER_ROWS = 24
ROUTE_ROWS = 8

OFF_A_U = 0
OFF_A_V = OFF_A_U + SGU_WIDTH
OFF_B_Q = OFF_A_V + SGU_WIDTH
OFF_B_K = OFF_B_Q + ATTN_Q_W
OFF_B_V = OFF_B_K + ATTN_KV_W
OFF_C_Q = OFF_B_V + ATTN_KV_W
OFF_C_K = OFF_C_Q + RET_QK_W
OFF_C_V = OFF_C_K + RET_QK_W
OFF_C_G = OFF_C_V + RET_V_W
OFF_GATE_A = OFF_C_G + RET_V_W
OFF_GATE_B = OFF_GATE_A + D_MODEL
OFF_GATE_C = OFF_GATE_B + D_MODEL
IN_WIDTH = OFF_GATE_C + D_MODEL

V7X_VMEM_LIMIT_BYTES = 56 * 1024 * 1024

_NT = (((1,), (1,)), ((), ()))
LOG2_E = 1.4426950408889634
FIXED_SHIFT_LIMIT = 60.0


def _tile(n, pref):
    t = min(pref, n)
    while n % t:
        t //= 2
    return t


def _params(*sem):
    return pltpu.CompilerParams(dimension_semantics=sem, vmem_limit_bytes=V7X_VMEM_LIMIT_BYTES)


def _rms(x, g):
    return x * lax.rsqrt(jnp.mean(x * x, axis=-1, keepdims=True) + NORM_EPS) * g


def _rms_matmul_kernel(x_ref, g_ref, w_ref, o_ref, h_ref):
    @pl.when(pl.program_id(1) == 0)
    def _():
        h_ref[...] = _rms(x_ref[...], g_ref[...]).astype(BF16)

    o_ref[...] = jnp.dot(h_ref[...], w_ref[...], preferred_element_type=F32).astype(o_ref.dtype)


def _rms_matmul(x, g, w, *, tm, tn):
    t, d = x.shape
    n = w.shape[1]
    tm, tn = _tile(t, tm), _tile(n, tn)
    return pl.pallas_call(
        _rms_matmul_kernel,
        grid=(t // tm, n // tn),
        in_specs=[
            pl.BlockSpec((tm, d), lambda i, j: (i, 0)),
            pl.BlockSpec((1, d), lambda i, j: (0, 0)),
            pl.BlockSpec((d, tn), lambda i, j: (0, j)),
        ],
        out_specs=pl.BlockSpec((tm, tn), lambda i, j: (i, j)),
        out_shape=jax.ShapeDtypeStruct((t, n), BF16),
        scratch_shapes=[pltpu.VMEM((tm, d), BF16)],
        compiler_params=_params("parallel", "arbitrary"),
        name="rms_matmul",
    )(x, g.reshape(1, d), w)


def _gelu(x):
    return 0.5 * x * (1.0 + lax.erf(x * (2.0 ** -0.5)))


def _sgu_kernel(u_ref, v_ref, ng_ref, ws_ref, bs_ref, o_ref, *, n_chunks):
    for c in range(n_chunks):
        rows = slice(c * CHUNK, (c + 1) * CHUNK)
        u = _gelu(u_ref[rows, :].astype(F32))
        v = _gelu(v_ref[rows, :].astype(F32))
        vc = v - jnp.mean(v, axis=-1, keepdims=True)
        vn = vc * lax.rsqrt(jnp.mean(vc * vc, axis=-1, keepdims=True) + NORM_EPS) * ng_ref[...]
        vn = vn.astype(BF16)
        for g in range(SGU_GROUPS):
            cols = slice(g * SGU_GROUP_W, (g + 1) * SGU_GROUP_W)
            s = jnp.dot(ws_ref[g], vn[:, cols], preferred_element_type=F32) + bs_ref[:, g:g + 1]
            o_ref[rows, cols] = (u[:, cols] * s).astype(o_ref.dtype)


def _sgu(z3, norm_g, w_s, b_s_t, *, ts):
    b, s, _ = z3.shape
    ts = _tile(s, ts)
    return pl.pallas_call(
        functools.partial(_sgu_kernel, n_chunks=ts // CHUNK),
        grid=(b, s // ts),
        in_specs=[
            pl.BlockSpec((None, ts, SGU_WIDTH), lambda bi, i: (bi, i, OFF_A_U // SGU_WIDTH)),
            pl.BlockSpec((None, ts, SGU_WIDTH), lambda bi, i: (bi, i, OFF_A_V // SGU_WIDTH)),
            pl.BlockSpec((1, SGU_WIDTH), lambda bi, i: (0, 0)),
            pl.BlockSpec((SGU_GROUPS, CHUNK, CHUNK), lambda bi, i: (0, 0, 0)),
            pl.BlockSpec((CHUNK, SGU_GROUPS), lambda bi, i: (0, 0)),
        ],
        out_specs=pl.BlockSpec((None, ts, SGU_WIDTH), lambda bi, i: (bi, i, 0)),
        out_shape=jax.ShapeDtypeStruct((b, s, SGU_WIDTH), BF16),
        compiler_params=_params("parallel", "parallel"),
        name="sgu",
    )(z3, z3, norm_g.reshape(1, SGU_WIDTH), w_s, b_s_t)


def _rope(x, cos, sin_signed):
    lane = lax.broadcasted_iota(jnp.int32, x.shape, 1)
    first_half = (lane % (HEAD_DIM // 2)) < (HEAD_DIM // 4)
    partner = jnp.where(first_half, pltpu.roll(x, HEAD_DIM - HEAD_DIM // 4, 1), pltpu.roll(x, HEAD_DIM // 4, 1))
    return x * cos + partner * sin_signed


def _prep_kernel(bq_ref, bk_ref, bv_ref, cq_ref, ck_ref, cos_ref, sin_ref, qg_ref, kg_ref,
                 oq_ref, ok_ref, ov_ref, ocq_ref, ock_ref, qn_ref, kn_ref):
    cos = cos_ref[...]
    sin = sin_ref[...]
    scale = HEAD_DIM ** -0.5
    q_ss = None
    for h in range(ATTN_HEADS):
        cols = slice(h * HEAD_DIM, (h + 1) * HEAD_DIM)
        x = _rope(_rms(bq_ref[:, cols].astype(F32), qg_ref[...]), cos, sin) * (scale * LOG2_E)
        oq_ref[:, cols] = x.astype(BF16)
        ss = jnp.sum(x * x, axis=-1, keepdims=True)
        q_ss = ss if q_ss is None else jnp.maximum(q_ss, ss)
    k_ss = None
    for h in range(ATTN_KV_HEADS):
        cols = slice(h * HEAD_DIM, (h + 1) * HEAD_DIM)
        x = _rope(_rms(bk_ref[:, cols].astype(F32), kg_ref[...]), cos, sin)
        ok_ref[:, cols] = x.astype(BF16)
        ss = jnp.sum(x * x, axis=-1, keepdims=True)
        k_ss = ss if k_ss is None else jnp.maximum(k_ss, ss)
        ov_ref[:, 2 * h * HEAD_DIM:(2 * h + 1) * HEAD_DIM] = bv_ref[:, cols]
        ov_ref[:, (2 * h + 1) * HEAD_DIM:(2 * h + 2) * HEAD_DIM] = jnp.ones((bv_ref.shape[0], HEAD_DIM), BF16)
    qn_ref[...] = jnp.broadcast_to(jnp.max(q_ss, axis=0, keepdims=True), qn_ref.shape)
    kn_ref[...] = jnp.broadcast_to(jnp.max(k_ss, axis=0, keepdims=True), kn_ref.shape)
    for h in range(RET_HEADS):
        cols = slice(h * HEAD_DIM, (h + 1) * HEAD_DIM)
        ocq_ref[:, cols] = (_rope(cq_ref[:, cols].astype(F32), cos, sin) * scale).astype(BF16)
        ock_ref[:, cols] = _rope(ck_ref[:, cols].astype(F32), cos, sin).astype(BF16)


def _prep(z3, cos_t, sin_t, q_norm_g, k_norm_g, *, ts):
    b, s, _ = z3.shape
    ts = _tile(s, ts)
    n = s // ts

    def zspec(width, off):
        return pl.BlockSpec((None, ts, width), lambda bi, i: (bi, i, off // width))

    def ospec(width):
        return pl.BlockSpec((None, ts, width), lambda bi, i: (bi, i, 0))

    tab = pl.BlockSpec((ts, HEAD_DIM), lambda bi, i: (i, 0))
    gain = pl.BlockSpec((1, HEAD_DIM), lambda bi, i: (0, 0))
    nspec = pl.BlockSpec((None, None, 8, HEAD_DIM), lambda bi, i: (bi, i, 0, 0))
    widths = (ATTN_Q_W, ATTN_KV_W, 2 * ATTN_KV_W, RET_QK_W, RET_QK_W)
    return pl.pallas_call(
        _prep_kernel,
        grid=(b, n),
        in_specs=[zspec(ATTN_Q_W, OFF_B_Q), zspec(ATTN_KV_W, OFF_B_K), zspec(ATTN_KV_W, OFF_B_V),
                  zspec(RET_QK_W, OFF_C_Q), zspec(RET_QK_W, OFF_C_K), tab, tab, gain, gain],
        out_specs=[ospec(w) for w in widths] + [nspec, nspec],
        out_shape=[jax.ShapeDtypeStruct((b, s, w), BF16) for w in widths]
        + [jax.ShapeDtypeStruct((b, n, 8, HEAD_DIM), F32)] * 2,
        compiler_params=_params("parallel", "parallel"),
        name="qk_prep",
    )(z3, z3, z3, z3, z3, cos_t, sin_t, q_norm_g.reshape(1, HEAD_DIM), k_norm_g.reshape(1, HEAD_DIM))


def _flash_kernel(b2_ref, q_ref, k_ref, v_ref, o_ref, qc_ref, acc_ref, m_ref, *, tq, tk):
    for r in range(ATTN_REP):
        qc_ref[r * tq:(r + 1) * tq, :] = q_ref[:, r * HEAD_DIM:(r + 1) * HEAD_DIM]
    b2 = b2_ref[pl.program_id(0)]
    n_kv = k_ref.shape[0] // tk
    acc_ref[...] = jnp.zeros(acc_ref.shape, F32)

    def tiles(j):
        off = pl.multiple_of(j * tk, tk)
        k = k_ref[pl.ds(off, tk), :]
        return lax.dot_general(qc_ref[...], k, _NT, preferred_element_type=F32), v_ref[pl.ds(off, tk), :]

    @pl.when(b2 <= FIXED_SHIFT_LIMIT)
    def _():
        def body(j, carry):
            s, v = tiles(j)
            acc_ref[...] += jnp.dot(jnp.exp2(s - b2).astype(BF16), v, preferred_element_type=F32)
            return carry

        lax.fori_loop(0, n_kv, body, 0, unroll=8 if n_kv % 8 == 0 else 1)

    @pl.when(b2 > FIXED_SHIFT_LIMIT)
    def _():
        m_ref[...] = jnp.full(m_ref.shape, -jnp.inf, F32)

        def body(j, carry):
            s, v = tiles(j)
            m_old = m_ref[...]
            m_new = jnp.maximum(m_old, jnp.max(s, axis=-1, keepdims=True))
            p = jnp.exp2(s - m_new).astype(BF16)
            acc_ref[...] = jnp.exp2(m_old - m_new) * acc_ref[...] + jnp.dot(p, v, preferred_element_type=F32)
            m_ref[...] = m_new
            return carry

        lax.fori_loop(0, n_kv, body, 0)

    out = acc_ref[:, :HEAD_DIM] / acc_ref[:, HEAD_DIM:]
    for r in range(ATTN_REP):
        o_ref[:, r * HEAD_DIM:(r + 1) * HEAD_DIM] = out[r * tq:(r + 1) * tq, :].astype(o_ref.dtype)


def _flash(qr, kr, vaug, bound2, *, tq, tk):
    b, s, _ = qr.shape
    tq, tk = _tile(s, tq), _tile(s, tk)
    gw = ATTN_REP * HEAD_DIM
    grid_spec = pltpu.PrefetchScalarGridSpec(
        num_scalar_prefetch=1,
        grid=(b, ATTN_KV_HEADS, s // tq),
        in_specs=[
            pl.BlockSpec((None, tq, gw), lambda bi, g, i, b2: (bi, i, g)),
            pl.BlockSpec((None, s, HEAD_DIM), lambda bi, g, i, b2: (bi, 0, g)),
            pl.BlockSpec((None, s, 2 * HEAD_DIM), lambda bi, g, i, b2: (bi, 0, g)),
        ],
        out_specs=pl.BlockSpec((None, tq, gw), lambda bi, g, i, b2: (bi, i, g)),
        scratch_shapes=[
            pltpu.VMEM((ATTN_REP * tq, HEAD_DIM), BF16),
            pltpu.VMEM((ATTN_REP * tq, 2 * HEAD_DIM), F32),
            pltpu.VMEM((ATTN_REP * tq, 1), F32),
        ],
    )
    return pl.pallas_call(
        functools.partial(_flash_kernel, tq=tq, tk=tk),
        grid_spec=grid_spec,
        out_shape=jax.ShapeDtypeStruct((b, s, ATTN_Q_W), BF16),
        compiler_params=_params("parallel", "parallel", "arbitrary"),
        name="flash_attention",
    )(bound2, qr, kr, vaug)


def _score_bound(qn, kn):
    return jnp.sqrt(jnp.max(qn[:, :, 0, 0], axis=1) * jnp.max(kn[:, :, 0, 0], axis=1)) * 1.01


def _ret_chunk(q, k, v, st, dm, qd, kd, cd):
    s = lax.dot_general(q, k, _NT, preferred_element_type=F32) * dm
    o = jnp.dot(s.astype(BF16), v, preferred_element_type=F32)
    o = o + qd * jnp.dot(q, st.astype(BF16), preferred_element_type=F32)
    kt = (k.astype(F32).T * kd).astype(BF16)
    return o, cd * st + jnp.dot(kt, v, preferred_element_type=F32)


def _ret_head_views(refs, rows, h, width):
    half = RET_HEADS // 2
    return refs[h // half][rows, (h % half) * width:(h % half + 1) * width]


def _ret_fwd_kernel(q_ref, k_ref, v0_ref, v1_ref, dm_ref, qd_ref, kd_ref, cd_ref, o_ref, st_ref, *, n_chunks):
    @pl.when(pl.program_id(1) == 0)
    def _():
        st_ref[...] = jnp.zeros(st_ref.shape, F32)

    for c in range(n_chunks):
        rows = slice(c * CHUNK, (c + 1) * CHUNK)
        for h in range(RET_HEADS):
            qk = slice(h * HEAD_DIM, (h + 1) * HEAD_DIM)
            o, st = _ret_chunk(q_ref[rows, qk], k_ref[rows, qk], _ret_head_views((v0_ref, v1_ref), rows, h, RET_V_DIM),
                               st_ref[h], dm_ref[h], qd_ref[h], kd_ref[h], cd_ref[h])
            o_ref[rows, h * RET_V_DIM:(h + 1) * RET_V_DIM] = o
            st_ref[h] = st


def _ret_bwd_kernel(q_ref, k_ref, v0_ref, v1_ref, g0_ref, g1_ref, of_ref, dm_ref, qd_ref, kd_ref, cd_ref, ng_ref,
                    o_ref, st_ref, *, n_chunks):
    @pl.when(pl.program_id(1) == 0)
    def _():
        st_ref[...] = jnp.zeros(st_ref.shape, F32)

    for c in reversed(range(n_chunks)):
        rows = slice(c * CHUNK, (c + 1) * CHUNK)
        for h in range(RET_HEADS):
            qk = slice(h * HEAD_DIM, (h + 1) * HEAD_DIM)
            vc = slice(h * RET_V_DIM, (h + 1) * RET_V_DIM)
            o, st = _ret_chunk(q_ref[rows, qk], k_ref[rows, qk], _ret_head_views((v0_ref, v1_ref), rows, h, RET_V_DIM),
                               st_ref[h], dm_ref[h], qd_ref[h], kd_ref[h], cd_ref[h])
            st_ref[h] = st
            o = o + of_ref[rows, vc]
            oc = o - jnp.mean(o, axis=-1, keepdims=True)
            o = oc * lax.rsqrt(jnp.mean(oc * oc, axis=-1, keepdims=True) + NORM_EPS) * ng_ref[:, vc]
            gate = _ret_head_views((g0_ref, g1_ref), rows, h, RET_V_DIM).astype(F32)
            o_ref[rows, vc] = (o * (gate * jax.nn.sigmoid(gate))).astype(o_ref.dtype)


def _retention(cqr, ckr, z3, tabs, norm_g, *, ts):
    b, s, _ = cqr.shape
    ts = _tile(s, ts)
    n = s // ts
    dv = RET_V_DIM
    half_w = RET_V_W // 2

    def seq(width, off, rev, part=0):
        if rev:
            return pl.BlockSpec((None, ts, width), lambda bi, i: (bi, n - 1 - i, off // width + part))
        return pl.BlockSpec((None, ts, width), lambda bi, i: (bi, i, off // width + part))

    def whole(*shape):
        return pl.BlockSpec(shape, lambda bi, i: (0,) * len(shape))

    tab_specs = [whole(RET_HEADS, CHUNK, CHUNK), whole(RET_HEADS, CHUNK, 1), whole(RET_HEADS, 1, CHUNK),
                 whole(RET_HEADS, 1, dv)]
    state = [pltpu.VMEM((RET_HEADS, HEAD_DIM, dv), F32)]
    sem = _params("parallel", "arbitrary")
    o_fwd = pl.pallas_call(
        functools.partial(_ret_fwd_kernel, n_chunks=ts // CHUNK),
        grid=(b, n),
        in_specs=[seq(RET_QK_W, 0, False), seq(RET_QK_W, 0, False), seq(half_w, OFF_C_V, False, 0),
                  seq(half_w, OFF_C_V, False, 1)] + tab_specs,
        out_specs=seq(RET_V_W, 0, False),
        out_shape=jax.ShapeDtypeStruct((b, s, RET_V_W), F32),
        scratch_shapes=state,
        compiler_params=sem,
        name="retention_fwd",
    )(cqr, ckr, z3, z3, *tabs[0])
    return pl.pallas_call(
        functools.partial(_ret_bwd_kernel, n_chunks=ts // CHUNK),
        grid=(b, n),
        in_specs=[seq(RET_QK_W, 0, True), seq(RET_QK_W, 0, True), seq(half_w, OFF_C_V, True, 0),
                  seq(half_w, OFF_C_V, True, 1), seq(half_w, OFF_C_G, True, 0), seq(half_w, OFF_C_G, True, 1),
                  seq(RET_V_W, 0, True)] + tab_specs + [whole(1, RET_V_W)],
        out_specs=seq(RET_V_W, 0, True),
        out_shape=jax.ShapeDtypeStruct((b, s, RET_V_W), BF16),
        scratch_shapes=state,
        compiler_params=sem,
        name="retention_bwd",
    )(cqr, ckr, z3, z3, z3, z3, o_fwd, *tabs[1], norm_g.reshape(1, RET_V_W))


def _retention_tables(ret_decay):
    lg = -jnp.exp(ret_decay.astype(F32))
    idx = jnp.arange(CHUNK, dtype=F32)
    diff = idx[:, None] - idx[None, :]
    out = []
    for d, (mask, delta, qpow, kpow) in enumerate((
            (diff >= 0, diff, idx + 1.0, CHUNK - 1.0 - idx),
            (diff < 0, -diff, CHUNK - idx, idx))):
        l = lg[d][:, None, None]
        dm = jnp.where(mask, jnp.exp(l * jnp.where(mask, delta, 0.0)), 0.0)
        qd = jnp.exp(lg[d][:, None] * qpow)[:, :, None]
        kd = jnp.exp(lg[d][:, None] * kpow)[:, None, :]
        cd = jnp.broadcast_to(jnp.exp(lg[d] * CHUNK)[:, None, None], (RET_HEADS, 1, RET_V_DIM))
        out.append((dm, qd, kd, cd))
    return out


def _merge_kernel(a_ref, b_ref, c_ref, wa_ref, wb_ref, wc_ref, ga_ref, gb_ref, gc_ref, o_ref):
    acc = jax.nn.sigmoid(ga_ref[...].astype(F32)) * jnp.dot(a_ref[...], wa_ref[...], preferred_element_type=F32)
    acc = acc + jax.nn.sigmoid(gb_ref[...].astype(F32)) * jnp.dot(b_ref[...], wb_ref[...], preferred_element_type=F32)
    acc = acc + jax.nn.sigmoid(gc_ref[...].astype(F32)) * jnp.dot(c_ref[...], wc_ref[...], preferred_element_type=F32)
    o_ref[...] = acc.astype(o_ref.dtype)


def _merge(out_a, out_b, out_c, w_a, w_b, w_c, z, *, tm, tn):
    t, k = out_a.shape
    tm, tn = _tile(t, tm), _tile(D_MODEL, tn)
    lhs = pl.BlockSpec((tm, k), lambda i, j: (i, 0))
    rhs = pl.BlockSpec((k, tn), lambda i, j: (0, j))

    def gate(off):
        return pl.BlockSpec((tm, tn), lambda i, j: (i, off // tn + j))

    return pl.pallas_call(
        _merge_kernel,
        grid=(t // tm, D_MODEL // tn),
        in_specs=[lhs, lhs, lhs, rhs, rhs, rhs, gate(OFF_GATE_A), gate(OFF_GATE_B), gate(OFF_GATE_C)],
        out_specs=pl.BlockSpec((tm, tn), lambda i, j: (i, j)),
        out_shape=jax.ShapeDtypeStruct((t, D_MODEL), BF16),
        compiler_params=_params("parallel", "parallel"),
        name="branch_merge",
    )(out_a, out_b, out_c, w_a, w_b, w_c, z, z, z)


def _matmul_res_kernel(a_ref, w_ref, x_ref, o_ref):
    o_ref[...] = x_ref[...] + jnp.dot(a_ref[...], w_ref[...], preferred_element_type=F32)


def _matmul_res(a, w, x, *, tm, tn):
    t, k = a.shape
    n = w.shape[1]
    tm, tn = _tile(t, tm), _tile(n, tn)
    return pl.pallas_call(
        _matmul_res_kernel,
        grid=(t // tm, n // tn),
        in_specs=[
            pl.BlockSpec((tm, k), lambda i, j: (i, 0)),
            pl.BlockSpec((k, tn), lambda i, j: (0, j)),
            pl.BlockSpec((tm, tn), lambda i, j: (i, j)),
        ],
        out_specs=pl.BlockSpec((tm, tn), lambda i, j: (i, j)),
        out_shape=jax.ShapeDtypeStruct((t, n), F32),
        compiler_params=_params("parallel", "parallel"),
        name="out_proj_residual",
    )(a, w, x)


def _route(h, wr, br):
    lt = lax.dot_general(wr, h, _NT, precision=lax.Precision.HIGHEST, preferred_element_type=F32) + br
    gl = [lt[i:i + 1, :] for i in range(N_GROUPS)]
    gmax = jnp.maximum(jnp.maximum(gl[0], gl[1]), jnp.maximum(gl[2], gl[3]))
    denom = sum(jnp.exp(v - gmax) for v in gl)
    g_prob = 1.0 / denom
    g_idx = jnp.where(gl[0] == gmax, 0, jnp.where(gl[1] == gmax, 1, jnp.where(gl[2] == gmax, 2, 3)))
    sel = []
    for j in range(EXPERTS_PER_GROUP):
        rows = [lt[N_GROUPS + g * EXPERTS_PER_GROUP + j:N_GROUPS + g * EXPERTS_PER_GROUP + j + 1, :]
                for g in range(N_GROUPS)]
        sel.append(jnp.where(g_idx == 0, rows[0], jnp.where(g_idx == 1, rows[1],
                                                            jnp.where(g_idx == 2, rows[2], rows[3]))))
    top1 = jnp.maximum(jnp.maximum(sel[0], sel[1]), jnp.maximum(sel[2], sel[3]))
    i1 = jnp.where(sel[0] == top1, 0, jnp.where(sel[1] == top1, 1, jnp.where(sel[2] == top1, 2, 3)))
    rest = [jnp.where(i1 == j, -jnp.inf, sel[j]) for j in range(EXPERTS_PER_GROUP)]
    top2 = jnp.maximum(jnp.maximum(rest[0], rest[1]), jnp.maximum(rest[2], rest[3]))
    i2 = jnp.where(rest[0] == top2, 0, jnp.where(rest[1] == top2, 1, jnp.where(rest[2] == top2, 2, 3)))
    e2 = jnp.exp(top2 - top1)
    w1 = g_prob / (1.0 + e2)
    w2 = g_prob * e2 / (1.0 + e2)
    row = lax.broadcasted_iota(jnp.int32, (ROUTE_ROWS, h.shape[0]), 0)
    return (jnp.where(row == i1, w1, 0.0) + jnp.where(row == i2, w2, 0.0)
            + jnp.where(row == EXPERTS_PER_GROUP, g_idx.astype(F32), 0.0))


def _xattn_kernel(x_ref, g_ref, wq_ref, kv_ref, wo_ref, fg_ref, wr_ref, br_ref, o_ref, rt_ref):
    x = x_ref[...]
    h = _rms(x, g_ref[...]).astype(BF16)
    q = (jnp.dot(h, wq_ref[...], preferred_element_type=F32) * (HEAD_DIM ** -0.5)).astype(BF16)
    heads = []
    for hh in range(XATTN_HEADS):
        cols = slice(hh * HEAD_DIM, (hh + 1) * HEAD_DIM)
        k = kv_ref[:, cols]
        v = kv_ref[:, XATTN_W + hh * HEAD_DIM:XATTN_W + (hh + 1) * HEAD_DIM]
        s = lax.dot_general(q[:, cols], k, _NT, preferred_element_type=F32)
        p = jnp.exp(s - jnp.max(s, axis=-1, keepdims=True))
        o = jnp.dot(p.astype(BF16), v, preferred_element_type=F32) / jnp.sum(p, axis=-1, keepdims=True)
        heads.append(o.astype(BF16))
    y = x + jnp.dot(jnp.concatenate(heads, axis=1), wo_ref[...], preferred_element_type=F32)
    o_ref[...] = y
    rt_ref[...] = _route(_rms(y, fg_ref[...]), wr_ref[...], br_ref[...])


def _xattn(x3, norm_g, wq, kv3, wo, ffn_norm_g, wr_t, br, *, tm):
    b, s, d = x3.shape
    m = kv3.shape[1]
    tm = _tile(s, tm)
    n = s // tm

    def whole(*shape):
        return pl.BlockSpec(shape, lambda bi, i: (0,) * len(shape))

    return pl.pallas_call(
        _xattn_kernel,
        grid=(b, n),
        in_specs=[
            pl.BlockSpec((None, tm, d), lambda bi, i: (bi, i, 0)),
            whole(1, d),
            whole(d, XATTN_W),
            pl.BlockSpec((None, m, 2 * XATTN_W), lambda bi, i: (bi, 0, 0)),
            whole(XATTN_W, d),
            whole(1, d),
            whole(ROUTER_ROWS, d),
            whole(ROUTER_ROWS, 1),
        ],
        out_specs=[pl.BlockSpec((None, tm, d), lambda bi, i: (bi, i, 0)),
                   pl.BlockSpec((ROUTE_ROWS, tm), lambda bi, i: (0, bi * n + i))],
        out_shape=[jax.ShapeDtypeStruct((b, s, d), F32), jax.ShapeDtypeStruct((ROUTE_ROWS, b * s), F32)],
        compiler_params=_params("parallel", "parallel"),
        name="memory_cross_attention",
    )(x3, norm_g.reshape(1, d), wq, kv3, wo, ffn_norm_g.reshape(1, d), wr_t, br)


def _row_gather_kernel(idx_ref, src_ref, dst_ref, sem, *, rows):
    def start(r, carry):
        pltpu.make_async_copy(src_ref.at[pl.ds(idx_ref[0, r], 1), :], dst_ref.at[pl.ds(r, 1), :], sem).start()
        return carry

    lax.fori_loop(0, rows, start, 0, unroll=8)
    pltpu.make_async_copy(src_ref.at[pl.ds(0, rows), :], dst_ref, sem).wait()


def _row_gather(src, idx, *, rows):
    n = idx.shape[0]
    d = src.shape[1]
    rows = _tile(n, rows)
    return pl.pallas_call(
        functools.partial(_row_gather_kernel, rows=rows),
        grid=(n // rows,),
        in_specs=[
            pl.BlockSpec((None, 1, rows), lambda i: (i, 0, 0), memory_space=pltpu.SMEM),
            pl.BlockSpec(memory_space=pl.ANY),
        ],
        out_specs=pl.BlockSpec((rows, d), lambda i: (i, 0)),
        out_shape=jax.ShapeDtypeStruct((n, d), src.dtype),
        scratch_shapes=[pltpu.SemaphoreType.DMA(())],
        compiler_params=_params("arbitrary"),
        name="row_gather",
    )(idx.reshape(n // rows, 1, rows), src)


def _moe_kernel(tg_ref, nu_ref, x_ref, w_ref, ng_ref, wg_ref, wu_ref, wd_ref, fg_ref, o_ref, h_ref, acc_ref, *,
                final_norm):
    i = pl.program_id(0)
    j = pl.program_id(1)
    used = i < nu_ref[0]

    @pl.when(j == 0)
    def _():
        acc_ref[...] = jnp.zeros(acc_ref.shape, F32)
        h_ref[...] = _rms(x_ref[...], ng_ref[...]).astype(BF16)

    @pl.when(used)
    def _():
        h = h_ref[...]
        a = jnp.dot(h, wg_ref[...], preferred_element_type=F32)
        hid = (a * jax.nn.sigmoid(a)) * jnp.dot(h, wu_ref[...], preferred_element_type=F32)
        w = w_ref[...]
        lane = lax.broadcasted_iota(jnp.int32, w.shape, 1)
        wj = jnp.sum(jnp.where(lane == j, w, 0.0), axis=-1, keepdims=True)
        acc_ref[...] += jnp.dot(hid.astype(BF16), wd_ref[...], preferred_element_type=F32) * wj

    @pl.when(j == pl.num_programs(1) - 1)
    def _():
        y = x_ref[...] + acc_ref[...]
        if final_norm:
            y = _rms(y, fg_ref[...])
        o_ref[...] = y


def _moe_grouped(xp, wp, tile_group, n_used, norm_g, w_gate, w_up, w_down, final_g, *, tm, final_norm):
    p, d = xp.shape

    def expert(i, j, tg, nu):
        return (tg[i] * EXPERTS_PER_GROUP + j, 0, 0)

    grid_spec = pltpu.PrefetchScalarGridSpec(
        num_scalar_prefetch=2,
        grid=(p // tm, EXPERTS_PER_GROUP),
        in_specs=[
            pl.BlockSpec((tm, d), lambda i, j, tg, nu: (i, 0)),
            pl.BlockSpec((tm, EXPERTS_PER_GROUP), lambda i, j, tg, nu: (i, 0)),
            pl.BlockSpec((1, d), lambda i, j, tg, nu: (0, 0)),
            pl.BlockSpec((None, d, EXPERT_FF), expert),
            pl.BlockSpec((None, d, EXPERT_FF), expert),
            pl.BlockSpec((None, EXPERT_FF, d), expert),
            pl.BlockSpec((1, d), lambda i, j, tg, nu: (0, 0)),
        ],
        out_specs=pl.BlockSpec((tm, d), lambda i, j, tg, nu: (i, 0)),
        scratch_shapes=[pltpu.VMEM((tm, d), BF16), pltpu.VMEM((tm, d), F32)],
    )
    return pl.pallas_call(
        functools.partial(_moe_kernel, final_norm=final_norm),
        grid_spec=grid_spec,
        out_shape=jax.ShapeDtypeStruct((p, d), F32),
        compiler_params=_params("parallel", "arbitrary"),
        name="experts",
    )(tile_group, n_used, xp, wp, norm_g.reshape(1, d), w_gate, w_up, w_down, final_g.reshape(1, d))


def _group_plan(route, tm):
    t = route.shape[1]
    gid = route[EXPERTS_PER_GROUP].astype(jnp.int32)
    onehot = (gid[:, None] == jnp.arange(N_GROUPS, dtype=jnp.int32)[None, :]).astype(jnp.int32)
    csum = jnp.cumsum(onehot, axis=0)
    count = csum[-1]
    rank = jnp.take_along_axis(csum, gid[:, None], axis=1)[:, 0] - 1
    padded = (count + tm - 1) // tm * tm
    end = jnp.cumsum(padded)
    pos = (end - padded)[gid] + rank
    n_tiles = t // tm + N_GROUPS
    src = jnp.zeros((n_tiles * tm,), jnp.int32).at[pos].set(jnp.arange(t, dtype=jnp.int32))
    tile_start = jnp.arange(n_tiles, dtype=jnp.int32) * tm
    tile_group = jnp.minimum(jnp.sum((tile_start[:, None] >= end[None, :]).astype(jnp.int32), axis=1), N_GROUPS - 1)
    n_used = (end[-1] // tm).reshape(1)
    return pos, src, tile_group, n_used


def _moe(x, route, norm_g, w_gate, w_up, w_down, final_g, *, tm, final_norm):
    pos, src, tile_group, n_used = _group_plan(route, tm)
    xp = _row_gather(x, src, rows=512)
    wp = route[:EXPERTS_PER_GROUP].T[src]
    yp = _moe_grouped(xp, wp, tile_group, n_used, norm_g, w_gate, w_up, w_down, final_g, tm=tm,
                      final_norm=final_norm)
    return _row_gather(yp, pos, rows=512)


def _rope_tables(seq_len):
    rows = seq_len // GRID_W
    row = jnp.repeat(jnp.arange(rows, dtype=F32), GRID_W)
    col = jnp.tile(jnp.arange(GRID_W, dtype=F32), rows)
    n_freq = HEAD_DIM // 4
    inv = ROPE_THETA ** (-jnp.arange(n_freq, dtype=F32) / n_freq)
    ar = row[:, None] * inv
    ac = col[:, None] * inv
    cos = jnp.concatenate([jnp.cos(ar), jnp.cos(ar), jnp.cos(ac), jnp.cos(ac)], axis=1)
    sin = jnp.concatenate([-jnp.sin(ar), jnp.sin(ar), -jnp.sin(ac), jnp.sin(ac)], axis=1)
    return cos, sin


def _prepare_layer(l, p):
    wr_t = jnp.concatenate([p["router_group_w"][l].T, p["router_expert_w"][l].T,
                            jnp.zeros((ROUTER_ROWS - N_GROUPS - N_EXPERTS, D_MODEL), F32)], axis=0)
    br = jnp.concatenate([p["router_group_b"][l], p["router_expert_b"][l].reshape(-1),
                          jnp.zeros((ROUTER_ROWS - N_GROUPS - N_EXPERTS,), F32)]).reshape(ROUTER_ROWS, 1)
    return dict(
        norm_mix_g=p["norm_mix_g"][l], w_in=p["w_in"][l].astype(BF16),
        sgu_norm_g=p["sgu_norm_g"][l], sgu_w=p["sgu_w"][l].astype(BF16), sgu_b_t=p["sgu_b"][l].T,
        q_norm_g=p["attn_q_norm_g"][l], k_norm_g=p["attn_k_norm_g"][l],
        ret_tabs=_retention_tables(p["ret_decay"][l]), ret_norm_g=p["ret_norm_g"][l],
        w_a=p["w_branch_a"][l].astype(BF16), w_b=p["w_branch_b"][l].astype(BF16),
        w_c=p["w_branch_c"][l].astype(BF16), w_out=p["w_out"][l].astype(BF16),
        xattn_norm_g=p["xattn_norm_g"][l], mem_norm_g=p["mem_norm_g"][l],
        wq=p["xattn_wq"][l].astype(BF16), wkv=p["xattn_wkv"][l].astype(BF16), wo=p["xattn_wo"][l].astype(BF16),
        ffn_norm_g=p["ffn_norm_g"][l], wr_t=wr_t, br=br,
        w_gate=p["expert_w_gate"][l].astype(BF16), w_up=p["expert_w_up"][l].astype(BF16),
        w_down=p["expert_w_down"][l].astype(BF16),
    )


def _trunk(x, mem, layers, final_norm_g):
    b, s, d = x.shape
    t = b * s
    m = mem.shape[1]
    cos_t, sin_t = _rope_tables(s)
    xt = x.reshape(t, d)
    memt = mem.reshape(b * m, d)
    for l, w in enumerate(layers):
        z = _rms_matmul(xt, w["norm_mix_g"], w["w_in"], tm=1024, tn=1280)
        z3 = z.reshape(b, s, IN_WIDTH)
        out_a = _sgu(z3, w["sgu_norm_g"], w["sgu_w"], w["sgu_b_t"], ts=512)
        qr, kr, vaug, cqr, ckr, qn, kn = _prep(z3, cos_t, sin_t, w["q_norm_g"], w["k_norm_g"], ts=512)
        out_b = _flash(qr, kr, vaug, _score_bound(qn, kn), tq=256, tk=512)
        out_c = _retention(cqr, ckr, z3, w["ret_tabs"], w["ret_norm_g"], ts=512)
        merged = _merge(out_a.reshape(t, SGU_WIDTH), out_b.reshape(t, ATTN_Q_W), out_c.reshape(t, RET_V_W),
                        w["w_a"], w["w_b"], w["w_c"], z, tm=1024, tn=512)
        xt = _matmul_res(merged, w["w_out"], xt, tm=1024, tn=1024)
        kv = _rms_matmul(memt, w["mem_norm_g"], w["wkv"], tm=256, tn=512)
        x3, route = _xattn(xt.reshape(b, s, d), w["xattn_norm_g"], w["wq"], kv.reshape(b, m, 2 * XATTN_W), w["wo"],
                           w["ffn_norm_g"], w["wr_t"], w["br"], tm=512)
        xt = x3.reshape(t, d)
        xt = _moe(xt, route, w["ffn_norm_g"], w["w_gate"], w["w_up"], w["w_down"], final_norm_g, tm=512,
                  final_norm=(l == len(layers) - 1))
    return xt.reshape(b, s, d)


def kernel(x_prompt, x_sample, mem_prompt, mem_sample, norm_mix_g, w_in, sgu_norm_g, sgu_w, sgu_b, attn_q_norm_g, attn_k_norm_g, ret_decay, ret_norm_g, w_branch_a, w_branch_b, w_branch_c, w_out, xattn_norm_g, mem_norm_g, xattn_wq, xattn_wkv, xattn_wo, ffn_norm_g, router_group_w, router_group_b, router_expert_w, router_expert_b, expert_w_gate, expert_w_up, expert_w_down, final_norm_g):
    p = dict(norm_mix_g=norm_mix_g, w_in=w_in, sgu_norm_g=sgu_norm_g, sgu_w=sgu_w, sgu_b=sgu_b,
             attn_q_norm_g=attn_q_norm_g, attn_k_norm_g=attn_k_norm_g, ret_decay=ret_decay, ret_norm_g=ret_norm_g,
             w_branch_a=w_branch_a, w_branch_b=w_branch_b, w_branch_c=w_branch_c, w_out=w_out,
             xattn_norm_g=xattn_norm_g, mem_norm_g=mem_norm_g, xattn_wq=xattn_wq, xattn_wkv=xattn_wkv,
             xattn_wo=xattn_wo, ffn_norm_g=ffn_norm_g, router_group_w=router_group_w, router_group_b=router_group_b,
             router_expert_w=router_expert_w, router_expert_b=router_expert_b, expert_w_gate=expert_w_gate,
             expert_w_up=expert_w_up, expert_w_down=expert_w_down)
    layers = [_prepare_layer(l, p) for l in range(DEPTH)]
    y_prompt = _trunk(x_prompt, mem_prompt, layers, final_norm_g)
    y_sample = _trunk(x_sample, mem_sample, layers, final_norm_g)
    return (y_prompt, y_sample)
```

```python
import functools

import jax
import jax.numpy as jnp
import numpy as np
from jax import lax
from jax.experimental import pallas as pl
from jax.experimental.pallas import tpu as pltpu

F32 = jnp.float32
BF16 = jnp.bfloat16

D_MODEL = 2048
DEPTH = 2
GRID_W = 64
ROPE_THETA = 10000.0
CHUNK = 128
HEAD_DIM = 128
NORM_EPS = 1e-6
SGU_WIDTH = 1024
SGU_GROUPS = 4
SGU_GROUP_W = SGU_WIDTH // SGU_GROUPS
ATTN_HEADS = 8
ATTN_KV_HEADS = 2
ATTN_REP = ATTN_HEADS // ATTN_KV_HEADS
ATTN_Q_W = ATTN_HEADS * HEAD_DIM
ATTN_KV_W = ATTN_KV_HEADS * HEAD_DIM
RET_HEADS = 4
RET_V_DIM = 256
RET_QK_W = RET_HEADS * HEAD_DIM
RET_V_W = RET_HEADS * RET_V_DIM
XATTN_HEADS = 4
XATTN_W = XATTN_HEADS * HEAD_DIM
N_GROUPS = 4
EXPERTS_PER_GROUP = 4
N_EXPERTS = N_GROUPS * EXPERTS_PER_GROUP
EXPERT_FF = 512
ROUTER_ROWS = 24
ROUTE_ROWS = 8

OFF_A_U = 0
OFF_A_V = OFF_A_U + SGU_WIDTH
OFF_B_Q = OFF_A_V + SGU_WIDTH
OFF_B_K = OFF_B_Q + ATTN_Q_W
OFF_B_V = OFF_B_K + ATTN_KV_W
OFF_C_Q = OFF_B_V + ATTN_KV_W
OFF_C_K = OFF_C_Q + RET_QK_W
OFF_C_V = OFF_C_K + RET_QK_W
OFF_C_G = OFF_C_V + RET_V_W
OFF_GATE_A = OFF_C_G + RET_V_W
OFF_GATE_B = OFF_GATE_A + D_MODEL
OFF_GATE_C = OFF_GATE_B + D_MODEL
IN_WIDTH = OFF_GATE_C + D_MODEL

V7X_VMEM_LIMIT_BYTES = 56 * 1024 * 1024

_NT = (((1,), (1,)), ((), ()))
LOG2_E = 1.4426950408889634
FIXED_SHIFT_LIMIT = 60.0


def _tile(n, pref):
    t = min(pref, n)
    while n % t:
        t //= 2
    return t


def _params(*sem):
    return pltpu.CompilerParams(dimension_semantics=sem, vmem_limit_bytes=V7X_VMEM_LIMIT_BYTES)


def _rms(x, g):
    return x * lax.rsqrt(jnp.mean(x * x, axis=-1, keepdims=True) + NORM_EPS) * g


def _rms_matmul_kernel(x_ref, g_ref, w_ref, o_ref, h_ref):
    @pl.when(pl.program_id(1) == 0)
    def _():
        h_ref[...] = _rms(x_ref[...], g_ref[...]).astype(BF16)

    o_ref[...] = jnp.dot(h_ref[...], w_ref[...], preferred_element_type=F32).astype(o_ref.dtype)


def _rms_matmul(x, g, w, *, tm, tn):
    t, d = x.shape
    n = w.shape[1]
    tm, tn = _tile(t, tm), _tile(n, tn)
    return pl.pallas_call(
        _rms_matmul_kernel,
        grid=(t // tm, n // tn),
        in_specs=[
            pl.BlockSpec((tm, d), lambda i, j: (i, 0)),
            pl.BlockSpec((1, d), lambda i, j: (0, 0)),
            pl.BlockSpec((d, tn), lambda i, j: (0, j)),
        ],
        out_specs=pl.BlockSpec((tm, tn), lambda i, j: (i, j)),
        out_shape=jax.ShapeDtypeStruct((t, n), BF16),
        scratch_shapes=[pltpu.VMEM((tm, d), BF16)],
        compiler_params=_params("parallel", "arbitrary"),
        name="rms_matmul",
    )(x, g.reshape(1, d), w)


def _gelu(x):
    return 0.5 * x * (1.0 + lax.erf(x * (2.0 ** -0.5)))


def _sgu_kernel(u_ref, v_ref, ng_ref, ws_ref, bs_ref, o_ref, *, n_chunks):
    for c in range(n_chunks):
        rows = slice(c * CHUNK, (c + 1) * CHUNK)
        u = _gelu(u_ref[rows, :].astype(F32))
        v = _gelu(v_ref[rows, :].astype(F32))
        vc = v - jnp.mean(v, axis=-1, keepdims=True)
        vn = vc * lax.rsqrt(jnp.mean(vc * vc, axis=-1, keepdims=True) + NORM_EPS) * ng_ref[...]
        vn = vn.astype(BF16)
        for g in range(SGU_GROUPS):
            cols = slice(g * SGU_GROUP_W, (g + 1) * SGU_GROUP_W)
            s = jnp.dot(ws_ref[g], vn[:, cols], preferred_element_type=F32) + bs_ref[:, g:g + 1]
            o_ref[rows, cols] = (u[:, cols] * s).astype(o_ref.dtype)


def _sgu(z3, norm_g, w_s, b_s_t, *, ts):
    b, s, _ = z3.shape
    ts = _tile(s, ts)
    return pl.pallas_call(
        functools.partial(_sgu_kernel, n_chunks=ts // CHUNK),
        grid=(b, s // ts),
        in_specs=[
            pl.BlockSpec((None, ts, SGU_WIDTH), lambda bi, i: (bi, i, OFF_A_U // SGU_WIDTH)),
            pl.BlockSpec((None, ts, SGU_WIDTH), lambda bi, i: (bi, i, OFF_A_V // SGU_WIDTH)),
            pl.BlockSpec((1, SGU_WIDTH), lambda bi, i: (0, 0)),
            pl.BlockSpec((SGU_GROUPS, CHUNK, CHUNK), lambda bi, i: (0, 0, 0)),
            pl.BlockSpec((CHUNK, SGU_GROUPS), lambda bi, i: (0, 0)),
        ],
        out_specs=pl.BlockSpec((None, ts, SGU_WIDTH), lambda bi, i: (bi, i, 0)),
        out_shape=jax.ShapeDtypeStruct((b, s, SGU_WIDTH), BF16),
        compiler_params=_params("parallel", "parallel"),
        name="sgu",
    )(z3, z3, norm_g.reshape(1, SGU_WIDTH), w_s, b_s_t)


def _rope(x, cos, sin_signed):
    return x * cos + pltpu.roll(x, HEAD_DIM // 2, 1) * sin_signed


def _prep_kernel(bq_ref, bk_ref, bv_ref, cq_ref, ck_ref, cos_ref, sin_ref, qg_ref, kg_ref,
                 oq_ref, ok_ref, ov_ref, ocq_ref, ock_ref):
    cos = cos_ref[...]
    sin = sin_ref[...]
    scale = HEAD_DIM ** -0.5
    for h in range(ATTN_HEADS):
        cols = slice(h * HEAD_DIM, (h + 1) * HEAD_DIM)
        x = _rope(_rms(bq_ref[:, cols].astype(F32), qg_ref[...]), cos, sin) * (scale * LOG2_E)
        oq_ref[:, cols] = x.astype(BF16)
    for h in range(ATTN_KV_HEADS):
        cols = slice(h * HEAD_DIM, (h + 1) * HEAD_DIM)
        x = _rope(_rms(bk_ref[:, cols].astype(F32), kg_ref[...]), cos, sin)
        ok_ref[:, cols] = x.astype(BF16)
        ov_ref[:, 2 * h * HEAD_DIM:(2 * h + 1) * HEAD_DIM] = bv_ref[:, cols]
        ov_ref[:, (2 * h + 1) * HEAD_DIM:(2 * h + 2) * HEAD_DIM] = jnp.ones((bv_ref.shape[0], HEAD_DIM), BF16)
    for h in range(RET_HEADS):
        cols = slice(h * HEAD_DIM, (h + 1) * HEAD_DIM)
        ocq_ref[:, cols] = (_rope(cq_ref[:, cols].astype(F32), cos, sin) * scale).astype(BF16)
        ock_ref[:, cols] = _rope(ck_ref[:, cols].astype(F32), cos, sin).astype(BF16)


def _prep(z3, cos_t, sin_t, q_norm_g, k_norm_g, *, ts):
    b, s, _ = z3.shape
    ts = _tile(s, ts)
    n = s // ts

    def zspec(width, off):
        return pl.BlockSpec((None, ts, width), lambda bi, i: (bi, i, off // width))

    def ospec(width):
        return pl.BlockSpec((None, ts, width), lambda bi, i: (bi, i, 0))

    tab = pl.BlockSpec((ts, HEAD_DIM), lambda bi, i: (i, 0))
    gain = pl.BlockSpec((1, HEAD_DIM), lambda bi, i: (0, 0))
    widths = (ATTN_Q_W, ATTN_KV_W, 2 * ATTN_KV_W, RET_QK_W, RET_QK_W)
    return pl.pallas_call(
        _prep_kernel,
        grid=(b, n),
        in_specs=[zspec(ATTN_Q_W, OFF_B_Q), zspec(ATTN_KV_W, OFF_B_K), zspec(ATTN_KV_W, OFF_B_V),
                  zspec(RET_QK_W, OFF_C_Q), zspec(RET_QK_W, OFF_C_K), tab, tab, gain, gain],
        out_specs=[ospec(w) for w in widths],
        out_shape=[jax.ShapeDtypeStruct((b, s, w), BF16) for w in widths],
        compiler_params=_params("parallel", "parallel"),
        name="qk_prep",
    )(z3, z3, z3, z3, z3, cos_t, sin_t, q_norm_g.reshape(1, HEAD_DIM), k_norm_g.reshape(1, HEAD_DIM))


def _flash_kernel(b2_ref, q_ref, k_ref, v_ref, o_ref, qc_ref, acc_ref, m_ref, *, tq, tk):
    for r in range(ATTN_REP):
        qc_ref[r * tq:(r + 1) * tq, :] = q_ref[:, r * HEAD_DIM:(r + 1) * HEAD_DIM]
    b2 = b2_ref[pl.program_id(0)]
    n_kv = k_ref.shape[0] // tk
    acc_ref[...] = jnp.zeros(acc_ref.shape, F32)

    def tiles(j):
        off = pl.multiple_of(j * tk, tk)
        k = k_ref[pl.ds(off, tk), :]
        return lax.dot_general(qc_ref[...], k, _NT, preferred_element_type=F32), v_ref[pl.ds(off, tk), :]

    @pl.when(b2 <= FIXED_SHIFT_LIMIT)
    def _():
        def body(j, carry):
            s, v = tiles(j)
            acc_ref[...] += jnp.dot(jnp.exp2(s - b2).astype(BF16), v, preferred_element_type=F32)
            return carry

        lax.fori_loop(0, n_kv, body, 0, unroll=8 if n_kv % 8 == 0 else 1)

    @pl.when(b2 > FIXED_SHIFT_LIMIT)
    def _():
        m_ref[...] = jnp.full(m_ref.shape, -jnp.inf, F32)

        def body(j, carry):
            s, v = tiles(j)
            m_old = m_ref[...]
            m_new = jnp.maximum(m_old, jnp.max(s, axis=-1, keepdims=True))
            p = jnp.exp2(s - m_new).astype(BF16)
            acc_ref[...] = jnp.exp2(m_old - m_new) * acc_ref[...] + jnp.dot(p, v, preferred_element_type=F32)
            m_ref[...] = m_new
            return carry

        lax.fori_loop(0, n_kv, body, 0)

    out = acc_ref[:, :HEAD_DIM] / acc_ref[:, HEAD_DIM:]
    for r in range(ATTN_REP):
        o_ref[:, r * HEAD_DIM:(r + 1) * HEAD_DIM] = out[r * tq:(r + 1) * tq, :].astype(o_ref.dtype)


def _flash(qr, kr, vaug, bound2, *, tq, tk):
    b, s, _ = qr.shape
    tq, tk = _tile(s, tq), _tile(s, tk)
    gw = ATTN_REP * HEAD_DIM
    grid_spec = pltpu.PrefetchScalarGridSpec(
        num_scalar_prefetch=1,
        grid=(b, ATTN_KV_HEADS, s // tq),
        in_specs=[
            pl.BlockSpec((None, tq, gw), lambda bi, g, i, b2: (bi, i, g)),
            pl.BlockSpec((None, s, HEAD_DIM), lambda bi, g, i, b2: (bi, 0, g)),
            pl.BlockSpec((None, s, 2 * HEAD_DIM), lambda bi, g, i, b2: (bi, 0, g)),
        ],
        out_specs=pl.BlockSpec((None, tq, gw), lambda bi, g, i, b2: (bi, i, g)),
        scratch_shapes=[
            pltpu.VMEM((ATTN_REP * tq, HEAD_DIM), BF16),
            pltpu.VMEM((ATTN_REP * tq, 2 * HEAD_DIM), F32),
            pltpu.VMEM((ATTN_REP * tq, 1), F32),
        ],
    )
    return pl.pallas_call(
        functools.partial(_flash_kernel, tq=tq, tk=tk),
        grid_spec=grid_spec,
        out_shape=jax.ShapeDtypeStruct((b, s, ATTN_Q_W), BF16),
        compiler_params=_params("parallel", "parallel", "arbitrary"),
        name="flash_attention",
    )(bound2, qr, kr, vaug)


def _score_bound(q_norm_g, k_norm_g, batch):
    bound = LOG2_E * HEAD_DIM ** 0.5 * jnp.max(jnp.abs(q_norm_g)) * jnp.max(jnp.abs(k_norm_g)) * 1.01
    return jnp.full((batch,), bound, F32)


def _ret_chunk(q, k, v, st, dm, qd, kd, cd):
    s = lax.dot_general(q, k, _NT, preferred_element_type=F32) * dm
    o = jnp.dot(s.astype(BF16), v, preferred_element_type=F32)
    o = o + qd * jnp.dot(q, st.astype(BF16), preferred_element_type=F32)
    kt = (k.astype(F32).T * kd).astype(BF16)
    return o, cd * st + jnp.dot(kt, v, preferred_element_type=F32)


def _ret_head_views(refs, rows, h, width):
    half = RET_HEADS // 2
    return refs[h // half][rows, (h % half) * width:(h % half + 1) * width]


def _ret_fwd_kernel(q_ref, k_ref, v0_ref, v1_ref, dm_ref, qd_ref, kd_ref, cd_ref, o_ref, st_ref, *, n_chunks):
    @pl.when(pl.program_id(1) == 0)
    def _():
        st_ref[...] = jnp.zeros(st_ref.shape, F32)

    for c in range(n_chunks):
        rows = slice(c * CHUNK, (c + 1) * CHUNK)
        for h in range(RET_HEADS):
            qk = slice(h * HEAD_DIM, (h + 1) * HEAD_DIM)
            o, st = _ret_chunk(q_ref[rows, qk], k_ref[rows, qk], _ret_head_views((v0_ref, v1_ref), rows, h, RET_V_DIM),
                               st_ref[h], dm_ref[h], qd_ref[h], kd_ref[h], cd_ref[h])
            o_ref[rows, h * RET_V_DIM:(h + 1) * RET_V_DIM] = o
            st_ref[h] = st


def _ret_bwd_kernel(q_ref, k_ref, v0_ref, v1_ref, g0_ref, g1_ref, of_ref, dm_ref, qd_ref, kd_ref, cd_ref, ng_ref,
                    o_ref, st_ref, *, n_chunks):
    @pl.when(pl.program_id(1) == 0)
    def _():
        st_ref[...] = jnp.zeros(st_ref.shape, F32)

    for c in reversed(range(n_chunks)):
        rows = slice(c * CHUNK, (c + 1) * CHUNK)
        for h in range(RET_HEADS):
            qk = slice(h * HEAD_DIM, (h + 1) * HEAD_DIM)
            vc = slice(h * RET_V_DIM, (h + 1) * RET_V_DIM)
            o, st = _ret_chunk(q_ref[rows, qk], k_ref[rows, qk], _ret_head_views((v0_ref, v1_ref), rows, h, RET_V_DIM),
                               st_ref[h], dm_ref[h], qd_ref[h], kd_ref[h], cd_ref[h])
            st_ref[h] = st
            o = o + of_ref[rows, vc]
            oc = o - jnp.mean(o, axis=-1, keepdims=True)
            o = oc * lax.rsqrt(jnp.mean(oc * oc, axis=-1, keepdims=True) + NORM_EPS) * ng_ref[:, vc]
            gate = _ret_head_views((g0_ref, g1_ref), rows, h, RET_V_DIM).astype(F32)
            o_ref[rows, vc] = (o * (gate * jax.nn.sigmoid(gate))).astype(o_ref.dtype)


def _retention(cqr, ckr, z3, tabs, norm_g, *, ts):
    b, s, _ = cqr.shape
    ts = _tile(s, ts)
    n = s // ts
    dv = RET_V_DIM
    half_w = RET_V_W // 2

    def seq(width, off, rev, part=0):
        if rev:
            return pl.BlockSpec((None, ts, width), lambda bi, i: (bi, n - 1 - i, off // width + part))
        return pl.BlockSpec((None, ts, width), lambda bi, i: (bi, i, off // width + part))

    def whole(*shape):
        return pl.BlockSpec(shape, lambda bi, i: (0,) * len(shape))

    tab_specs = [whole(RET_HEADS, CHUNK, CHUNK), whole(RET_HEADS, CHUNK, 1), whole(RET_HEADS, 1, CHUNK),
                 whole(RET_HEADS, 1, dv)]
    state = [pltpu.VMEM((RET_HEADS, HEAD_DIM, dv), F32)]
    sem = _params("parallel", "arbitrary")
    o_fwd = pl.pallas_call(
        functools.partial(_ret_fwd_kernel, n_chunks=ts // CHUNK),
        grid=(b, n),
        in_specs=[seq(RET_QK_W, 0, False), seq(RET_QK_W, 0, False), seq(half_w, OFF_C_V, False, 0),
                  seq(half_w, OFF_C_V, False, 1)] + tab_specs,
        out_specs=seq(RET_V_W, 0, False),
        out_shape=jax.ShapeDtypeStruct((b, s, RET_V_W), F32),
        scratch_shapes=state,
        compiler_params=sem,
        name="retention_fwd",
    )(cqr, ckr, z3, z3, *tabs[0])
    return pl.pallas_call(
        functools.partial(_ret_bwd_kernel, n_chunks=ts // CHUNK),
        grid=(b, n),
        in_specs=[seq(RET_QK_W, 0, True), seq(RET_QK_W, 0, True), seq(half_w, OFF_C_V, True, 0),
                  seq(half_w, OFF_C_V, True, 1), seq(half_w, OFF_C_G, True, 0), seq(half_w, OFF_C_G, True, 1),
                  seq(RET_V_W, 0, True)] + tab_specs + [whole(1, RET_V_W)],
        out_specs=seq(RET_V_W, 0, True),
        out_shape=jax.ShapeDtypeStruct((b, s, RET_V_W), BF16),
        scratch_shapes=state,
        compiler_params=sem,
        name="retention_bwd",
    )(cqr, ckr, z3, z3, z3, z3, o_fwd, *tabs[1], norm_g.reshape(1, RET_V_W))


def _retention_tables(ret_decay):
    lg = -jnp.exp(ret_decay.astype(F32))
    idx = jnp.arange(CHUNK, dtype=F32)
    diff = idx[:, None] - idx[None, :]
    out = []
    for d, (mask, delta, qpow, kpow) in enumerate((
            (diff >= 0, diff, idx + 1.0, CHUNK - 1.0 - idx),
            (diff < 0, -diff, CHUNK - idx, idx))):
        l = lg[d][:, None, None]
        dm = jnp.where(mask, jnp.exp(l * jnp.where(mask, delta, 0.0)), 0.0)
        qd = jnp.exp(lg[d][:, None] * qpow)[:, :, None]
        kd = jnp.exp(lg[d][:, None] * kpow)[:, None, :]
        cd = jnp.broadcast_to(jnp.exp(lg[d] * CHUNK)[:, None, None], (RET_HEADS, 1, RET_V_DIM))
        out.append((dm, qd, kd, cd))
    return out


def _merge_kernel(a_ref, b_ref, c_ref, wa_ref, wb_ref, wc_ref, ga_ref, gb_ref, gc_ref, o_ref):
    acc = jax.nn.sigmoid(ga_ref[...].astype(F32)) * jnp.dot(a_ref[...], wa_ref[...], preferred_element_type=F32)
    acc = acc + jax.nn.sigmoid(gb_ref[...].astype(F32)) * jnp.dot(b_ref[...], wb_ref[...], preferred_element_type=F32)
    acc = acc + jax.nn.sigmoid(gc_ref[...].astype(F32)) * jnp.dot(c_ref[...], wc_ref[...], preferred_element_type=F32)
    o_ref[...] = acc.astype(o_ref.dtype)


def _merge(out_a, out_b, out_c, w_a, w_b, w_c, z, *, tm, tn):
    t, k = out_a.shape
    tm, tn = _tile(t, tm), _tile(D_MODEL, tn)
    lhs = pl.BlockSpec((tm, k), lambda i, j: (i, 0))
    rhs = pl.BlockSpec((k, tn), lambda i, j: (0, j))

    def gate(off):
        return pl.BlockSpec((tm, tn), lambda i, j: (i, off // tn + j))

    return pl.pallas_call(
        _merge_kernel,
        grid=(t // tm, D_MODEL // tn),
        in_specs=[lhs, lhs, lhs, rhs, rhs, rhs, gate(OFF_GATE_A), gate(OFF_GATE_B), gate(OFF_GATE_C)],
        out_specs=pl.BlockSpec((tm, tn), lambda i, j: (i, j)),
        out_shape=jax.ShapeDtypeStruct((t, D_MODEL), BF16),
        compiler_params=_params("parallel", "parallel"),
        name="branch_merge",
    )(out_a, out_b, out_c, w_a, w_b, w_c, z, z, z)


def _matmul_res_kernel(a_ref, w_ref, x_ref, o_ref):
    o_ref[...] = x_ref[...] + jnp.dot(a_ref[...], w_ref[...], preferred_element_type=F32)


def _matmul_res(a, w, x, *, tm, tn):
    t, k = a.shape
    n = w.shape[1]
    tm, tn = _tile(t, tm), _tile(n, tn)
    return pl.pallas_call(
        _matmul_res_kernel,
        grid=(t // tm, n // tn),
        in_specs=[
            pl.BlockSpec((tm, k), lambda i, j: (i, 0)),
            pl.BlockSpec((k, tn), lambda i, j: (0, j)),
            pl.BlockSpec((tm, tn), lambda i, j: (i, j)),
        ],
        out_specs=pl.BlockSpec((tm, tn), lambda i, j: (i, j)),
        out_shape=jax.ShapeDtypeStruct((t, n), F32),
        compiler_params=_params("parallel", "parallel"),
        name="out_proj_residual",
    )(a, w, x)


def _route(h, wr, br):
    lt = lax.dot_general(wr, h, _NT, precision=lax.Precision.HIGHEST, preferred_element_type=F32) + br
    gl = [lt[i:i + 1, :] for i in range(N_GROUPS)]
    gmax = jnp.maximum(jnp.maximum(gl[0], gl[1]), jnp.maximum(gl[2], gl[3]))
    denom = sum(jnp.exp(v - gmax) for v in gl)
    g_prob = 1.0 / denom
    g_idx = jnp.where(gl[0] == gmax, 0, jnp.where(gl[1] == gmax, 1, jnp.where(gl[2] == gmax, 2, 3)))
    sel = []
    for j in range(EXPERTS_PER_GROUP):
        rows = [lt[N_GROUPS + g * EXPERTS_PER_GROUP + j:N_GROUPS + g * EXPERTS_PER_GROUP + j + 1, :]
                for g in range(N_GROUPS)]
        sel.append(jnp.where(g_idx == 0, rows[0], jnp.where(g_idx == 1, rows[1],
                                                            jnp.where(g_idx == 2, rows[2], rows[3]))))
    top1 = jnp.maximum(jnp.maximum(sel[0], sel[1]), jnp.maximum(sel[2], sel[3]))
    i1 = jnp.where(sel[0] == top1, 0, jnp.where(sel[1] == top1, 1, jnp.where(sel[2] == top1, 2, 3)))
    rest = [jnp.where(i1 == j, -jnp.inf, sel[j]) for j in range(EXPERTS_PER_GROUP)]
    top2 = jnp.maximum(jnp.maximum(rest[0], rest[1]), jnp.maximum(rest[2], rest[3]))
    i2 = jnp.where(rest[0] == top2, 0, jnp.where(rest[1] == top2, 1, jnp.where(rest[2] == top2, 2, 3)))
    e2 = jnp.exp(top2 - top1)
    w1 = g_prob / (1.0 + e2)
    w2 = g_prob * e2 / (1.0 + e2)
    row = lax.broadcasted_iota(jnp.int32, (ROUTE_ROWS, h.shape[0]), 0)
    return (jnp.where(row == i1, w1, 0.0) + jnp.where(row == i2, w2, 0.0)
            + jnp.where(row == EXPERTS_PER_GROUP, g_idx.astype(F32), 0.0))


def _xattn_kernel(x_ref, g_ref, wq_ref, kv_ref, wo_ref, fg_ref, wr_ref, br_ref, o_ref, rt_ref):
    x = x_ref[...]
    h = _rms(x, g_ref[...]).astype(BF16)
    q = (jnp.dot(h, wq_ref[...], preferred_element_type=F32) * (HEAD_DIM ** -0.5)).astype(BF16)
    heads = []
    for hh in range(XATTN_HEADS):
        cols = slice(hh * HEAD_DIM, (hh + 1) * HEAD_DIM)
        k = kv_ref[:, cols]
        v = kv_ref[:, XATTN_W + hh * HEAD_DIM:XATTN_W + (hh + 1) * HEAD_DIM]
        s = lax.dot_general(q[:, cols], k, _NT, preferred_element_type=F32)
        p = jnp.exp(s - jnp.max(s, axis=-1, keepdims=True))
        o = jnp.dot(p.astype(BF16), v, preferred_element_type=F32) / jnp.sum(p, axis=-1, keepdims=True)
        heads.append(o.astype(BF16))
    y = x + jnp.dot(jnp.concatenate(heads, axis=1), wo_ref[...], preferred_element_type=F32)
    o_ref[...] = y
    rt_ref[...] = _route(_rms(y, fg_ref[...]), wr_ref[...], br_ref[...])


def _xattn(x3, norm_g, wq, kv3, wo, ffn_norm_g, wr_t, br, *, tm):
    b, s, d = x3.shape
    m = kv3.shape[1]
    tm = _tile(s, tm)
    n = s // tm

    def whole(*shape):
        return pl.BlockSpec(shape, lambda bi, i: (0,) * len(shape))

    return pl.pallas_call(
        _xattn_kernel,
        grid=(b, n),
        in_specs=[
            pl.BlockSpec((None, tm, d), lambda bi, i: (bi, i, 0)),
            whole(1, d),
            whole(d, XATTN_W),
            pl.BlockSpec((None, m, 2 * XATTN_W), lambda bi, i: (bi, 0, 0)),
            whole(XATTN_W, d),
            whole(1, d),
            whole(ROUTER_ROWS, d),
            whole(ROUTER_ROWS, 1),
        ],
        out_specs=[pl.BlockSpec((None, tm, d), lambda bi, i: (bi, i, 0)),
                   pl.BlockSpec((ROUTE_ROWS, tm), lambda bi, i: (0, bi * n + i))],
        out_shape=[jax.ShapeDtypeStruct((b, s, d), F32), jax.ShapeDtypeStruct((ROUTE_ROWS, b * s), F32)],
        compiler_params=_params("parallel", "parallel"),
        name="memory_cross_attention",
    )(x3, norm_g.reshape(1, d), wq, kv3, wo, ffn_norm_g.reshape(1, d), wr_t, br)


def _row_gather_kernel(idx_ref, src_ref, dst_ref, sem, *, rows):
    def start(r, carry):
        pltpu.make_async_copy(src_ref.at[pl.ds(idx_ref[0, r], 1), :], dst_ref.at[pl.ds(r, 1), :], sem).start()
        return carry

    lax.fori_loop(0, rows, start, 0, unroll=8)
    pltpu.make_async_copy(src_ref.at[pl.ds(0, rows), :], dst_ref, sem).wait()


def _row_gather(src, idx, *, rows):
    n = idx.shape[0]
    d = src.shape[1]
    rows = _tile(n, rows)
    return pl.pallas_call(
        functools.partial(_row_gather_kernel, rows=rows),
        grid=(n // rows,),
        in_specs=[
            pl.BlockSpec((None, 1, rows), lambda i: (i, 0, 0), memory_space=pltpu.SMEM),
            pl.BlockSpec(memory_space=pl.ANY),
        ],
        out_specs=pl.BlockSpec((rows, d), lambda i: (i, 0)),
        out_shape=jax.ShapeDtypeStruct((n, d), src.dtype),
        scratch_shapes=[pltpu.SemaphoreType.DMA(())],
        compiler_params=_params("arbitrary"),
        name="row_gather",
    )(idx.reshape(n // rows, 1, rows), src)


def _moe_kernel(tg_ref, nu_ref, x_ref, w_ref, ng_ref, wg_ref, wu_ref, wd_ref, fg_ref, o_ref, h_ref, acc_ref, *,
                final_norm):
    i = pl.program_id(0)
    j = pl.program_id(1)
    used = i < nu_ref[0]

    @pl.when(j == 0)
    def _():
        acc_ref[...] = jnp.zeros(acc_ref.shape, F32)
        h_ref[...] = _rms(x_ref[...], ng_ref[...]).astype(BF16)

    @pl.when(used)
    def _():
        h = h_ref[...]
        a = jnp.dot(h, wg_ref[...], preferred_element_type=F32)
        hid = (a * jax.nn.sigmoid(a)) * jnp.dot(h, wu_ref[...], preferred_element_type=F32)
        w = w_ref[...]
        lane = lax.broadcasted_iota(jnp.int32, w.shape, 1)
        wj = jnp.sum(jnp.where(lane == j, w, 0.0), axis=-1, keepdims=True)
        acc_ref[...] += jnp.dot(hid.astype(BF16), wd_ref[...], preferred_element_type=F32) * wj

    @pl.when(j == pl.num_programs(1) - 1)
    def _():
        y = x_ref[...] + acc_ref[...]
        if final_norm:
            y = _rms(y, fg_ref[...])
        o_ref[...] = y


def _moe_grouped(xp, wp, tile_group, n_used, norm_g, w_gate, w_up, w_down, final_g, *, tm, final_norm):
    p, d = xp.shape

    def expert(i, j, tg, nu):
        return (tg[i] * EXPERTS_PER_GROUP + j, 0, 0)

    grid_spec = pltpu.PrefetchScalarGridSpec(
        num_scalar_prefetch=2,
        grid=(p // tm, EXPERTS_PER_GROUP),
        in_specs=[
            pl.BlockSpec((tm, d), lambda i, j, tg, nu: (i, 0)),
            pl.BlockSpec((tm, EXPERTS_PER_GROUP), lambda i, j, tg, nu: (i, 0)),
            pl.BlockSpec((1, d), lambda i, j, tg, nu: (0, 0)),
            pl.BlockSpec((None, d, EXPERT_FF), expert),
            pl.BlockSpec((None, d, EXPERT_FF), expert),
            pl.BlockSpec((None, EXPERT_FF, d), expert),
            pl.BlockSpec((1, d), lambda i, j, tg, nu: (0, 0)),
        ],
        out_specs=pl.BlockSpec((tm, d), lambda i, j, tg, nu: (i, 0)),
        scratch_shapes=[pltpu.VMEM((tm, d), BF16), pltpu.VMEM((tm, d), F32)],
    )
    return pl.pallas_call(
        functools.partial(_moe_kernel, final_norm=final_norm),
        grid_spec=grid_spec,
        out_shape=jax.ShapeDtypeStruct((p, d), F32),
        compiler_params=_params("parallel", "arbitrary"),
        name="experts",
    )(tile_group, n_used, xp, wp, norm_g.reshape(1, d), w_gate, w_up, w_down, final_g.reshape(1, d))


def _group_plan(route, tm):
    t = route.shape[1]
    gid = route[EXPERTS_PER_GROUP].astype(jnp.int32)
    onehot = (gid[:, None] == jnp.arange(N_GROUPS, dtype=jnp.int32)[None, :]).astype(jnp.int32)
    csum = jnp.cumsum(onehot, axis=0)
    count = csum[-1]
    rank = jnp.take_along_axis(csum, gid[:, None], axis=1)[:, 0] - 1
    padded = (count + tm - 1) // tm * tm
    end = jnp.cumsum(padded)
    pos = (end - padded)[gid] + rank
    n_tiles = t // tm + N_GROUPS
    src = jnp.zeros((n_tiles * tm,), jnp.int32).at[pos].set(jnp.arange(t, dtype=jnp.int32))
    tile_start = jnp.arange(n_tiles, dtype=jnp.int32) * tm
    tile_group = jnp.minimum(jnp.sum((tile_start[:, None] >= end[None, :]).astype(jnp.int32), axis=1), N_GROUPS - 1)
    n_used = (end[-1] // tm).reshape(1)
    return pos, src, tile_group, n_used


def _moe(x, route, norm_g, w_gate, w_up, w_down, final_g, *, tm, final_norm):
    pos, src, tile_group, n_used = _group_plan(route, tm)
    xp = _row_gather(x, src, rows=512)
    wp = route[:EXPERTS_PER_GROUP].T[src]
    yp = _moe_grouped(xp, wp, tile_group, n_used, norm_g, w_gate, w_up, w_down, final_g, tm=tm,
                      final_norm=final_norm)
    return _row_gather(yp, pos, rows=512)


def _rope_tables(seq_len):
    rows = seq_len // GRID_W
    row = jnp.repeat(jnp.arange(rows, dtype=F32), GRID_W)
    col = jnp.tile(jnp.arange(GRID_W, dtype=F32), rows)
    n_freq = HEAD_DIM // 4
    inv = ROPE_THETA ** (-jnp.arange(n_freq, dtype=F32) / n_freq)
    ar = row[:, None] * inv
    ac = col[:, None] * inv
    cos = jnp.concatenate([jnp.cos(ar), jnp.cos(ac), jnp.cos(ar), jnp.cos(ac)], axis=1)
    sin = jnp.concatenate([-jnp.sin(ar), -jnp.sin(ac), jnp.sin(ar), jnp.sin(ac)], axis=1)
    return cos, sin


def _pair_major(a):
    lead = a.shape[:-1]
    heads = a.shape[-1] // HEAD_DIM
    a = a.reshape(lead + (heads, 2, 2, HEAD_DIM // 4))
    return jnp.swapaxes(a, -3, -2).reshape(lead + (heads * HEAD_DIM,))


def _rotary_layout(w_in):
    parts = [w_in[:, :OFF_B_Q], _pair_major(w_in[:, OFF_B_Q:OFF_B_V]), w_in[:, OFF_B_V:OFF_C_Q],
             _pair_major(w_in[:, OFF_C_Q:OFF_C_V]), w_in[:, OFF_C_V:]]
    return jnp.concatenate(parts, axis=1)


def _prepare_layer(l, p):
    wr_t = jnp.concatenate([p["router_group_w"][l].T, p["router_expert_w"][l].T,
                            jnp.zeros((ROUTER_ROWS - N_GROUPS - N_EXPERTS, D_MODEL), F32)], axis=0)
    br = jnp.concatenate([p["router_group_b"][l], p["router_expert_b"][l].reshape(-1),
                          jnp.zeros((ROUTER_ROWS - N_GROUPS - N_EXPERTS,), F32)]).reshape(ROUTER_ROWS, 1)
    return dict(
        norm_mix_g=p["norm_mix_g"][l], w_in=_rotary_layout(p["w_in"][l].astype(BF16)),
        sgu_norm_g=p["sgu_norm_g"][l], sgu_w=p["sgu_w"][l].astype(BF16), sgu_b_t=p["sgu_b"][l].T,
        q_norm_g=_pair_major(p["attn_q_norm_g"][l]), k_norm_g=_pair_major(p["attn_k_norm_g"][l]),
        ret_tabs=_retention_tables(p["ret_decay"][l]), ret_norm_g=p["ret_norm_g"][l],
        w_a=p["w_branch_a"][l].astype(BF16), w_b=p["w_branch_b"][l].astype(BF16),
        w_c=p["w_branch_c"][l].astype(BF16), w_out=p["w_out"][l].astype(BF16),
        xattn_norm_g=p["xattn_norm_g"][l], mem_norm_g=p["mem_norm_g"][l],
        wq=p["xattn_wq"][l].astype(BF16), wkv=p["xattn_wkv"][l].astype(BF16), wo=p["xattn_wo"][l].astype(BF16),
        ffn_norm_g=p["ffn_norm_g"][l], wr_t=wr_t, br=br,
        w_gate=p["expert_w_gate"][l].astype(BF16), w_up=p["expert_w_up"][l].astype(BF16),
        w_down=p["expert_w_down"][l].astype(BF16),
    )


def _trunk(x, mem, layers, final_norm_g):
    b, s, d = x.shape
    t = b * s
    m = mem.shape[1]
    cos_t, sin_t = _rope_tables(s)
    xt = x.reshape(t, d)
    memt = mem.reshape(b * m, d)
    for l, w in enumerate(layers):
        z = _rms_matmul(xt, w["norm_mix_g"], w["w_in"], tm=1024, tn=1280)
        z3 = z.reshape(b, s, IN_WIDTH)
        out_a = _sgu(z3, w["sgu_norm_g"], w["sgu_w"], w["sgu_b_t"], ts=512)
        qr, kr, vaug, cqr, ckr = _prep(z3, cos_t, sin_t, w["q_norm_g"], w["k_norm_g"], ts=512)
        out_b = _flash(qr, kr, vaug, _score_bound(w["q_norm_g"], w["k_norm_g"], b), tq=512, tk=256)
        out_c = _retention(cqr, ckr, z3, w["ret_tabs"], w["ret_norm_g"], ts=512)
        merged = _merge(out_a.reshape(t, SGU_WIDTH), out_b.reshape(t, ATTN_Q_W), out_c.reshape(t, RET_V_W),
                        w["w_a"], w["w_b"], w["w_c"], z, tm=1024, tn=512)
        xt = _matmul_res(merged, w["w_out"], xt, tm=1024, tn=1024)
        kv = _rms_matmul(memt, w["mem_norm_g"], w["wkv"], tm=256, tn=512)
        x3, route = _xattn(xt.reshape(b, s, d), w["xattn_norm_g"], w["wq"], kv.reshape(b, m, 2 * XATTN_W), w["wo"],
                           w["ffn_norm_g"], w["wr_t"], w["br"], tm=512)
        xt = x3.reshape(t, d)
        xt = _moe(xt, route, w["ffn_norm_g"], w["w_gate"], w["w_up"], w["w_down"], final_norm_g, tm=512,
                  final_norm=(l == len(layers) - 1))
    return xt.reshape(b, s, d)


def kernel(x_prompt, x_sample, mem_prompt, mem_sample, norm_mix_g, w_in, sgu_norm_g, sgu_w, sgu_b, attn_q_norm_g, attn_k_norm_g, ret_decay, ret_norm_g, w_branch_a, w_branch_b, w_branch_c, w_out, xattn_norm_g, mem_norm_g, xattn_wq, xattn_wkv, xattn_wo, ffn_norm_g, router_group_w, router_group_b, router_expert_w, router_expert_b, expert_w_gate, expert_w_up, expert_w_down, final_norm_g):
    p = dict(norm_mix_g=norm_mix_g, w_in=w_in, sgu_norm_g=sgu_norm_g, sgu_w=sgu_w, sgu_b=sgu_b,
             attn_q_norm_g=attn_q_norm_g, attn_k_norm_g=attn_k_norm_g, ret_decay=ret_decay, ret_norm_g=ret_norm_g,
             w_branch_a=w_branch_a, w_branch_b=w_branch_b, w_branch_c=w_branch_c, w_out=w_out,
             xattn_norm_g=xattn_norm_g, mem_norm_g=mem_norm_g, xattn_wq=xattn_wq, xattn_wkv=xattn_wkv,
             xattn_wo=xattn_wo, ffn_norm_g=ffn_norm_g, router_group_w=router_group_w, router_group_b=router_group_b,
             router_expert_w=router_expert_w, router_expert_b=router_expert_b, expert_w_gate=expert_w_gate,
             expert_w_up=expert_w_up, expert_w_down=expert_w_down)
    layers = [_prepare_layer(l, p) for l in range(DEPTH)]
    y_prompt = _trunk(x_prompt, mem_prompt, layers, final_norm_g)
    y_sample = _trunk(x_sample, mem_sample, layers, final_norm_g)
    return (y_prompt, y_sample)
```

```python
import functools

import jax
import jax.numpy as jnp
import numpy as np
from jax import lax
from jax.experimental import pallas as pl
from jax.experimental.pallas import tpu as pltpu

F32 = jnp.float32
BF16 = jnp.bfloat16

D_MODEL = 2048
DEPTH = 2
GRID_W = 64
ROPE_THETA = 10000.0
CHUNK = 128
HEAD_DIM = 128
NORM_EPS = 1e-6
SGU_WIDTH = 1024
SGU_GROUPS = 4
SGU_GROUP_W = SGU_WIDTH // SGU_GROUPS
ATTN_HEADS = 8
ATTN_KV_HEADS = 2
ATTN_REP = ATTN_HEADS // ATTN_KV_HEADS
ATTN_Q_W = ATTN_HEADS * HEAD_DIM
ATTN_KV_W = ATTN_KV_HEADS * HEAD_DIM
RET_HEADS = 4
RET_V_DIM = 256
RET_QK_W = RET_HEADS * HEAD_DIM
RET_V_W = RET_HEADS * RET_V_DIM
XATTN_HEADS = 4
XATTN_W = XATTN_HEADS * HEAD_DIM
N_GROUPS = 4
EXPERTS_PER_GROUP = 4
N_EXPERTS = N_GROUPS * EXPERTS_PER_GROUP
EXPERT_FF = 512
ROUTER_ROWS = 24
ROUTE_ROWS = 8

OFF_A_U = 0
OFF_A_V = OFF_A_U + SGU_WIDTH
OFF_B_Q = OFF_A_V + SGU_WIDTH
OFF_B_K = OFF_B_Q + ATTN_Q_W
OFF_B_V = OFF_B_K + ATTN_KV_W
OFF_C_Q = OFF_B_V + ATTN_KV_W
OFF_C_K = OFF_C_Q + RET_QK_W
OFF_C_V = OFF_C_K + RET_QK_W
OFF_C_G = OFF_C_V + RET_V_W
OFF_GATE_A = OFF_C_G + RET_V_W
OFF_GATE_B = OFF_GATE_A + D_MODEL
OFF_GATE_C = OFF_GATE_B + D_MODEL
IN_WIDTH = OFF_GATE_C + D_MODEL

V7X_VMEM_LIMIT_BYTES = 56 * 1024 * 1024

_NT = (((1,), (1,)), ((), ()))
LOG2_E = 1.4426950408889634
FIXED_SHIFT_LIMIT = 60.0


def _tile(n, pref):
    t = min(pref, n)
    while n % t:
        t //= 2
    return t


def _params(*sem):
    return pltpu.CompilerParams(dimension_semantics=sem, vmem_limit_bytes=V7X_VMEM_LIMIT_BYTES)


def _rms(x, g):
    return x * lax.rsqrt(jnp.mean(x * x, axis=-1, keepdims=True) + NORM_EPS) * g


def _rms_matmul_kernel(x_ref, g_ref, w_ref, o_ref, h_ref):
    @pl.when(pl.program_id(1) == 0)
    def _():
        h_ref[...] = _rms(x_ref[...], g_ref[...]).astype(BF16)

    o_ref[...] = jnp.dot(h_ref[...], w_ref[...], preferred_element_type=F32).astype(o_ref.dtype)


def _rms_matmul(x, g, w, *, tm, tn):
    t, d = x.shape
    n = w.shape[1]
    tm, tn = _tile(t, tm), _tile(n, tn)
    return pl.pallas_call(
        _rms_matmul_kernel,
        grid=(t // tm, n // tn),
        in_specs=[
            pl.BlockSpec((tm, d), lambda i, j: (i, 0)),
            pl.BlockSpec((1, d), lambda i, j: (0, 0)),
            pl.BlockSpec((d, tn), lambda i, j: (0, j)),
        ],
        out_specs=pl.BlockSpec((tm, tn), lambda i, j: (i, j)),
        out_shape=jax.ShapeDtypeStruct((t, n), BF16),
        scratch_shapes=[pltpu.VMEM((tm, d), BF16)],
        compiler_params=_params("parallel", "arbitrary"),
        name="rms_matmul",
    )(x, g.reshape(1, d), w)


def _gelu(x):
    return 0.5 * x * (1.0 + lax.erf(x * (2.0 ** -0.5)))


def _sgu_kernel(u_ref, v_ref, ng_ref, ws_ref, bs_ref, o_ref, *, n_chunks):
    for c in range(n_chunks):
        rows = slice(c * CHUNK, (c + 1) * CHUNK)
        u = _gelu(u_ref[rows, :].astype(F32))
        v = _gelu(v_ref[rows, :].astype(F32))
        vc = v - jnp.mean(v, axis=-1, keepdims=True)
        vn = vc * lax.rsqrt(jnp.mean(vc * vc, axis=-1, keepdims=True) + NORM_EPS) * ng_ref[...]
        vn = vn.astype(BF16)
        for g in range(SGU_GROUPS):
            cols = slice(g * SGU_GROUP_W, (g + 1) * SGU_GROUP_W)
            s = jnp.dot(ws_ref[g], vn[:, cols], preferred_element_type=F32) + bs_ref[:, g:g + 1]
            o_ref[rows, cols] = (u[:, cols] * s).astype(o_ref.dtype)


def _sgu(z3, norm_g, w_s, b_s_t, *, ts):
    b, s, _ = z3.shape
    ts = _tile(s, ts)
    return pl.pallas_call(
        functools.partial(_sgu_kernel, n_chunks=ts // CHUNK),
        grid=(b, s // ts),
        in_specs=[
            pl.BlockSpec((None, ts, SGU_WIDTH), lambda bi, i: (bi, i, OFF_A_U // SGU_WIDTH)),
            pl.BlockSpec((None, ts, SGU_WIDTH), lambda bi, i: (bi, i, OFF_A_V // SGU_WIDTH)),
            pl.BlockSpec((1, SGU_WIDTH), lambda bi, i: (0, 0)),
            pl.BlockSpec((SGU_GROUPS, CHUNK, CHUNK), lambda bi, i: (0, 0, 0)),
            pl.BlockSpec((CHUNK, SGU_GROUPS), lambda bi, i: (0, 0)),
        ],
        out_specs=pl.BlockSpec((None, ts, SGU_WIDTH), lambda bi, i: (bi, i, 0)),
        out_shape=jax.ShapeDtypeStruct((b, s, SGU_WIDTH), BF16),
        compiler_params=_params("parallel", "parallel"),
        name="sgu",
    )(z3, z3, norm_g.reshape(1, SGU_WIDTH), w_s, b_s_t)


def _rope(x, cos, sin_signed):
    return x * cos + pltpu.roll(x, HEAD_DIM // 2, 1) * sin_signed


def _prep_kernel(bq_ref, bk_ref, bv_ref, cq_ref, ck_ref, cos_ref, sin_ref, qg_ref, kg_ref,
                 oq_ref, ok_ref, ov_ref, ocq_ref, ock_ref):
    cos = cos_ref[...]
    sin = sin_ref[...]
    scale = HEAD_DIM ** -0.5
    for h in range(ATTN_HEADS):
        cols = slice(h * HEAD_DIM, (h + 1) * HEAD_DIM)
        x = _rope(_rms(bq_ref[:, cols].astype(F32), qg_ref[...]), cos, sin) * (scale * LOG2_E)
        oq_ref[:, cols] = x.astype(BF16)
    for h in range(ATTN_KV_HEADS):
        cols = slice(h * HEAD_DIM, (h + 1) * HEAD_DIM)
        x = _rope(_rms(bk_ref[:, cols].astype(F32), kg_ref[...]), cos, sin)
        ok_ref[:, cols] = x.astype(BF16)
        ov_ref[:, 2 * h * HEAD_DIM:(2 * h + 1) * HEAD_DIM] = bv_ref[:, cols]
        ov_ref[:, (2 * h + 1) * HEAD_DIM:(2 * h + 2) * HEAD_DIM] = jnp.ones((bv_ref.shape[0], HEAD_DIM), BF16)
    for h in range(RET_HEADS):
        cols = slice(h * HEAD_DIM, (h + 1) * HEAD_DIM)
        ocq_ref[:, cols] = (_rope(cq_ref[:, cols].astype(F32), cos, sin) * scale).astype(BF16)
        ock_ref[:, cols] = _rope(ck_ref[:, cols].astype(F32), cos, sin).astype(BF16)


def _prep(z3, cos_t, sin_t, q_norm_g, k_norm_g, *, ts):
    b, s, _ = z3.shape
    ts = _tile(s, ts)
    n = s // ts

    def zspec(width, off):
        return pl.BlockSpec((None, ts, width), lambda bi, i: (bi, i, off // width))

    def ospec(width):
        return pl.BlockSpec((None, ts, width), lambda bi, i: (bi, i, 0))

    tab = pl.BlockSpec((ts, HEAD_DIM), lambda bi, i: (i, 0))
    gain = pl.BlockSpec((1, HEAD_DIM), lambda bi, i: (0, 0))
    widths = (ATTN_Q_W, ATTN_KV_W, 2 * ATTN_KV_W, RET_QK_W, RET_QK_W)
    return pl.pallas_call(
        _prep_kernel,
        grid=(b, n),
        in_specs=[zspec(ATTN_Q_W, OFF_B_Q), zspec(ATTN_KV_W, OFF_B_K), zspec(ATTN_KV_W, OFF_B_V),
                  zspec(RET_QK_W, OFF_C_Q), zspec(RET_QK_W, OFF_C_K), tab, tab, gain, gain],
        out_specs=[ospec(w) for w in widths],
        out_shape=[jax.ShapeDtypeStruct((b, s, w), BF16) for w in widths],
        compiler_params=_params("parallel", "parallel"),
        name="qk_prep",
    )(z3, z3, z3, z3, z3, cos_t, sin_t, q_norm_g.reshape(1, HEAD_DIM), k_norm_g.reshape(1, HEAD_DIM))


def _flash_kernel(b2_ref, q_ref, k_ref, v_ref, o_ref, qc_ref, acc_ref, m_ref, *, tq, tk):
    for r in range(ATTN_REP):
        qc_ref[r * tq:(r + 1) * tq, :] = q_ref[:, r * HEAD_DIM:(r + 1) * HEAD_DIM]
    b2 = b2_ref[pl.program_id(0)]
    n_kv = k_ref.shape[0] // tk
    acc_ref[...] = jnp.zeros(acc_ref.shape, F32)

    def tiles(j):
        off = pl.multiple_of(j * tk, tk)
        k = k_ref[pl.ds(off, tk), :]
        return lax.dot_general(qc_ref[...], k, _NT, preferred_element_type=F32), v_ref[pl.ds(off, tk), :]

    @pl.when(b2 <= FIXED_SHIFT_LIMIT)
    def _():
        def body(j, carry):
            s, v = tiles(j)
            acc_ref[...] += jnp.dot(jnp.exp2(s - b2).astype(BF16), v, preferred_element_type=F32)
            return carry

        lax.fori_loop(0, n_kv, body, 0, unroll=8 if n_kv % 8 == 0 else 1)

    @pl.when(b2 > FIXED_SHIFT_LIMIT)
    def _():
        m_ref[...] = jnp.full(m_ref.shape, -jnp.inf, F32)

        def body(j, carry):
            s, v = tiles(j)
            m_old = m_ref[...]
            m_new = jnp.maximum(m_old, jnp.max(s, axis=-1, keepdims=True))
            p = jnp.exp2(s - m_new).astype(BF16)
            acc_ref[...] = jnp.exp2(m_old - m_new) * acc_ref[...] + jnp.dot(p, v, preferred_element_type=F32)
            m_ref[...] = m_new
            return carry

        lax.fori_loop(0, n_kv, body, 0)

    out = acc_ref[:, :HEAD_DIM] / acc_ref[:, HEAD_DIM:]
    for r in range(ATTN_REP):
        o_ref[:, r * HEAD_DIM:(r + 1) * HEAD_DIM] = out[r * tq:(r + 1) * tq, :].astype(o_ref.dtype)


def _flash(qr, kr, vaug, bound2, *, tq, tk):
    b, s, _ = qr.shape
    tq, tk = _tile(s, tq), _tile(s, tk)
    gw = ATTN_REP * HEAD_DIM
    grid_spec = pltpu.PrefetchScalarGridSpec(
        num_scalar_prefetch=1,
        grid=(b, ATTN_KV_HEADS, s // tq),
        in_specs=[
            pl.BlockSpec((None, tq, gw), lambda bi, g, i, b2: (bi, i, g)),
            pl.BlockSpec((None, s, HEAD_DIM), lambda bi, g, i, b2: (bi, 0, g)),
            pl.BlockSpec((None, s, 2 * HEAD_DIM), lambda bi, g, i, b2: (bi, 0, g)),
        ],
        out_specs=pl.BlockSpec((None, tq, gw), lambda bi, g, i, b2: (bi, i, g)),
        scratch_shapes=[
            pltpu.VMEM((ATTN_REP * tq, HEAD_DIM), BF16),
            pltpu.VMEM((ATTN_REP * tq, 2 * HEAD_DIM), F32),
            pltpu.VMEM((ATTN_REP * tq, 1), F32),
        ],
    )
    return pl.pallas_call(
        functools.partial(_flash_kernel, tq=tq, tk=tk),
        grid_spec=grid_spec,
        out_shape=jax.ShapeDtypeStruct((b, s, ATTN_Q_W), BF16),
        compiler_params=_params("parallel", "parallel", "arbitrary"),
        name="flash_attention",
    )(bound2, qr, kr, vaug)


def _score_bound(q_norm_g, k_norm_g, batch):
    bound = LOG2_E * HEAD_DIM ** 0.5 * jnp.max(jnp.abs(q_norm_g)) * jnp.max(jnp.abs(k_norm_g)) * 1.01
    return jnp.full((batch,), bound, F32)


def _ret_head_views(refs, rows, h, width):
    half = RET_HEADS // 2
    return refs[h // half][rows, (h % half) * width:(h % half + 1) * width]


def _ret_head(chunks, h, q_ref, k_ref, v_refs, st, dm, qd, kd, cd):
    qk = slice(h * HEAD_DIM, (h + 1) * HEAD_DIM)
    intra, outer = {}, {}
    for c in chunks:
        rows = slice(c * CHUNK, (c + 1) * CHUNK)
        q, k, v = q_ref[rows, qk], k_ref[rows, qk], _ret_head_views(v_refs, rows, h, RET_V_DIM)
        s = lax.dot_general(q, k, _NT, preferred_element_type=F32) * dm
        intra[c] = jnp.dot(s.astype(BF16), v, preferred_element_type=F32)
        kt = (k.astype(F32).T * kd).astype(BF16)
        outer[c] = jnp.dot(kt, v, preferred_element_type=F32)
    out = {}
    for c in chunks:
        rows = slice(c * CHUNK, (c + 1) * CHUNK)
        out[c] = intra[c] + qd * jnp.dot(q_ref[rows, qk], st.astype(BF16), preferred_element_type=F32)
        st = cd * st + outer[c]
    return out, st


def _ret_fwd_kernel(q_ref, k_ref, v0_ref, v1_ref, dm_ref, qd_ref, kd_ref, cd_ref, o_ref, st_ref, *, n_chunks):
    @pl.when(pl.program_id(1) == 0)
    def _():
        st_ref[...] = jnp.zeros(st_ref.shape, F32)

    chunks = list(range(n_chunks))
    for h in range(RET_HEADS):
        out, st = _ret_head(chunks, h, q_ref, k_ref, (v0_ref, v1_ref), st_ref[h],
                            dm_ref[h], qd_ref[h], kd_ref[h], cd_ref[h])
        st_ref[h] = st
        for c in chunks:
            o_ref[c * CHUNK:(c + 1) * CHUNK, h * RET_V_DIM:(h + 1) * RET_V_DIM] = out[c]


def _ret_bwd_kernel(q_ref, k_ref, v0_ref, v1_ref, g0_ref, g1_ref, of_ref, dm_ref, qd_ref, kd_ref, cd_ref, ng_ref,
                    o_ref, st_ref, *, n_chunks):
    @pl.when(pl.program_id(1) == 0)
    def _():
        st_ref[...] = jnp.zeros(st_ref.shape, F32)

    chunks = list(reversed(range(n_chunks)))
    for h in range(RET_HEADS):
        vc = slice(h * RET_V_DIM, (h + 1) * RET_V_DIM)
        out, st = _ret_head(chunks, h, q_ref, k_ref, (v0_ref, v1_ref), st_ref[h],
                            dm_ref[h], qd_ref[h], kd_ref[h], cd_ref[h])
        st_ref[h] = st
        for c in chunks:
            rows = slice(c * CHUNK, (c + 1) * CHUNK)
            o = out[c] + of_ref[rows, vc]
            oc = o - jnp.mean(o, axis=-1, keepdims=True)
            o = oc * lax.rsqrt(jnp.mean(oc * oc, axis=-1, keepdims=True) + NORM_EPS) * ng_ref[:, vc]
            gate = _ret_head_views((g0_ref, g1_ref), rows, h, RET_V_DIM).astype(F32)
            o_ref[rows, vc] = (o * (gate * jax.nn.sigmoid(gate))).astype(o_ref.dtype)


def _retention(cqr, ckr, z3, tabs, norm_g, *, ts):
    b, s, _ = cqr.shape
    ts = _tile(s, ts)
    n = s // ts
    dv = RET_V_DIM
    half_w = RET_V_W // 2

    def seq(width, off, rev, part=0):
        if rev:
            return pl.BlockSpec((None, ts, width), lambda bi, i: (bi, n - 1 - i, off // width + part))
        return pl.BlockSpec((None, ts, width), lambda bi, i: (bi, i, off // width + part))

    def whole(*shape):
        return pl.BlockSpec(shape, lambda bi, i: (0,) * len(shape))

    tab_specs = [whole(RET_HEADS, CHUNK, CHUNK), whole(RET_HEADS, CHUNK, 1), whole(RET_HEADS, 1, CHUNK),
                 whole(RET_HEADS, 1, dv)]
    state = [pltpu.VMEM((RET_HEADS, HEAD_DIM, dv), F32)]
    sem = _params("parallel", "arbitrary")
    o_fwd = pl.pallas_call(
        functools.partial(_ret_fwd_kernel, n_chunks=ts // CHUNK),
        grid=(b, n),
        in_specs=[seq(RET_QK_W, 0, False), seq(RET_QK_W, 0, False), seq(half_w, OFF_C_V, False, 0),
                  seq(half_w, OFF_C_V, False, 1)] + tab_specs,
        out_specs=seq(RET_V_W, 0, False),
        out_shape=jax.ShapeDtypeStruct((b, s, RET_V_W), F32),
        scratch_shapes=state,
        compiler_params=sem,
        name="retention_fwd",
    )(cqr, ckr, z3, z3, *tabs[0])
    return pl.pallas_call(
        functools.partial(_ret_bwd_kernel, n_chunks=ts // CHUNK),
        grid=(b, n),
        in_specs=[seq(RET_QK_W, 0, True), seq(RET_QK_W, 0, True), seq(half_w, OFF_C_V, True, 0),
                  seq(half_w, OFF_C_V, True, 1), seq(half_w, OFF_C_G, True, 0), seq(half_w, OFF_C_G, True, 1),
                  seq(RET_V_W, 0, True)] + tab_specs + [whole(1, RET_V_W)],
        out_specs=seq(RET_V_W, 0, True),
        out_shape=jax.ShapeDtypeStruct((b, s, RET_V_W), BF16),
        scratch_shapes=state,
        compiler_params=sem,
        name="retention_bwd",
    )(cqr, ckr, z3, z3, z3, z3, o_fwd, *tabs[1], norm_g.reshape(1, RET_V_W))


def _retention_tables(ret_decay):
    lg = -jnp.exp(ret_decay.astype(F32))
    idx = jnp.arange(CHUNK, dtype=F32)
    diff = idx[:, None] - idx[None, :]
    out = []
    for d, (mask, delta, qpow, kpow) in enumerate((
            (diff >= 0, diff, idx + 1.0, CHUNK - 1.0 - idx),
            (diff < 0, -diff, CHUNK - idx, idx))):
        l = lg[d][:, None, None]
        dm = jnp.where(mask, jnp.exp(l * jnp.where(mask, delta, 0.0)), 0.0)
        qd = jnp.exp(lg[d][:, None] * qpow)[:, :, None]
        kd = jnp.exp(lg[d][:, None] * kpow)[:, None, :]
        cd = jnp.broadcast_to(jnp.exp(lg[d] * CHUNK)[:, None, None], (RET_HEADS, 1, RET_V_DIM))
        out.append((dm, qd, kd, cd))
    return out


def _merge_kernel(a_ref, b_ref, c_ref, wa_ref, wb_ref, wc_ref, ga_ref, gb_ref, gc_ref, o_ref):
    acc = jax.nn.sigmoid(ga_ref[...].astype(F32)) * jnp.dot(a_ref[...], wa_ref[...], preferred_element_type=F32)
    acc = acc + jax.nn.sigmoid(gb_ref[...].astype(F32)) * jnp.dot(b_ref[...], wb_ref[...], preferred_element_type=F32)
    acc = acc + jax.nn.sigmoid(gc_ref[...].astype(F32)) * jnp.dot(c_ref[...], wc_ref[...], preferred_element_type=F32)
    o_ref[...] = acc.astype(o_ref.dtype)


def _merge(out_a, out_b, out_c, w_a, w_b, w_c, z, *, tm, tn):
    t, k = out_a.shape
    tm, tn = _tile(t, tm), _tile(D_MODEL, tn)
    lhs = pl.BlockSpec((tm, k), lambda i, j: (i, 0))
    rhs = pl.BlockSpec((k, tn), lambda i, j: (0, j))

    def gate(off):
        return pl.BlockSpec((tm, tn), lambda i, j: (i, off // tn + j))

    return pl.pallas_call(
        _merge_kernel,
        grid=(t // tm, D_MODEL // tn),
        in_specs=[lhs, lhs, lhs, rhs, rhs, rhs, gate(OFF_GATE_A), gate(OFF_GATE_B), gate(OFF_GATE_C)],
        out_specs=pl.BlockSpec((tm, tn), lambda i, j: (i, j)),
        out_shape=jax.ShapeDtypeStruct((t, D_MODEL), BF16),
        compiler_params=_params("parallel", "parallel"),
        name="branch_merge",
    )(out_a, out_b, out_c, w_a, w_b, w_c, z, z, z)


def _matmul_res_kernel(a_ref, w_ref, x_ref, o_ref):
    o_ref[...] = x_ref[...] + jnp.dot(a_ref[...], w_ref[...], preferred_element_type=F32)


def _matmul_res(a, w, x, *, tm, tn):
    t, k = a.shape
    n = w.shape[1]
    tm, tn = _tile(t, tm), _tile(n, tn)
    return pl.pallas_call(
        _matmul_res_kernel,
        grid=(t // tm, n // tn),
        in_specs=[
            pl.BlockSpec((tm, k), lambda i, j: (i, 0)),
            pl.BlockSpec((k, tn), lambda i, j: (0, j)),
            pl.BlockSpec((tm, tn), lambda i, j: (i, j)),
        ],
        out_specs=pl.BlockSpec((tm, tn), lambda i, j: (i, j)),
        out_shape=jax.ShapeDtypeStruct((t, n), F32),
        compiler_params=_params("parallel", "parallel"),
        name="out_proj_residual",
    )(a, w, x)


def _route(h, wr, br):
    lt = lax.dot_general(wr, h, _NT, precision=lax.Precision.HIGHEST, preferred_element_type=F32) + br
    gl = [lt[i:i + 1, :] for i in range(N_GROUPS)]
    gmax = jnp.maximum(jnp.maximum(gl[0], gl[1]), jnp.maximum(gl[2], gl[3]))
    denom = sum(jnp.exp(v - gmax) for v in gl)
    g_prob = 1.0 / denom
    g_idx = jnp.where(gl[0] == gmax, 0, jnp.where(gl[1] == gmax, 1, jnp.where(gl[2] == gmax, 2, 3)))
    sel = []
    for j in range(EXPERTS_PER_GROUP):
        rows = [lt[N_GROUPS + g * EXPERTS_PER_GROUP + j:N_GROUPS + g * EXPERTS_PER_GROUP + j + 1, :]
                for g in range(N_GROUPS)]
        sel.append(jnp.where(g_idx == 0, rows[0], jnp.where(g_idx == 1, rows[1],
                                                            jnp.where(g_idx == 2, rows[2], rows[3]))))
    top1 = jnp.maximum(jnp.maximum(sel[0], sel[1]), jnp.maximum(sel[2], sel[3]))
    i1 = jnp.where(sel[0] == top1, 0, jnp.where(sel[1] == top1, 1, jnp.where(sel[2] == top1, 2, 3)))
    rest = [jnp.where(i1 == j, -jnp.inf, sel[j]) for j in range(EXPERTS_PER_GROUP)]
    top2 = jnp.maximum(jnp.maximum(rest[0], rest[1]), jnp.maximum(rest[2], rest[3]))
    i2 = jnp.where(rest[0] == top2, 0, jnp.where(rest[1] == top2, 1, jnp.where(rest[2] == top2, 2, 3)))
    e2 = jnp.exp(top2 - top1)
    w1 = g_prob / (1.0 + e2)
    w2 = g_prob * e2 / (1.0 + e2)
    row = lax.broadcasted_iota(jnp.int32, (ROUTE_ROWS, h.shape[0]), 0)
    return (jnp.where(row == i1, w1, 0.0) + jnp.where(row == i2, w2, 0.0)
            + jnp.where(row == EXPERTS_PER_GROUP, g_idx.astype(F32), 0.0))


def _xattn_kernel(x_ref, g_ref, wq_ref, kv_ref, wo_ref, fg_ref, wr_ref, br_ref, o_ref, rt_ref):
    x = x_ref[...]
    h = _rms(x, g_ref[...]).astype(BF16)
    q = (jnp.dot(h, wq_ref[...], preferred_element_type=F32) * (HEAD_DIM ** -0.5)).astype(BF16)
    heads = []
    for hh in range(XATTN_HEADS):
        cols = slice(hh * HEAD_DIM, (hh + 1) * HEAD_DIM)
        k = kv_ref[:, cols]
        v = kv_ref[:, XATTN_W + hh * HEAD_DIM:XATTN_W + (hh + 1) * HEAD_DIM]
        s = lax.dot_general(q[:, cols], k, _NT, preferred_element_type=F32)
        p = jnp.exp(s - jnp.max(s, axis=-1, keepdims=True))
        o = jnp.dot(p.astype(BF16), v, preferred_element_type=F32) / jnp.sum(p, axis=-1, keepdims=True)
        heads.append(o.astype(BF16))
    y = x + jnp.dot(jnp.concatenate(heads, axis=1), wo_ref[...], preferred_element_type=F32)
    o_ref[...] = y
    rt_ref[...] = _route(_rms(y, fg_ref[...]), wr_ref[...], br_ref[...])


def _xattn(x3, norm_g, wq, kv3, wo, ffn_norm_g, wr_t, br, *, tm):
    b, s, d = x3.shape
    m = kv3.shape[1]
    tm = _tile(s, tm)
    n = s // tm

    def whole(*shape):
        return pl.BlockSpec(shape, lambda bi, i: (0,) * len(shape))

    return pl.pallas_call(
        _xattn_kernel,
        grid=(b, n),
        in_specs=[
            pl.BlockSpec((None, tm, d), lambda bi, i: (bi, i, 0)),
            whole(1, d),
            whole(d, XATTN_W),
            pl.BlockSpec((None, m, 2 * XATTN_W), lambda bi, i: (bi, 0, 0)),
            whole(XATTN_W, d),
            whole(1, d),
            whole(ROUTER_ROWS, d),
            whole(ROUTER_ROWS, 1),
        ],
        out_specs=[pl.BlockSpec((None, tm, d), lambda bi, i: (bi, i, 0)),
                   pl.BlockSpec((ROUTE_ROWS, tm), lambda bi, i: (0, bi * n + i))],
        out_shape=[jax.ShapeDtypeStruct((b, s, d), F32), jax.ShapeDtypeStruct((ROUTE_ROWS, b * s), F32)],
        compiler_params=_params("parallel", "parallel"),
        name="memory_cross_attention",
    )(x3, norm_g.reshape(1, d), wq, kv3, wo, ffn_norm_g.reshape(1, d), wr_t, br)


def _row_gather_kernel(idx_ref, src_ref, dst_ref, sem, *, rows):
    def start(r, carry):
        pltpu.make_async_copy(src_ref.at[pl.ds(idx_ref[0, r], 1), :], dst_ref.at[pl.ds(r, 1), :], sem).start()
        return carry

    lax.fori_loop(0, rows, start, 0, unroll=8)
    pltpu.make_async_copy(src_ref.at[pl.ds(0, rows), :], dst_ref, sem).wait()


def _row_gather(src, idx, *, rows):
    n = idx.shape[0]
    d = src.shape[1]
    rows = _tile(n, rows)
    return pl.pallas_call(
        functools.partial(_row_gather_kernel, rows=rows),
        grid=(n // rows,),
        in_specs=[
            pl.BlockSpec((None, 1, rows), lambda i: (i, 0, 0), memory_space=pltpu.SMEM),
            pl.BlockSpec(memory_space=pl.ANY),
        ],
        out_specs=pl.BlockSpec((rows, d), lambda i: (i, 0)),
        out_shape=jax.ShapeDtypeStruct((n, d), src.dtype),
        scratch_shapes=[pltpu.SemaphoreType.DMA(())],
        compiler_params=_params("arbitrary"),
        name="row_gather",
    )(idx.reshape(n // rows, 1, rows), src)


def _moe_kernel(tg_ref, nu_ref, x_ref, w_ref, ng_ref, wg_ref, wu_ref, wd_ref, fg_ref, o_ref, h_ref, acc_ref, *,
                final_norm):
    i = pl.program_id(0)
    j = pl.program_id(1)
    used = i < nu_ref[0]

    @pl.when(j == 0)
    def _():
        acc_ref[...] = jnp.zeros(acc_ref.shape, F32)
        h_ref[...] = _rms(x_ref[...], ng_ref[...]).astype(BF16)

    @pl.when(used)
    def _():
        h = h_ref[...]
        a = jnp.dot(h, wg_ref[...], preferred_element_type=F32)
        hid = (a * jax.nn.sigmoid(a)) * jnp.dot(h, wu_ref[...], preferred_element_type=F32)
        w = w_ref[...]
        lane = lax.broadcasted_iota(jnp.int32, w.shape, 1)
        wj = jnp.sum(jnp.where(lane == j, w, 0.0), axis=-1, keepdims=True)
        acc_ref[...] += jnp.dot(hid.astype(BF16), wd_ref[...], preferred_element_type=F32) * wj

    @pl.when(j == pl.num_programs(1) - 1)
    def _():
        y = x_ref[...] + acc_ref[...]
        if final_norm:
            y = _rms(y, fg_ref[...])
        o_ref[...] = y


def _moe_grouped(xp, wp, tile_group, n_used, norm_g, w_gate, w_up, w_down, final_g, *, tm, final_norm):
    p, d = xp.shape

    def expert(i, j, tg, nu):
        return (tg[i] * EXPERTS_PER_GROUP + j, 0, 0)

    grid_spec = pltpu.PrefetchScalarGridSpec(
        num_scalar_prefetch=2,
        grid=(p // tm, EXPERTS_PER_GROUP),
        in_specs=[
            pl.BlockSpec((tm, d), lambda i, j, tg, nu: (i, 0)),
            pl.BlockSpec((tm, EXPERTS_PER_GROUP), lambda i, j, tg, nu: (i, 0)),
            pl.BlockSpec((1, d), lambda i, j, tg, nu: (0, 0)),
            pl.BlockSpec((None, d, EXPERT_FF), expert),
            pl.BlockSpec((None, d, EXPERT_FF), expert),
            pl.BlockSpec((None, EXPERT_FF, d), expert),
            pl.BlockSpec((1, d), lambda i, j, tg, nu: (0, 0)),
        ],
        out_specs=pl.BlockSpec((tm, d), lambda i, j, tg, nu: (i, 0)),
        scratch_shapes=[pltpu.VMEM((tm, d), BF16), pltpu.VMEM((tm, d), F32)],
    )
    return pl.pallas_call(
        functools.partial(_moe_kernel, final_norm=final_norm),
        grid_spec=grid_spec,
        out_shape=jax.ShapeDtypeStruct((p, d), F32),
        compiler_params=_params("parallel", "arbitrary"),
        name="experts",
    )(tile_group, n_used, xp, wp, norm_g.reshape(1, d), w_gate, w_up, w_down, final_g.reshape(1, d))


def _group_plan(route, tm):
    t = route.shape[1]
    gid = route[EXPERTS_PER_GROUP].astype(jnp.int32)
    onehot = (gid[:, None] == jnp.arange(N_GROUPS, dtype=jnp.int32)[None, :]).astype(jnp.int32)
    csum = jnp.cumsum(onehot, axis=0)
    count = csum[-1]
    rank = jnp.take_along_axis(csum, gid[:, None], axis=1)[:, 0] - 1
    padded = (count + tm - 1) // tm * tm
    end = jnp.cumsum(padded)
    pos = (end - padded)[gid] + rank
    n_tiles = t // tm + N_GROUPS
    src = jnp.zeros((n_tiles * tm,), jnp.int32).at[pos].set(jnp.arange(t, dtype=jnp.int32))
    tile_start = jnp.arange(n_tiles, dtype=jnp.int32) * tm
    tile_group = jnp.minimum(jnp.sum((tile_start[:, None] >= end[None, :]).astype(jnp.int32), axis=1), N_GROUPS - 1)
    n_used = (end[-1] // tm).reshape(1)
    return pos, src, tile_group, n_used


def _moe(x, route, norm_g, w_gate, w_up, w_down, final_g, *, tm, final_norm):
    pos, src, tile_group, n_used = _group_plan(route, tm)
    xp = _row_gather(x, src, rows=512)
    wp = route[:EXPERTS_PER_GROUP].T[src]
    yp = _moe_grouped(xp, wp, tile_group, n_used, norm_g, w_gate, w_up, w_down, final_g, tm=tm,
                      final_norm=final_norm)
    return _row_gather(yp, pos, rows=512)


def _rope_tables(seq_len):
    rows = seq_len // GRID_W
    row = jnp.repeat(jnp.arange(rows, dtype=F32), GRID_W)
    col = jnp.tile(jnp.arange(GRID_W, dtype=F32), rows)
    n_freq = HEAD_DIM // 4
    inv = ROPE_THETA ** (-jnp.arange(n_freq, dtype=F32) / n_freq)
    ar = row[:, None] * inv
    ac = col[:, None] * inv
    cos = jnp.concatenate([jnp.cos(ar), jnp.cos(ac), jnp.cos(ar), jnp.cos(ac)], axis=1)
    sin = jnp.concatenate([-jnp.sin(ar), -jnp.sin(ac), jnp.sin(ar), jnp.sin(ac)], axis=1)
    return cos, sin


def _pair_major(a):
    lead = a.shape[:-1]
    heads = a.shape[-1] // HEAD_DIM
    a = a.reshape(lead + (heads, 2, 2, HEAD_DIM // 4))
    return jnp.swapaxes(a, -3, -2).reshape(lead + (heads * HEAD_DIM,))


def _rotary_layout(w_in):
    parts = [w_in[:, :OFF_B_Q], _pair_major(w_in[:, OFF_B_Q:OFF_B_V]), w_in[:, OFF_B_V:OFF_C_Q],
             _pair_major(w_in[:, OFF_C_Q:OFF_C_V]), w_in[:, OFF_C_V:]]
    return jnp.concatenate(parts, axis=1)


def _prepare_layer(l, p):
    wr_t = jnp.concatenate([p["router_group_w"][l].T, p["router_expert_w"][l].T,
                            jnp.zeros((ROUTER_ROWS - N_GROUPS - N_EXPERTS, D_MODEL), F32)], axis=0)
    br = jnp.concatenate([p["router_group_b"][l], p["router_expert_b"][l].reshape(-1),
                          jnp.zeros((ROUTER_ROWS - N_GROUPS - N_EXPERTS,), F32)]).reshape(ROUTER_ROWS, 1)
    return dict(
        norm_mix_g=p["norm_mix_g"][l], w_in=_rotary_layout(p["w_in"][l].astype(BF16)),
        sgu_norm_g=p["sgu_norm_g"][l], sgu_w=p["sgu_w"][l].astype(BF16), sgu_b_t=p["sgu_b"][l].T,
        q_norm_g=_pair_major(p["attn_q_norm_g"][l]), k_norm_g=_pair_major(p["attn_k_norm_g"][l]),
        ret_tabs=_retention_tables(p["ret_decay"][l]), ret_norm_g=p["ret_norm_g"][l],
        w_a=p["w_branch_a"][l].astype(BF16), w_b=p["w_branch_b"][l].astype(BF16),
        w_c=p["w_branch_c"][l].astype(BF16), w_out=p["w_out"][l].astype(BF16),
        xattn_norm_g=p["xattn_norm_g"][l], mem_norm_g=p["mem_norm_g"][l],
        wq=p["xattn_wq"][l].astype(BF16), wkv=p["xattn_wkv"][l].astype(BF16), wo=p["xattn_wo"][l].astype(BF16),
        ffn_norm_g=p["ffn_norm_g"][l], wr_t=wr_t, br=br,
        w_gate=p["expert_w_gate"][l].astype(BF16), w_up=p["expert_w_up"][l].astype(BF16),
        w_down=p["expert_w_down"][l].astype(BF16),
    )


def _trunk(x, mem, layers, final_norm_g):
    b, s, d = x.shape
    t = b * s
    m = mem.shape[1]
    cos_t, sin_t = _rope_tables(s)
    xt = x.reshape(t, d)
    memt = mem.reshape(b * m, d)
    for l, w in enumerate(layers):
        z = _rms_matmul(xt, w["norm_mix_g"], w["w_in"], tm=1024, tn=1280)
        z3 = z.reshape(b, s, IN_WIDTH)
        out_a = _sgu(z3, w["sgu_norm_g"], w["sgu_w"], w["sgu_b_t"], ts=512)
        qr, kr, vaug, cqr, ckr = _prep(z3, cos_t, sin_t, w["q_norm_g"], w["k_norm_g"], ts=512)
        out_b = _flash(qr, kr, vaug, _score_bound(w["q_norm_g"], w["k_norm_g"], b), tq=512, tk=256)
        out_c = _retention(cqr, ckr, z3, w["ret_tabs"], w["ret_norm_g"], ts=512)
        merged = _merge(out_a.reshape(t, SGU_WIDTH), out_b.reshape(t, ATTN_Q_W), out_c.reshape(t, RET_V_W),
                        w["w_a"], w["w_b"], w["w_c"], z, tm=1024, tn=512)
        xt = _matmul_res(merged, w["w_out"], xt, tm=512, tn=2048)
        kv = _rms_matmul(memt, w["mem_norm_g"], w["wkv"], tm=256, tn=512)
        x3, route = _xattn(xt.reshape(b, s, d), w["xattn_norm_g"], w["wq"], kv.reshape(b, m, 2 * XATTN_W), w["wo"],
                           w["ffn_norm_g"], w["wr_t"], w["br"], tm=512)
        xt = x3.reshape(t, d)
        xt = _moe(xt, route, w["ffn_norm_g"], w["w_gate"], w["w_up"], w["w_down"], final_norm_g, tm=512,
                  final_norm=(l == len(layers) - 1))
    return xt.reshape(b, s, d)


def kernel(x_prompt, x_sample, mem_prompt, mem_sample, norm_mix_g, w_in, sgu_norm_g, sgu_w, sgu_b, attn_q_norm_g, attn_k_norm_g, ret_decay, ret_norm_g, w_branch_a, w_branch_b, w_branch_c, w_out, xattn_norm_g, mem_norm_g, xattn_wq, xattn_wkv, xattn_wo, ffn_norm_g, router_group_w, router_group_b, router_expert_w, router_expert_b, expert_w_gate, expert_w_up, expert_w_down, final_norm_g):
    p = dict(norm_mix_g=norm_mix_g, w_in=w_in, sgu_norm_g=sgu_norm_g, sgu_w=sgu_w, sgu_b=sgu_b,
             attn_q_norm_g=attn_q_norm_g, attn_k_norm_g=attn_k_norm_g, ret_decay=ret_decay, ret_norm_g=ret_norm_g,
             w_branch_a=w_branch_a, w_branch_b=w_branch_b, w_branch_c=w_branch_c, w_out=w_out,
             xattn_norm_g=xattn_norm_g, mem_norm_g=mem_norm_g, xattn_wq=xattn_wq, xattn_wkv=xattn_wkv,
             xattn_wo=xattn_wo, ffn_norm_g=ffn_norm_g, router_group_w=router_group_w, router_group_b=router_group_b,
             router_expert_w=router_expert_w, router_expert_b=router_expert_b, expert_w_gate=expert_w_gate,
             expert_w_up=expert_w_up, expert_w_down=expert_w_down)
    layers = [_prepare_layer(l, p) for l in range(DEPTH)]
    y_prompt = _trunk(x_prompt, mem_prompt, layers, final_norm_g)
    y_sample = _trunk(x_sample, mem_sample, layers, final_norm_g)
    return (y_prompt, y_sample)
```

```python
import functools

import jax
import jax.numpy as jnp
import numpy as np
from jax import lax
from jax.experimental import pallas as pl
from jax.experimental.pallas import tpu as pltpu

F32 = jnp.float32
BF16 = jnp.bfloat16

D_MODEL = 2048
DEPTH = 2
GRID_W = 64
ROPE_THETA = 10000.0
CHUNK = 128
HEAD_DIM = 128
NORM_EPS = 1e-6
SGU_WIDTH = 1024
SGU_GROUPS = 4
SGU_GROUP_W = SGU_WIDTH // SGU_GROUPS
ATTN_HEADS = 8
ATTN_KV_HEADS = 2
ATTN_REP = ATTN_HEADS // ATTN_KV_HEADS
ATTN_Q_W = ATTN_HEADS * HEAD_DIM
ATTN_KV_W = ATTN_KV_HEADS * HEAD_DIM
RET_HEADS = 4
RET_V_DIM = 256
RET_QK_W = RET_HEADS * HEAD_DIM
RET_V_W = RET_HEADS * RET_V_DIM
XATTN_HEADS = 4
XATTN_W = XATTN_HEADS * HEAD_DIM
N_GROUPS = 4
EXPERTS_PER_GROUP = 4
N_EXPERTS = N_GROUPS * EXPERTS_PER_GROUP
EXPERT_FF = 512
ROUTER_ROWS = 24
ROUTE_ROWS = 8

OFF_A_U = 0
OFF_A_V = OFF_A_U + SGU_WIDTH
OFF_B_Q = OFF_A_V + SGU_WIDTH
OFF_B_K = OFF_B_Q + ATTN_Q_W
OFF_B_V = OFF_B_K + ATTN_KV_W
OFF_C_Q = OFF_B_V + ATTN_KV_W
OFF_C_K = OFF_C_Q + RET_QK_W
OFF_C_V = OFF_C_K + RET_QK_W
OFF_C_G = OFF_C_V + RET_V_W
OFF_GATE_A = OFF_C_G + RET_V_W
OFF_GATE_B = OFF_GATE_A + D_MODEL
OFF_GATE_C = OFF_GATE_B + D_MODEL
IN_WIDTH = OFF_GATE_C + D_MODEL

V7X_VMEM_LIMIT_BYTES = 56 * 1024 * 1024

_NT = (((1,), (1,)), ((), ()))
LOG2_E = 1.4426950408889634
FIXED_SHIFT_LIMIT = 60.0


def _tile(n, pref):
    t = min(pref, n)
    while n % t:
        t //= 2
    return t


def _params(*sem):
    return pltpu.CompilerParams(dimension_semantics=sem, vmem_limit_bytes=V7X_VMEM_LIMIT_BYTES)


def _rms(x, g):
    return x * lax.rsqrt(jnp.mean(x * x, axis=-1, keepdims=True) + NORM_EPS) * g


def _rms_matmul_kernel(x_ref, g_ref, w_ref, o_ref, h_ref):
    @pl.when(pl.program_id(1) == 0)
    def _():
        h_ref[...] = _rms(x_ref[...], g_ref[...]).astype(BF16)

    o_ref[...] = jnp.dot(h_ref[...], w_ref[...], preferred_element_type=F32).astype(o_ref.dtype)


def _rms_matmul(x, g, w, *, tm, tn):
    t, d = x.shape
    n = w.shape[1]
    tm, tn = _tile(t, tm), _tile(n, tn)
    return pl.pallas_call(
        _rms_matmul_kernel,
        grid=(t // tm, n // tn),
        in_specs=[
            pl.BlockSpec((tm, d), lambda i, j: (i, 0)),
            pl.BlockSpec((1, d), lambda i, j: (0, 0)),
            pl.BlockSpec((d, tn), lambda i, j: (0, j)),
        ],
        out_specs=pl.BlockSpec((tm, tn), lambda i, j: (i, j)),
        out_shape=jax.ShapeDtypeStruct((t, n), BF16),
        scratch_shapes=[pltpu.VMEM((tm, d), BF16)],
        compiler_params=_params("parallel", "arbitrary"),
        name="rms_matmul",
    )(x, g.reshape(1, d), w)


def _gelu(x):
    return 0.5 * x * (1.0 + lax.erf(x * (2.0 ** -0.5)))


def _sgu_kernel(u_ref, v_ref, ng_ref, ws_ref, bs_ref, o_ref, *, n_chunks):
    for c in range(n_chunks):
        rows = slice(c * CHUNK, (c + 1) * CHUNK)
        u = _gelu(u_ref[rows, :].astype(F32))
        v = _gelu(v_ref[rows, :].astype(F32))
        vc = v - jnp.mean(v, axis=-1, keepdims=True)
        vn = vc * lax.rsqrt(jnp.mean(vc * vc, axis=-1, keepdims=True) + NORM_EPS) * ng_ref[...]
        vn = vn.astype(BF16)
        for g in range(SGU_GROUPS):
            cols = slice(g * SGU_GROUP_W, (g + 1) * SGU_GROUP_W)
            s = jnp.dot(ws_ref[g], vn[:, cols], preferred_element_type=F32) + bs_ref[:, g:g + 1]
            o_ref[rows, cols] = (u[:, cols] * s).astype(o_ref.dtype)


def _sgu(z3, norm_g, w_s, b_s_t, *, ts):
    b, s, _ = z3.shape
    ts = _tile(s, ts)
    return pl.pallas_call(
        functools.partial(_sgu_kernel, n_chunks=ts // CHUNK),
        grid=(b, s // ts),
        in_specs=[
            pl.BlockSpec((None, ts, SGU_WIDTH), lambda bi, i: (bi, i, OFF_A_U // SGU_WIDTH)),
            pl.BlockSpec((None, ts, SGU_WIDTH), lambda bi, i: (bi, i, OFF_A_V // SGU_WIDTH)),
            pl.BlockSpec((1, SGU_WIDTH), lambda bi, i: (0, 0)),
            pl.BlockSpec((SGU_GROUPS, CHUNK, CHUNK), lambda bi, i: (0, 0, 0)),
            pl.BlockSpec((CHUNK, SGU_GROUPS), lambda bi, i: (0, 0)),
        ],
        out_specs=pl.BlockSpec((None, ts, SGU_WIDTH), lambda bi, i: (bi, i, 0)),
        out_shape=jax.ShapeDtypeStruct((b, s, SGU_WIDTH), BF16),
        compiler_params=_params("parallel", "parallel"),
        name="sgu",
    )(z3, z3, norm_g.reshape(1, SGU_WIDTH), w_s, b_s_t)


def _rope(x, cos, sin_signed):
    return x * cos + pltpu.roll(x, HEAD_DIM // 2, 1) * sin_signed


def _prep_kernel(bq_ref, bk_ref, bv_ref, cq_ref, ck_ref, cos_ref, sin_ref, qg_ref, kg_ref,
                 oq_ref, ok_ref, ov_ref, ocq_ref, ock_ref):
    cos = cos_ref[...]
    sin = sin_ref[...]
    scale = HEAD_DIM ** -0.5
    for h in range(ATTN_HEADS):
        cols = slice(h * HEAD_DIM, (h + 1) * HEAD_DIM)
        x = _rope(_rms(bq_ref[:, cols].astype(F32), qg_ref[...]), cos, sin) * (scale * LOG2_E)
        oq_ref[:, cols] = x.astype(BF16)
    for h in range(ATTN_KV_HEADS):
        cols = slice(h * HEAD_DIM, (h + 1) * HEAD_DIM)
        x = _rope(_rms(bk_ref[:, cols].astype(F32), kg_ref[...]), cos, sin)
        ok_ref[:, cols] = x.astype(BF16)
        ov_ref[:, 2 * h * HEAD_DIM:(2 * h + 1) * HEAD_DIM] = bv_ref[:, cols]
        ov_ref[:, (2 * h + 1) * HEAD_DIM:(2 * h + 2) * HEAD_DIM] = jnp.ones((bv_ref.shape[0], HEAD_DIM), BF16)
    for h in range(RET_HEADS):
        cols = slice(h * HEAD_DIM, (h + 1) * HEAD_DIM)
        ocq_ref[:, cols] = (_rope(cq_ref[:, cols].astype(F32), cos, sin) * scale).astype(BF16)
        ock_ref[:, cols] = _rope(ck_ref[:, cols].astype(F32), cos, sin).astype(BF16)


def _prep(z3, cos_t, sin_t, q_norm_g, k_norm_g, *, ts):
    b, s, _ = z3.shape
    ts = _tile(s, ts)
    n = s // ts

    def zspec(width, off):
        return pl.BlockSpec((None, ts, width), lambda bi, i: (bi, i, off // width))

    def ospec(width):
        return pl.BlockSpec((None, ts, width), lambda bi, i: (bi, i, 0))

    tab = pl.BlockSpec((ts, HEAD_DIM), lambda bi, i: (i, 0))
    gain = pl.BlockSpec((1, HEAD_DIM), lambda bi, i: (0, 0))
    widths = (ATTN_Q_W, ATTN_KV_W, 2 * ATTN_KV_W, RET_QK_W, RET_QK_W)
    return pl.pallas_call(
        _prep_kernel,
        grid=(b, n),
        in_specs=[zspec(ATTN_Q_W, OFF_B_Q), zspec(ATTN_KV_W, OFF_B_K), zspec(ATTN_KV_W, OFF_B_V),
                  zspec(RET_QK_W, OFF_C_Q), zspec(RET_QK_W, OFF_C_K), tab, tab, gain, gain],
        out_specs=[ospec(w) for w in widths],
        out_shape=[jax.ShapeDtypeStruct((b, s, w), BF16) for w in widths],
        compiler_params=_params("parallel", "parallel"),
        name="qk_prep",
    )(z3, z3, z3, z3, z3, cos_t, sin_t, q_norm_g.reshape(1, HEAD_DIM), k_norm_g.reshape(1, HEAD_DIM))


def _flash_kernel(b2_ref, q_ref, k_ref, v_ref, o_ref, qc_ref, acc_ref, m_ref, *, tq, tk):
    for r in range(ATTN_REP):
        qc_ref[r * tq:(r + 1) * tq, :] = q_ref[:, r * HEAD_DIM:(r + 1) * HEAD_DIM]
    b2 = b2_ref[pl.program_id(0)]
    n_kv = k_ref.shape[0] // tk
    acc_ref[...] = jnp.zeros(acc_ref.shape, F32)

    def tiles(j):
        off = pl.multiple_of(j * tk, tk)
        k = k_ref[pl.ds(off, tk), :]
        return lax.dot_general(qc_ref[...], k, _NT, preferred_element_type=F32), v_ref[pl.ds(off, tk), :]

    @pl.when(b2 <= FIXED_SHIFT_LIMIT)
    def _():
        def body(j, carry):
            s, v = tiles(j)
            acc_ref[...] += jnp.dot(jnp.exp2(s - b2).astype(BF16), v, preferred_element_type=F32)
            return carry

        lax.fori_loop(0, n_kv, body, 0, unroll=8 if n_kv % 8 == 0 else 1)

    @pl.when(b2 > FIXED_SHIFT_LIMIT)
    def _():
        m_ref[...] = jnp.full(m_ref.shape, -jnp.inf, F32)

        def body(j, carry):
            s, v = tiles(j)
            m_old = m_ref[...]
            m_new = jnp.maximum(m_old, jnp.max(s, axis=-1, keepdims=True))
            p = jnp.exp2(s - m_new).astype(BF16)
            acc_ref[...] = jnp.exp2(m_old - m_new) * acc_ref[...] + jnp.dot(p, v, preferred_element_type=F32)
            m_ref[...] = m_new
            return carry

        lax.fori_loop(0, n_kv, body, 0)

    out = acc_ref[:, :HEAD_DIM] / acc_ref[:, HEAD_DIM:]
    for r in range(ATTN_REP):
        o_ref[:, r * HEAD_DIM:(r + 1) * HEAD_DIM] = out[r * tq:(r + 1) * tq, :].astype(o_ref.dtype)


def _flash(qr, kr, vaug, bound2, *, tq, tk):
    b, s, _ = qr.shape
    tq, tk = _tile(s, tq), _tile(s, tk)
    gw = ATTN_REP * HEAD_DIM
    grid_spec = pltpu.PrefetchScalarGridSpec(
        num_scalar_prefetch=1,
        grid=(b, ATTN_KV_HEADS, s // tq),
        in_specs=[
            pl.BlockSpec((None, tq, gw), lambda bi, g, i, b2: (bi, i, g)),
            pl.BlockSpec((None, s, HEAD_DIM), lambda bi, g, i, b2: (bi, 0, g)),
            pl.BlockSpec((None, s, 2 * HEAD_DIM), lambda bi, g, i, b2: (bi, 0, g)),
        ],
        out_specs=pl.BlockSpec((None, tq, gw), lambda bi, g, i, b2: (bi, i, g)),
        scratch_shapes=[
            pltpu.VMEM((ATTN_REP * tq, HEAD_DIM), BF16),
            pltpu.VMEM((ATTN_REP * tq, 2 * HEAD_DIM), F32),
            pltpu.VMEM((ATTN_REP * tq, 1), F32),
        ],
    )
    return pl.pallas_call(
        functools.partial(_flash_kernel, tq=tq, tk=tk),
        grid_spec=grid_spec,
        out_shape=jax.ShapeDtypeStruct((b, s, ATTN_Q_W), BF16),
        compiler_params=_params("parallel", "parallel", "arbitrary"),
        name="flash_attention",
    )(bound2, qr, kr, vaug)


def _score_bound(q_norm_g, k_norm_g, batch):
    bound = LOG2_E * HEAD_DIM ** 0.5 * jnp.max(jnp.abs(q_norm_g)) * jnp.max(jnp.abs(k_norm_g)) * 1.01
    return jnp.full((batch,), bound, F32)


def _ret_head_views(refs, rows, h, width):
    half = RET_HEADS // 2
    return refs[h // half][rows, (h % half) * width:(h % half + 1) * width]


def _ret_head(chunks, h, q_ref, k_ref, v_refs, st, dm, qd, kd, cd):
    qk = slice(h * HEAD_DIM, (h + 1) * HEAD_DIM)
    intra, outer = {}, {}
    for c in chunks:
        rows = slice(c * CHUNK, (c + 1) * CHUNK)
        q, k, v = q_ref[rows, qk], k_ref[rows, qk], _ret_head_views(v_refs, rows, h, RET_V_DIM)
        s = lax.dot_general(q, k, _NT, preferred_element_type=F32) * dm
        intra[c] = jnp.dot(s.astype(BF16), v, preferred_element_type=F32)
        kt = (k.astype(F32).T * kd).astype(BF16)
        outer[c] = jnp.dot(kt, v, preferred_element_type=F32)
    out = {}
    for c in chunks:
        rows = slice(c * CHUNK, (c + 1) * CHUNK)
        out[c] = intra[c] + qd * jnp.dot(q_ref[rows, qk], st.astype(BF16), preferred_element_type=F32)
        st = cd * st + outer[c]
    return out, st


def _ret_fwd_kernel(q_ref, k_ref, v0_ref, v1_ref, dm_ref, qd_ref, kd_ref, cd_ref, o_ref, st_ref, *, n_chunks):
    @pl.when(pl.program_id(1) == 0)
    def _():
        st_ref[...] = jnp.zeros(st_ref.shape, F32)

    chunks = list(range(n_chunks))
    for h in range(RET_HEADS):
        out, st = _ret_head(chunks, h, q_ref, k_ref, (v0_ref, v1_ref), st_ref[h],
                            dm_ref[h], qd_ref[h], kd_ref[h], cd_ref[h])
        st_ref[h] = st
        for c in chunks:
            o_ref[c * CHUNK:(c + 1) * CHUNK, h * RET_V_DIM:(h + 1) * RET_V_DIM] = out[c]


def _ret_bwd_kernel(q_ref, k_ref, v0_ref, v1_ref, g0_ref, g1_ref, of_ref, dm_ref, qd_ref, kd_ref, cd_ref, ng_ref,
                    o_ref, st_ref, *, n_chunks):
    @pl.when(pl.program_id(1) == 0)
    def _():
        st_ref[...] = jnp.zeros(st_ref.shape, F32)

    chunks = list(reversed(range(n_chunks)))
    for h in range(RET_HEADS):
        vc = slice(h * RET_V_DIM, (h + 1) * RET_V_DIM)
        out, st = _ret_head(chunks, h, q_ref, k_ref, (v0_ref, v1_ref), st_ref[h],
                            dm_ref[h], qd_ref[h], kd_ref[h], cd_ref[h])
        st_ref[h] = st
        for c in chunks:
            rows = slice(c * CHUNK, (c + 1) * CHUNK)
            o = out[c] + of_ref[rows, vc]
            oc = o - jnp.mean(o, axis=-1, keepdims=True)
            o = oc * lax.rsqrt(jnp.mean(oc * oc, axis=-1, keepdims=True) + NORM_EPS) * ng_ref[:, vc]
            gate = _ret_head_views((g0_ref, g1_ref), rows, h, RET_V_DIM).astype(F32)
            o_ref[rows, vc] = (o * (gate * jax.nn.sigmoid(gate))).astype(o_ref.dtype)


def _retention(cqr, ckr, z3, tabs, norm_g, *, ts):
    b, s, _ = cqr.shape
    ts = _tile(s, ts)
    n = s // ts
    dv = RET_V_DIM
    half_w = RET_V_W // 2

    def seq(width, off, rev, part=0):
        if rev:
            return pl.BlockSpec((None, ts, width), lambda bi, i: (bi, n - 1 - i, off // width + part))
        return pl.BlockSpec((None, ts, width), lambda bi, i: (bi, i, off // width + part))

    def whole(*shape):
        return pl.BlockSpec(shape, lambda bi, i: (0,) * len(shape))

    tab_specs = [whole(RET_HEADS, CHUNK, CHUNK), whole(RET_HEADS, CHUNK, 1), whole(RET_HEADS, 1, CHUNK),
                 whole(RET_HEADS, 1, dv)]
    state = [pltpu.VMEM((RET_HEADS, HEAD_DIM, dv), F32)]
    sem = _params("parallel", "arbitrary")
    o_fwd = pl.pallas_call(
        functools.partial(_ret_fwd_kernel, n_chunks=ts // CHUNK),
        grid=(b, n),
        in_specs=[seq(RET_QK_W, 0, False), seq(RET_QK_W, 0, False), seq(half_w, OFF_C_V, False, 0),
                  seq(half_w, OFF_C_V, False, 1)] + tab_specs,
        out_specs=seq(RET_V_W, 0, False),
        out_shape=jax.ShapeDtypeStruct((b, s, RET_V_W), F32),
        scratch_shapes=state,
        compiler_params=sem,
        name="retention_fwd",
    )(cqr, ckr, z3, z3, *tabs[0])
    return pl.pallas_call(
        functools.partial(_ret_bwd_kernel, n_chunks=ts // CHUNK),
        grid=(b, n),
        in_specs=[seq(RET_QK_W, 0, True), seq(RET_QK_W, 0, True), seq(half_w, OFF_C_V, True, 0),
                  seq(half_w, OFF_C_V, True, 1), seq(half_w, OFF_C_G, True, 0), seq(half_w, OFF_C_G, True, 1),
                  seq(RET_V_W, 0, True)] + tab_specs + [whole(1, RET_V_W)],
        out_specs=seq(RET_V_W, 0, True),
        out_shape=jax.ShapeDtypeStruct((b, s, RET_V_W), BF16),
        scratch_shapes=state,
        compiler_params=sem,
        name="retention_bwd",
    )(cqr, ckr, z3, z3, z3, z3, o_fwd, *tabs[1], norm_g.reshape(1, RET_V_W))


def _retention_tables(ret_decay):
    lg = -jnp.exp(ret_decay.astype(F32))
    idx = jnp.arange(CHUNK, dtype=F32)
    diff = idx[:, None] - idx[None, :]
    out = []
    for d, (mask, delta, qpow, kpow) in enumerate((
            (diff >= 0, diff, idx + 1.0, CHUNK - 1.0 - idx),
            (diff < 0, -diff, CHUNK - idx, idx))):
        l = lg[d][:, None, None]
        dm = jnp.where(mask, jnp.exp(l * jnp.where(mask, delta, 0.0)), 0.0)
        qd = jnp.exp(lg[d][:, None] * qpow)[:, :, None]
        kd = jnp.exp(lg[d][:, None] * kpow)[:, None, :]
        cd = jnp.broadcast_to(jnp.exp(lg[d] * CHUNK)[:, None, None], (RET_HEADS, 1, RET_V_DIM))
        out.append((dm, qd, kd, cd))
    return out


def _merge_kernel(a_ref, b_ref, c_ref, wa_ref, wb_ref, wc_ref, ga_ref, gb_ref, gc_ref, o_ref):
    acc = jax.nn.sigmoid(ga_ref[...].astype(F32)) * jnp.dot(a_ref[...], wa_ref[...], preferred_element_type=F32)
    acc = acc + jax.nn.sigmoid(gb_ref[...].astype(F32)) * jnp.dot(b_ref[...], wb_ref[...], preferred_element_type=F32)
    acc = acc + jax.nn.sigmoid(gc_ref[...].astype(F32)) * jnp.dot(c_ref[...], wc_ref[...], preferred_element_type=F32)
    o_ref[...] = acc.astype(o_ref.dtype)


def _merge(out_a, out_b, out_c, w_a, w_b, w_c, z, *, tm, tn):
    t, k = out_a.shape
    tm, tn = _tile(t, tm), _tile(D_MODEL, tn)
    lhs = pl.BlockSpec((tm, k), lambda i, j: (i, 0))
    rhs = pl.BlockSpec((k, tn), lambda i, j: (0, j))

    def gate(off):
        return pl.BlockSpec((tm, tn), lambda i, j: (i, off // tn + j))

    return pl.pallas_call(
        _merge_kernel,
        grid=(t // tm, D_MODEL // tn),
        in_specs=[lhs, lhs, lhs, rhs, rhs, rhs, gate(OFF_GATE_A), gate(OFF_GATE_B), gate(OFF_GATE_C)],
        out_specs=pl.BlockSpec((tm, tn), lambda i, j: (i, j)),
        out_shape=jax.ShapeDtypeStruct((t, D_MODEL), BF16),
        compiler_params=_params("parallel", "parallel"),
        name="branch_merge",
    )(out_a, out_b, out_c, w_a, w_b, w_c, z, z, z)


def _matmul_res_kernel(a_ref, w_ref, x_ref, o_ref):
    o_ref[...] = x_ref[...] + jnp.dot(a_ref[...], w_ref[...], preferred_element_type=F32)


def _matmul_res(a, w, x, *, tm, tn):
    t, k = a.shape
    n = w.shape[1]
    tm, tn = _tile(t, tm), _tile(n, tn)
    return pl.pallas_call(
        _matmul_res_kernel,
        grid=(t // tm, n // tn),
        in_specs=[
            pl.BlockSpec((tm, k), lambda i, j: (i, 0)),
            pl.BlockSpec((k, tn), lambda i, j: (0, j)),
            pl.BlockSpec((tm, tn), lambda i, j: (i, j)),
        ],
        out_specs=pl.BlockSpec((tm, tn), lambda i, j: (i, j)),
        out_shape=jax.ShapeDtypeStruct((t, n), F32),
        compiler_params=_params("parallel", "parallel"),
        name="out_proj_residual",
    )(a, w, x)


def _route(h, wr, br):
    h_hi = h.astype(BF16)
    h_lo = (h - h_hi.astype(F32)).astype(BF16)
    both = lax.dot_general(wr, h_hi, _NT, preferred_element_type=F32)
    lt = (both[:ROUTER_ROWS] + both[ROUTER_ROWS:]
          + lax.dot_general(wr[:ROUTER_ROWS], h_lo, _NT, preferred_element_type=F32) + br)
    gl = [lt[i:i + 1, :] for i in range(N_GROUPS)]
    gmax = jnp.maximum(jnp.maximum(gl[0], gl[1]), jnp.maximum(gl[2], gl[3]))
    denom = sum(jnp.exp(v - gmax) for v in gl)
    g_prob = 1.0 / denom
    g_idx = jnp.where(gl[0] == gmax, 0, jnp.where(gl[1] == gmax, 1, jnp.where(gl[2] == gmax, 2, 3)))
    sel = []
    for j in range(EXPERTS_PER_GROUP):
        rows = [lt[N_GROUPS + g * EXPERTS_PER_GROUP + j:N_GROUPS + g * EXPERTS_PER_GROUP + j + 1, :]
                for g in range(N_GROUPS)]
        sel.append(jnp.where(g_idx == 0, rows[0], jnp.where(g_idx == 1, rows[1],
                                                            jnp.where(g_idx == 2, rows[2], rows[3]))))
    top1 = jnp.maximum(jnp.maximum(sel[0], sel[1]), jnp.maximum(sel[2], sel[3]))
    i1 = jnp.where(sel[0] == top1, 0, jnp.where(sel[1] == top1, 1, jnp.where(sel[2] == top1, 2, 3)))
    rest = [jnp.where(i1 == j, -jnp.inf, sel[j]) for j in range(EXPERTS_PER_GROUP)]
    top2 = jnp.maximum(jnp.maximum(rest[0], rest[1]), jnp.maximum(rest[2], rest[3]))
    i2 = jnp.where(rest[0] == top2, 0, jnp.where(rest[1] == top2, 1, jnp.where(rest[2] == top2, 2, 3)))
    e2 = jnp.exp(top2 - top1)
    w1 = g_prob / (1.0 + e2)
    w2 = g_prob * e2 / (1.0 + e2)
    row = lax.broadcasted_iota(jnp.int32, (ROUTE_ROWS, h.shape[0]), 0)
    return (jnp.where(row == i1, w1, 0.0) + jnp.where(row == i2, w2, 0.0)
            + jnp.where(row == EXPERTS_PER_GROUP, g_idx.astype(F32), 0.0))


def _xattn_kernel(x_ref, g_ref, wq_ref, kv_ref, wo_ref, fg_ref, wr_ref, br_ref, o_ref, rt_ref):
    x = x_ref[...]
    h = _rms(x, g_ref[...]).astype(BF16)
    q = (jnp.dot(h, wq_ref[...], preferred_element_type=F32) * (HEAD_DIM ** -0.5)).astype(BF16)
    heads = []
    for hh in range(XATTN_HEADS):
        cols = slice(hh * HEAD_DIM, (hh + 1) * HEAD_DIM)
        k = kv_ref[:, cols]
        v = kv_ref[:, XATTN_W + hh * HEAD_DIM:XATTN_W + (hh + 1) * HEAD_DIM]
        s = lax.dot_general(q[:, cols], k, _NT, preferred_element_type=F32)
        p = jnp.exp(s - jnp.max(s, axis=-1, keepdims=True))
        o = jnp.dot(p.astype(BF16), v, preferred_element_type=F32) / jnp.sum(p, axis=-1, keepdims=True)
        heads.append(o.astype(BF16))
    y = x + jnp.dot(jnp.concatenate(heads, axis=1), wo_ref[...], preferred_element_type=F32)
    o_ref[...] = y
    rt_ref[...] = _route(_rms(y, fg_ref[...]), wr_ref[...], br_ref[...])


def _xattn(x3, norm_g, wq, kv3, wo, ffn_norm_g, wr_t, br, *, tm):
    b, s, d = x3.shape
    m = kv3.shape[1]
    tm = _tile(s, tm)
    n = s // tm

    def whole(*shape):
        return pl.BlockSpec(shape, lambda bi, i: (0,) * len(shape))

    return pl.pallas_call(
        _xattn_kernel,
        grid=(b, n),
        in_specs=[
            pl.BlockSpec((None, tm, d), lambda bi, i: (bi, i, 0)),
            whole(1, d),
            whole(d, XATTN_W),
            pl.BlockSpec((None, m, 2 * XATTN_W), lambda bi, i: (bi, 0, 0)),
            whole(XATTN_W, d),
            whole(1, d),
            whole(2 * ROUTER_ROWS, d),
            whole(ROUTER_ROWS, 1),
        ],
        out_specs=[pl.BlockSpec((None, tm, d), lambda bi, i: (bi, i, 0)),
                   pl.BlockSpec((ROUTE_ROWS, tm), lambda bi, i: (0, bi * n + i))],
        out_shape=[jax.ShapeDtypeStruct((b, s, d), F32), jax.ShapeDtypeStruct((ROUTE_ROWS, b * s), F32)],
        compiler_params=_params("parallel", "parallel"),
        name="memory_cross_attention",
    )(x3, norm_g.reshape(1, d), wq, kv3, wo, ffn_norm_g.reshape(1, d), wr_t, br)


def _row_gather_kernel(idx_ref, src_ref, dst_ref, sem, *, rows):
    def start(r, carry):
        pltpu.make_async_copy(src_ref.at[pl.ds(idx_ref[0, r], 1), :], dst_ref.at[pl.ds(r, 1), :], sem).start()
        return carry

    lax.fori_loop(0, rows, start, 0, unroll=8)
    pltpu.make_async_copy(src_ref.at[pl.ds(0, rows), :], dst_ref, sem).wait()


def _row_gather(src, idx, *, rows):
    n = idx.shape[0]
    d = src.shape[1]
    rows = _tile(n, rows)
    return pl.pallas_call(
        functools.partial(_row_gather_kernel, rows=rows),
        grid=(n // rows,),
        in_specs=[
            pl.BlockSpec((None, 1, rows), lambda i: (i, 0, 0), memory_space=pltpu.SMEM),
            pl.BlockSpec(memory_space=pl.ANY),
        ],
        out_specs=pl.BlockSpec((rows, d), lambda i: (i, 0)),
        out_shape=jax.ShapeDtypeStruct((n, d), src.dtype),
        scratch_shapes=[pltpu.SemaphoreType.DMA(())],
        compiler_params=_params("arbitrary"),
        name="row_gather",
    )(idx.reshape(n // rows, 1, rows), src)


def _moe_kernel(tg_ref, nu_ref, x_ref, w_ref, ng_ref, wg_ref, wu_ref, wd_ref, fg_ref, o_ref, h_ref, acc_ref, *,
                final_norm):
    i = pl.program_id(0)
    j = pl.program_id(1)
    used = i < nu_ref[0]

    @pl.when(j == 0)
    def _():
        acc_ref[...] = jnp.zeros(acc_ref.shape, F32)
        h_ref[...] = _rms(x_ref[...], ng_ref[...]).astype(BF16)

    @pl.when(used)
    def _():
        h = h_ref[...]
        a = jnp.dot(h, wg_ref[...], preferred_element_type=F32)
        hid = (a * jax.nn.sigmoid(a)) * jnp.dot(h, wu_ref[...], preferred_element_type=F32)
        w = w_ref[...]
        lane = lax.broadcasted_iota(jnp.int32, w.shape, 1)
        wj = jnp.sum(jnp.where(lane == j, w, 0.0), axis=-1, keepdims=True)
        acc_ref[...] += jnp.dot(hid.astype(BF16), wd_ref[...], preferred_element_type=F32) * wj

    @pl.when(j == pl.num_programs(1) - 1)
    def _():
        y = x_ref[...] + acc_ref[...]
        if final_norm:
            y = _rms(y, fg_ref[...])
        o_ref[...] = y


def _moe_grouped(xp, wp, tile_group, n_used, norm_g, w_gate, w_up, w_down, final_g, *, tm, final_norm):
    p, d = xp.shape

    def expert(i, j, tg, nu):
        return (tg[i] * EXPERTS_PER_GROUP + j, 0, 0)

    grid_spec = pltpu.PrefetchScalarGridSpec(
        num_scalar_prefetch=2,
        grid=(p // tm, EXPERTS_PER_GROUP),
        in_specs=[
            pl.BlockSpec((tm, d), lambda i, j, tg, nu: (i, 0)),
            pl.BlockSpec((tm, EXPERTS_PER_GROUP), lambda i, j, tg, nu: (i, 0)),
            pl.BlockSpec((1, d), lambda i, j, tg, nu: (0, 0)),
            pl.BlockSpec((None, d, EXPERT_FF), expert),
            pl.BlockSpec((None, d, EXPERT_FF), expert),
            pl.BlockSpec((None, EXPERT_FF, d), expert),
            pl.BlockSpec((1, d), lambda i, j, tg, nu: (0, 0)),
        ],
        out_specs=pl.BlockSpec((tm, d), lambda i, j, tg, nu: (i, 0)),
        scratch_shapes=[pltpu.VMEM((tm, d), BF16), pltpu.VMEM((tm, d), F32)],
    )
    return pl.pallas_call(
        functools.partial(_moe_kernel, final_norm=final_norm),
        grid_spec=grid_spec,
        out_shape=jax.ShapeDtypeStruct((p, d), F32),
        compiler_params=_params("parallel", "arbitrary"),
        name="experts",
    )(tile_group, n_used, xp, wp, norm_g.reshape(1, d), w_gate, w_up, w_down, final_g.reshape(1, d))


def _group_plan(route, tm):
    t = route.shape[1]
    gid = route[EXPERTS_PER_GROUP].astype(jnp.int32)
    onehot = (gid[:, None] == jnp.arange(N_GROUPS, dtype=jnp.int32)[None, :]).astype(jnp.int32)
    csum = jnp.cumsum(onehot, axis=0)
    count = csum[-1]
    rank = jnp.take_along_axis(csum, gid[:, None], axis=1)[:, 0] - 1
    padded = (count + tm - 1) // tm * tm
    end = jnp.cumsum(padded)
    pos = (end - padded)[gid] + rank
    n_tiles = t // tm + N_GROUPS
    src = jnp.zeros((n_tiles * tm,), jnp.int32).at[pos].set(jnp.arange(t, dtype=jnp.int32))
    tile_start = jnp.arange(n_tiles, dtype=jnp.int32) * tm
    tile_group = jnp.minimum(jnp.sum((tile_start[:, None] >= end[None, :]).astype(jnp.int32), axis=1), N_GROUPS - 1)
    n_used = (end[-1] // tm).reshape(1)
    return pos, src, tile_group, n_used


def _moe(x, route, norm_g, w_gate, w_up, w_down, final_g, *, tm, final_norm):
    pos, src, tile_group, n_used = _group_plan(route, tm)
    xp = _row_gather(x, src, rows=512)
    wp = route[:EXPERTS_PER_GROUP].T[src]
    yp = _moe_grouped(xp, wp, tile_group, n_used, norm_g, w_gate, w_up, w_down, final_g, tm=tm,
                      final_norm=final_norm)
    return _row_gather(yp, pos, rows=512)


def _rope_tables(seq_len):
    rows = seq_len // GRID_W
    row = jnp.repeat(jnp.arange(rows, dtype=F32), GRID_W)
    col = jnp.tile(jnp.arange(GRID_W, dtype=F32), rows)
    n_freq = HEAD_DIM // 4
    inv = ROPE_THETA ** (-jnp.arange(n_freq, dtype=F32) / n_freq)
    ar = row[:, None] * inv
    ac = col[:, None] * inv
    cos = jnp.concatenate([jnp.cos(ar), jnp.cos(ac), jnp.cos(ar), jnp.cos(ac)], axis=1)
    sin = jnp.concatenate([-jnp.sin(ar), -jnp.sin(ac), jnp.sin(ar), jnp.sin(ac)], axis=1)
    return cos, sin


def _pair_major(a):
    lead = a.shape[:-1]
    heads = a.shape[-1] // HEAD_DIM
    a = a.reshape(lead + (heads, 2, 2, HEAD_DIM // 4))
    return jnp.swapaxes(a, -3, -2).reshape(lead + (heads * HEAD_DIM,))


def _rotary_layout(w_in):
    parts = [w_in[:, :OFF_B_Q], _pair_major(w_in[:, OFF_B_Q:OFF_B_V]), w_in[:, OFF_B_V:OFF_C_Q],
             _pair_major(w_in[:, OFF_C_Q:OFF_C_V]), w_in[:, OFF_C_V:]]
    return jnp.concatenate(parts, axis=1)


def _prepare_layer(l, p):
    wr_t = jnp.concatenate([p["router_group_w"][l].T, p["router_expert_w"][l].T,
                            jnp.zeros((ROUTER_ROWS - N_GROUPS - N_EXPERTS, D_MODEL), F32)], axis=0)
    wr_hi = wr_t.astype(BF16)
    wr_t = jnp.concatenate([wr_hi, (wr_t - wr_hi.astype(F32)).astype(BF16)], axis=0)
    br = jnp.concatenate([p["router_group_b"][l], p["router_expert_b"][l].reshape(-1),
                          jnp.zeros((ROUTER_ROWS - N_GROUPS - N_EXPERTS,), F32)]).reshape(ROUTER_ROWS, 1)
    return dict(
        norm_mix_g=p["norm_mix_g"][l], w_in=_rotary_layout(p["w_in"][l].astype(BF16)),
        sgu_norm_g=p["sgu_norm_g"][l], sgu_w=p["sgu_w"][l].astype(BF16), sgu_b_t=p["sgu_b"][l].T,
        q_norm_g=_pair_major(p["attn_q_norm_g"][l]), k_norm_g=_pair_major(p["attn_k_norm_g"][l]),
        ret_tabs=_retention_tables(p["ret_decay"][l]), ret_norm_g=p["ret_norm_g"][l],
        w_a=p["w_branch_a"][l].astype(BF16), w_b=p["w_branch_b"][l].astype(BF16),
        w_c=p["w_branch_c"][l].astype(BF16), w_out=p["w_out"][l].astype(BF16),
        xattn_norm_g=p["xattn_norm_g"][l], mem_norm_g=p["mem_norm_g"][l],
        wq=p["xattn_wq"][l].astype(BF16), wkv=p["xattn_wkv"][l].astype(BF16), wo=p["xattn_wo"][l].astype(BF16),
        ffn_norm_g=p["ffn_norm_g"][l], wr_t=wr_t, br=br,
        w_gate=p["expert_w_gate"][l].astype(BF16), w_up=p["expert_w_up"][l].astype(BF16),
        w_down=p["expert_w_down"][l].astype(BF16),
    )


def _trunk(x, mem, layers, final_norm_g):
    b, s, d = x.shape
    t = b * s
    m = mem.shape[1]
    cos_t, sin_t = _rope_tables(s)
    xt = x.reshape(t, d)
    memt = mem.reshape(b * m, d)
    for l, w in enumerate(layers):
        z = _rms_matmul(xt, w["norm_mix_g"], w["w_in"], tm=1024, tn=1280)
        z3 = z.reshape(b, s, IN_WIDTH)
        out_a = _sgu(z3, w["sgu_norm_g"], w["sgu_w"], w["sgu_b_t"], ts=512)
        qr, kr, vaug, cqr, ckr = _prep(z3, cos_t, sin_t, w["q_norm_g"], w["k_norm_g"], ts=512)
        out_b = _flash(qr, kr, vaug, _score_bound(w["q_norm_g"], w["k_norm_g"], b), tq=512, tk=256)
        out_c = _retention(cqr, ckr, z3, w["ret_tabs"], w["ret_norm_g"], ts=512)
        merged = _merge(out_a.reshape(t, SGU_WIDTH), out_b.reshape(t, ATTN_Q_W), out_c.reshape(t, RET_V_W),
                        w["w_a"], w["w_b"], w["w_c"], z, tm=1024, tn=512)
        xt = _matmul_res(merged, w["w_out"], xt, tm=512, tn=2048)
        kv = _rms_matmul(memt, w["mem_norm_g"], w["wkv"], tm=256, tn=512)
        x3, route = _xattn(xt.reshape(b, s, d), w["xattn_norm_g"], w["wq"], kv.reshape(b, m, 2 * XATTN_W), w["wo"],
                           w["ffn_norm_g"], w["wr_t"], w["br"], tm=512)
        xt = x3.reshape(t, d)
        xt = _moe(xt, route, w["ffn_norm_g"], w["w_gate"], w["w_up"], w["w_down"], final_norm_g, tm=512,
                  final_norm=(l == len(layers) - 1))
    return xt.reshape(b, s, d)


def kernel(x_prompt, x_sample, mem_prompt, mem_sample, norm_mix_g, w_in, sgu_norm_g, sgu_w, sgu_b, attn_q_norm_g, attn_k_norm_g, ret_decay, ret_norm_g, w_branch_a, w_branch_b, w_branch_c, w_out, xattn_norm_g, mem_norm_g, xattn_wq, xattn_wkv, xattn_wo, ffn_norm_g, router_group_w, router_group_b, router_expert_w, router_expert_b, expert_w_gate, expert_w_up, expert_w_down, final_norm_g):
    p = dict(norm_mix_g=norm_mix_g, w_in=w_in, sgu_norm_g=sgu_norm_g, sgu_w=sgu_w, sgu_b=sgu_b,
             attn_q_norm_g=attn_q_norm_g, attn_k_norm_g=attn_k_norm_g, ret_decay=ret_decay, ret_norm_g=ret_norm_g,
             w_branch_a=w_branch_a, w_branch_b=w_branch_b, w_branch_c=w_branch_c, w_out=w_out,
             xattn_norm_g=xattn_norm_g, mem_norm_g=mem_norm_g, xattn_wq=xattn_wq, xattn_wkv=xattn_wkv,
             xattn_wo=xattn_wo, ffn_norm_g=ffn_norm_g, router_group_w=router_group_w, router_group_b=router_group_b,
             router_expert_w=router_expert_w, router_expert_b=router_expert_b, expert_w_gate=expert_w_gate,
             expert_w_up=expert_w_up, expert_w_down=expert_w_down)
    layers = [_prepare_layer(l, p) for l in range(DEPTH)]
    y_prompt = _trunk(x_prompt, mem_prompt, layers, final_norm_g)
    y_sample = _trunk(x_sample, mem_sample, layers, final_norm_g)
    return (y_prompt, y_sample)
```

```python
import functools

import jax
import jax.numpy as jnp
import numpy as np
from jax import lax
from jax.experimental import pallas as pl
from jax.experimental.pallas import tpu as pltpu

F32 = jnp.float32
BF16 = jnp.bfloat16

D_MODEL = 2048
DEPTH = 2
GRID_W = 64
ROPE_THETA = 10000.0
CHUNK = 128
HEAD_DIM = 128
NORM_EPS = 1e-6
SGU_WIDTH = 1024
SGU_GROUPS = 4
SGU_GROUP_W = SGU_WIDTH // SGU_GROUPS
ATTN_HEADS = 8
ATTN_KV_HEADS = 2
ATTN_REP = ATTN_HEADS // ATTN_KV_HEADS
ATTN_Q_W = ATTN_HEADS * HEAD_DIM
ATTN_KV_W = ATTN_KV_HEADS * HEAD_DIM
RET_HEADS = 4
RET_V_DIM = 256
RET_QK_W = RET_HEADS * HEAD_DIM
RET_V_W = RET_HEADS * RET_V_DIM
XATTN_HEADS = 4
XATTN_W = XATTN_HEADS * HEAD_DIM
N_GROUPS = 4
EXPERTS_PER_GROUP = 4
N_EXPERTS = N_GROUPS * EXPERTS_PER_GROUP
EXPERT_FF = 512
ROUTER_ROWS = 24
ROUTE_ROWS = 8

OFF_A_U = 0
OFF_A_V = OFF_A_U + SGU_WIDTH
OFF_B_Q = OFF_A_V + SGU_WIDTH
OFF_B_K = OFF_B_Q + ATTN_Q_W
OFF_B_V = OFF_B_K + ATTN_KV_W
OFF_C_Q = OFF_B_V + ATTN_KV_W
OFF_C_K = OFF_C_Q + RET_QK_W
OFF_C_V = OFF_C_K + RET_QK_W
OFF_C_G = OFF_C_V + RET_V_W
OFF_GATE_A = OFF_C_G + RET_V_W
OFF_GATE_B = OFF_GATE_A + D_MODEL
OFF_GATE_C = OFF_GATE_B + D_MODEL
IN_WIDTH = OFF_GATE_C + D_MODEL

V7X_VMEM_LIMIT_BYTES = 56 * 1024 * 1024

_NT = (((1,), (1,)), ((), ()))
LOG2_E = 1.4426950408889634
FIXED_SHIFT_LIMIT = 60.0


def _tile(n, pref):
    t = min(pref, n)
    while n % t:
        t //= 2
    return t


def _params(*sem):
    return pltpu.CompilerParams(dimension_semantics=sem, vmem_limit_bytes=V7X_VMEM_LIMIT_BYTES)


def _rms(x, g):
    return x * lax.rsqrt(jnp.mean(x * x, axis=-1, keepdims=True) + NORM_EPS) * g


def _rms_matmul_kernel(x_ref, g_ref, w_ref, o_ref, h_ref):
    @pl.when(pl.program_id(1) == 0)
    def _():
        h_ref[...] = _rms(x_ref[...], g_ref[...]).astype(BF16)

    o_ref[...] = jnp.dot(h_ref[...], w_ref[...], preferred_element_type=F32).astype(o_ref.dtype)


def _rms_matmul(x, g, w, *, tm, tn):
    t, d = x.shape
    n = w.shape[1]
    tm, tn = _tile(t, tm), _tile(n, tn)
    return pl.pallas_call(
        _rms_matmul_kernel,
        grid=(t // tm, n // tn),
        in_specs=[
            pl.BlockSpec((tm, d), lambda i, j: (i, 0)),
            pl.BlockSpec((1, d), lambda i, j: (0, 0)),
            pl.BlockSpec((d, tn), lambda i, j: (0, j)),
        ],
        out_specs=pl.BlockSpec((tm, tn), lambda i, j: (i, j)),
        out_shape=jax.ShapeDtypeStruct((t, n), BF16),
        scratch_shapes=[pltpu.VMEM((tm, d), BF16)],
        compiler_params=_params("parallel", "arbitrary"),
        name="rms_matmul",
    )(x, g.reshape(1, d), w)


def _gelu(x):
    return 0.5 * x * (1.0 + lax.erf(x * (2.0 ** -0.5)))


def _sgu_kernel(u_ref, v_ref, ng_ref, ws_ref, bs_ref, o_ref, *, n_chunks):
    for c in range(n_chunks):
        rows = slice(c * CHUNK, (c + 1) * CHUNK)
        u = _gelu(u_ref[rows, :].astype(F32))
        v = _gelu(v_ref[rows, :].astype(F32))
        vc = v - jnp.mean(v, axis=-1, keepdims=True)
        vn = vc * lax.rsqrt(jnp.mean(vc * vc, axis=-1, keepdims=True) + NORM_EPS) * ng_ref[...]
        vn = vn.astype(BF16)
        for g in range(SGU_GROUPS):
            cols = slice(g * SGU_GROUP_W, (g + 1) * SGU_GROUP_W)
            s = jnp.dot(ws_ref[g], vn[:, cols], preferred_element_type=F32) + bs_ref[:, g:g + 1]
            o_ref[rows, cols] = (u[:, cols] * s).astype(o_ref.dtype)


def _sgu(z3, norm_g, w_s, b_s_t, *, ts):
    b, s, _ = z3.shape
    ts = _tile(s, ts)
    return pl.pallas_call(
        functools.partial(_sgu_kernel, n_chunks=ts // CHUNK),
        grid=(b, s // ts),
        in_specs=[
            pl.BlockSpec((None, ts, SGU_WIDTH), lambda bi, i: (bi, i, OFF_A_U // SGU_WIDTH)),
            pl.BlockSpec((None, ts, SGU_WIDTH), lambda bi, i: (bi, i, OFF_A_V // SGU_WIDTH)),
            pl.BlockSpec((1, SGU_WIDTH), lambda bi, i: (0, 0)),
            pl.BlockSpec((SGU_GROUPS, CHUNK, CHUNK), lambda bi, i: (0, 0, 0)),
            pl.BlockSpec((CHUNK, SGU_GROUPS), lambda bi, i: (0, 0)),
        ],
        out_specs=pl.BlockSpec((None, ts, SGU_WIDTH), lambda bi, i: (bi, i, 0)),
        out_shape=jax.ShapeDtypeStruct((b, s, SGU_WIDTH), BF16),
        compiler_params=_params("parallel", "parallel"),
        name="sgu",
    )(z3, z3, norm_g.reshape(1, SGU_WIDTH), w_s, b_s_t)


def _rope(x, cos, sin_signed):
    return x * cos + pltpu.roll(x, HEAD_DIM // 2, 1) * sin_signed


def _prep_kernel(bq_ref, bk_ref, bv_ref, cq_ref, ck_ref, cos_ref, sin_ref, qg_ref, kg_ref,
                 oq_ref, ok_ref, ov_ref, ocq_ref, ock_ref):
    cos = cos_ref[...]
    sin = sin_ref[...]
    scale = HEAD_DIM ** -0.5
    for h in range(ATTN_HEADS):
        cols = slice(h * HEAD_DIM, (h + 1) * HEAD_DIM)
        x = _rope(_rms(bq_ref[:, cols].astype(F32), qg_ref[...]), cos, sin) * (scale * LOG2_E)
        oq_ref[:, cols] = x.astype(BF16)
    for h in range(ATTN_KV_HEADS):
        cols = slice(h * HEAD_DIM, (h + 1) * HEAD_DIM)
        x = _rope(_rms(bk_ref[:, cols].astype(F32), kg_ref[...]), cos, sin)
        ok_ref[:, cols] = x.astype(BF16)
        ov_ref[:, 2 * h * HEAD_DIM:(2 * h + 1) * HEAD_DIM] = bv_ref[:, cols]
        ov_ref[:, (2 * h + 1) * HEAD_DIM:(2 * h + 2) * HEAD_DIM] = jnp.ones((bv_ref.shape[0], HEAD_DIM), BF16)
    for h in range(RET_HEADS):
        cols = slice(h * HEAD_DIM, (h + 1) * HEAD_DIM)
        ocq_ref[:, cols] = (_rope(cq_ref[:, cols].astype(F32), cos, sin) * scale).astype(BF16)
        ock_ref[:, cols] = _rope(ck_ref[:, cols].astype(F32), cos, sin).astype(BF16)


def _prep(z3, cos_t, sin_t, q_norm_g, k_norm_g, *, ts):
    b, s, _ = z3.shape
    ts = _tile(s, ts)
    n = s // ts

    def zspec(width, off):
        return pl.BlockSpec((None, ts, width), lambda bi, i: (bi, i, off // width))

    def ospec(width):
        return pl.BlockSpec((None, ts, width), lambda bi, i: (bi, i, 0))

    tab = pl.BlockSpec((ts, HEAD_DIM), lambda bi, i: (i, 0))
    gain = pl.BlockSpec((1, HEAD_DIM), lambda bi, i: (0, 0))
    widths = (ATTN_Q_W, ATTN_KV_W, 2 * ATTN_KV_W, RET_QK_W, RET_QK_W)
    return pl.pallas_call(
        _prep_kernel,
        grid=(b, n),
        in_specs=[zspec(ATTN_Q_W, OFF_B_Q), zspec(ATTN_KV_W, OFF_B_K), zspec(ATTN_KV_W, OFF_B_V),
                  zspec(RET_QK_W, OFF_C_Q), zspec(RET_QK_W, OFF_C_K), tab, tab, gain, gain],
        out_specs=[ospec(w) for w in widths],
        out_shape=[jax.ShapeDtypeStruct((b, s, w), BF16) for w in widths],
        compiler_params=_params("parallel", "parallel"),
        name="qk_prep",
    )(z3, z3, z3, z3, z3, cos_t, sin_t, q_norm_g.reshape(1, HEAD_DIM), k_norm_g.reshape(1, HEAD_DIM))


def _flash_kernel(b2_ref, q_ref, k_ref, v_ref, o_ref, qc_ref, acc_ref, m_ref, *, tq, tk):
    for r in range(ATTN_REP):
        qc_ref[r * tq:(r + 1) * tq, :] = q_ref[:, r * HEAD_DIM:(r + 1) * HEAD_DIM]
    b2 = b2_ref[pl.program_id(0)]
    n_kv = k_ref.shape[0] // tk
    acc_ref[...] = jnp.zeros(acc_ref.shape, F32)

    def tiles(j):
        off = pl.multiple_of(j * tk, tk)
        k = k_ref[pl.ds(off, tk), :]
        return lax.dot_general(qc_ref[...], k, _NT, preferred_element_type=F32), v_ref[pl.ds(off, tk), :]

    @pl.when(b2 <= FIXED_SHIFT_LIMIT)
    def _():
        def body(j, carry):
            s, v = tiles(j)
            acc_ref[...] += jnp.dot(jnp.exp2(s - b2).astype(BF16), v, preferred_element_type=F32)
            return carry

        lax.fori_loop(0, n_kv, body, 0, unroll=16 if n_kv % 16 == 0 else 1)

    @pl.when(b2 > FIXED_SHIFT_LIMIT)
    def _():
        m_ref[...] = jnp.full(m_ref.shape, -jnp.inf, F32)

        def body(j, carry):
            s, v = tiles(j)
            m_old = m_ref[...]
            m_new = jnp.maximum(m_old, jnp.max(s, axis=-1, keepdims=True))
            p = jnp.exp2(s - m_new).astype(BF16)
            acc_ref[...] = jnp.exp2(m_old - m_new) * acc_ref[...] + jnp.dot(p, v, preferred_element_type=F32)
            m_ref[...] = m_new
            return carry

        lax.fori_loop(0, n_kv, body, 0)

    out = acc_ref[:, :HEAD_DIM] / acc_ref[:, HEAD_DIM:]
    for r in range(ATTN_REP):
        o_ref[:, r * HEAD_DIM:(r + 1) * HEAD_DIM] = out[r * tq:(r + 1) * tq, :].astype(o_ref.dtype)


def _flash(qr, kr, vaug, bound2, *, tq, tk):
    b, s, _ = qr.shape
    tq, tk = _tile(s, tq), _tile(s, tk)
    gw = ATTN_REP * HEAD_DIM
    grid_spec = pltpu.PrefetchScalarGridSpec(
        num_scalar_prefetch=1,
        grid=(b, ATTN_KV_HEADS, s // tq),
        in_specs=[
            pl.BlockSpec((None, tq, gw), lambda bi, g, i, b2: (bi, i, g)),
            pl.BlockSpec((None, s, HEAD_DIM), lambda bi, g, i, b2: (bi, 0, g)),
            pl.BlockSpec((None, s, 2 * HEAD_DIM), lambda bi, g, i, b2: (bi, 0, g)),
        ],
        out_specs=pl.BlockSpec((None, tq, gw), lambda bi, g, i, b2: (bi, i, g)),
        scratch_shapes=[
            pltpu.VMEM((ATTN_REP * tq, HEAD_DIM), BF16),
            pltpu.VMEM((ATTN_REP * tq, 2 * HEAD_DIM), F32),
            pltpu.VMEM((ATTN_REP * tq, 1), F32),
        ],
    )
    return pl.pallas_call(
        functools.partial(_flash_kernel, tq=tq, tk=tk),
        grid_spec=grid_spec,
        out_shape=jax.ShapeDtypeStruct((b, s, ATTN_Q_W), BF16),
        compiler_params=_params("parallel", "parallel", "arbitrary"),
        name="flash_attention",
    )(bound2, qr, kr, vaug)


def _score_bound(q_norm_g, k_norm_g, batch):
    bound = LOG2_E * HEAD_DIM ** 0.5 * jnp.max(jnp.abs(q_norm_g)) * jnp.max(jnp.abs(k_norm_g)) * 1.01
    return jnp.full((batch,), bound, F32)


def _ret_head_views(refs, rows, h, width):
    half = RET_HEADS // 2
    return refs[h // half][rows, (h % half) * width:(h % half + 1) * width]


def _ret_head(chunks, h, q_ref, k_ref, v_refs, st, dm, qd, kd, cd):
    qk = slice(h * HEAD_DIM, (h + 1) * HEAD_DIM)
    intra, outer = {}, {}
    for c in chunks:
        rows = slice(c * CHUNK, (c + 1) * CHUNK)
        q, k, v = q_ref[rows, qk], k_ref[rows, qk], _ret_head_views(v_refs, rows, h, RET_V_DIM)
        s = lax.dot_general(q, k, _NT, preferred_element_type=F32) * dm
        intra[c] = jnp.dot(s.astype(BF16), v, preferred_element_type=F32)
        kt = (k.astype(F32).T * kd).astype(BF16)
        outer[c] = jnp.dot(kt, v, preferred_element_type=F32)
    out = {}
    for c in chunks:
        rows = slice(c * CHUNK, (c + 1) * CHUNK)
        out[c] = intra[c] + qd * jnp.dot(q_ref[rows, qk], st.astype(BF16), preferred_element_type=F32)
        st = cd * st + outer[c]
    return out, st


def _ret_fwd_kernel(q_ref, k_ref, v0_ref, v1_ref, dm_ref, qd_ref, kd_ref, cd_ref, o_ref, st_ref, *, n_chunks):
    @pl.when(pl.program_id(1) == 0)
    def _():
        st_ref[...] = jnp.zeros(st_ref.shape, F32)

    chunks = list(range(n_chunks))
    for h in range(RET_HEADS):
        out, st = _ret_head(chunks, h, q_ref, k_ref, (v0_ref, v1_ref), st_ref[h],
                            dm_ref[h], qd_ref[h], kd_ref[h], cd_ref[h])
        st_ref[h] = st
        for c in chunks:
            o_ref[c * CHUNK:(c + 1) * CHUNK, h * RET_V_DIM:(h + 1) * RET_V_DIM] = out[c]


def _ret_bwd_kernel(q_ref, k_ref, v0_ref, v1_ref, g0_ref, g1_ref, of_ref, dm_ref, qd_ref, kd_ref, cd_ref, ng_ref,
                    o_ref, st_ref, *, n_chunks):
    @pl.when(pl.program_id(1) == 0)
    def _():
        st_ref[...] = jnp.zeros(st_ref.shape, F32)

    chunks = list(reversed(range(n_chunks)))
    for h in range(RET_HEADS):
        vc = slice(h * RET_V_DIM, (h + 1) * RET_V_DIM)
        out, st = _ret_head(chunks, h, q_ref, k_ref, (v0_ref, v1_ref), st_ref[h],
                            dm_ref[h], qd_ref[h], kd_ref[h], cd_ref[h])
        st_ref[h] = st
        for c in chunks:
            rows = slice(c * CHUNK, (c + 1) * CHUNK)
            o = out[c] + of_ref[rows, vc]
            oc = o - jnp.mean(o, axis=-1, keepdims=True)
            o = oc * lax.rsqrt(jnp.mean(oc * oc, axis=-1, keepdims=True) + NORM_EPS) * ng_ref[:, vc]
            gate = _ret_head_views((g0_ref, g1_ref), rows, h, RET_V_DIM).astype(F32)
            o_ref[rows, vc] = (o * (gate * jax.nn.sigmoid(gate))).astype(o_ref.dtype)


def _retention(cqr, ckr, z3, tabs, norm_g, *, ts):
    b, s, _ = cqr.shape
    ts = _tile(s, ts)
    n = s // ts
    dv = RET_V_DIM
    half_w = RET_V_W // 2

    def seq(width, off, rev, part=0):
        if rev:
            return pl.BlockSpec((None, ts, width), lambda bi, i: (bi, n - 1 - i, off // width + part))
        return pl.BlockSpec((None, ts, width), lambda bi, i: (bi, i, off // width + part))

    def whole(*shape):
        return pl.BlockSpec(shape, lambda bi, i: (0,) * len(shape))

    tab_specs = [whole(RET_HEADS, CHUNK, CHUNK), whole(RET_HEADS, CHUNK, 1), whole(RET_HEADS, 1, CHUNK),
                 whole(RET_HEADS, 1, dv)]
    state = [pltpu.VMEM((RET_HEADS, HEAD_DIM, dv), F32)]
    sem = _params("parallel", "arbitrary")
    o_fwd = pl.pallas_call(
        functools.partial(_ret_fwd_kernel, n_chunks=ts // CHUNK),
        grid=(b, n),
        in_specs=[seq(RET_QK_W, 0, False), seq(RET_QK_W, 0, False), seq(half_w, OFF_C_V, False, 0),
                  seq(half_w, OFF_C_V, False, 1)] + tab_specs,
        out_specs=seq(RET_V_W, 0, False),
        out_shape=jax.ShapeDtypeStruct((b, s, RET_V_W), F32),
        scratch_shapes=state,
        compiler_params=sem,
        name="retention_fwd",
    )(cqr, ckr, z3, z3, *tabs[0])
    return pl.pallas_call(
        functools.partial(_ret_bwd_kernel, n_chunks=ts // CHUNK),
        grid=(b, n),
        in_specs=[seq(RET_QK_W, 0, True), seq(RET_QK_W, 0, True), seq(half_w, OFF_C_V, True, 0),
                  seq(half_w, OFF_C_V, True, 1), seq(half_w, OFF_C_G, True, 0), seq(half_w, OFF_C_G, True, 1),
                  seq(RET_V_W, 0, True)] + tab_specs + [whole(1, RET_V_W)],
        out_specs=seq(RET_V_W, 0, True),
        out_shape=jax.ShapeDtypeStruct((b, s, RET_V_W), BF16),
        scratch_shapes=state,
        compiler_params=sem,
        name="retention_bwd",
    )(cqr, ckr, z3, z3, z3, z3, o_fwd, *tabs[1], norm_g.reshape(1, RET_V_W))


def _retention_tables(ret_decay):
    lg = -jnp.exp(ret_decay.astype(F32))
    idx = jnp.arange(CHUNK, dtype=F32)
    diff = idx[:, None] - idx[None, :]
    out = []
    for d, (mask, delta, qpow, kpow) in enumerate((
            (diff >= 0, diff, idx + 1.0, CHUNK - 1.0 - idx),
            (diff < 0, -diff, CHUNK - idx, idx))):
        l = lg[d][:, None, None]
        dm = jnp.where(mask, jnp.exp(l * jnp.where(mask, delta, 0.0)), 0.0)
        qd = jnp.exp(lg[d][:, None] * qpow)[:, :, None]
        kd = jnp.exp(lg[d][:, None] * kpow)[:, None, :]
        cd = jnp.broadcast_to(jnp.exp(lg[d] * CHUNK)[:, None, None], (RET_HEADS, 1, RET_V_DIM))
        out.append((dm, qd, kd, cd))
    return out


def _merge_kernel(a_ref, b_ref, c_ref, wa_ref, wb_ref, wc_ref, ga_ref, gb_ref, gc_ref, o_ref):
    acc = jax.nn.sigmoid(ga_ref[...].astype(F32)) * jnp.dot(a_ref[...], wa_ref[...], preferred_element_type=F32)
    acc = acc + jax.nn.sigmoid(gb_ref[...].astype(F32)) * jnp.dot(b_ref[...], wb_ref[...], preferred_element_type=F32)
    acc = acc + jax.nn.sigmoid(gc_ref[...].astype(F32)) * jnp.dot(c_ref[...], wc_ref[...], preferred_element_type=F32)
    o_ref[...] = acc.astype(o_ref.dtype)


def _merge(out_a, out_b, out_c, w_a, w_b, w_c, z, *, tm, tn):
    t, k = out_a.shape
    tm, tn = _tile(t, tm), _tile(D_MODEL, tn)
    lhs = pl.BlockSpec((tm, k), lambda i, j: (i, 0))
    rhs = pl.BlockSpec((k, tn), lambda i, j: (0, j))

    def gate(off):
        return pl.BlockSpec((tm, tn), lambda i, j: (i, off // tn + j))

    return pl.pallas_call(
        _merge_kernel,
        grid=(t // tm, D_MODEL // tn),
        in_specs=[lhs, lhs, lhs, rhs, rhs, rhs, gate(OFF_GATE_A), gate(OFF_GATE_B), gate(OFF_GATE_C)],
        out_specs=pl.BlockSpec((tm, tn), lambda i, j: (i, j)),
        out_shape=jax.ShapeDtypeStruct((t, D_MODEL), BF16),
        compiler_params=_params("parallel", "parallel"),
        name="branch_merge",
    )(out_a, out_b, out_c, w_a, w_b, w_c, z, z, z)


def _matmul_res_kernel(a_ref, w_ref, x_ref, o_ref):
    o_ref[...] = x_ref[...] + jnp.dot(a_ref[...], w_ref[...], preferred_element_type=F32)


def _matmul_res(a, w, x, *, tm, tn):
    t, k = a.shape
    n = w.shape[1]
    tm, tn = _tile(t, tm), _tile(n, tn)
    return pl.pallas_call(
        _matmul_res_kernel,
        grid=(t // tm, n // tn),
        in_specs=[
            pl.BlockSpec((tm, k), lambda i, j: (i, 0)),
            pl.BlockSpec((k, tn), lambda i, j: (0, j)),
            pl.BlockSpec((tm, tn), lambda i, j: (i, j)),
        ],
        out_specs=pl.BlockSpec((tm, tn), lambda i, j: (i, j)),
        out_shape=jax.ShapeDtypeStruct((t, n), F32),
        compiler_params=_params("parallel", "parallel"),
        name="out_proj_residual",
    )(a, w, x)


def _route(h, wr, br):
    h_hi = h.astype(BF16)
    h_lo = (h - h_hi.astype(F32)).astype(BF16)
    both = lax.dot_general(wr, h_hi, _NT, preferred_element_type=F32)
    lt = (both[:ROUTER_ROWS] + both[ROUTER_ROWS:]
          + lax.dot_general(wr[:ROUTER_ROWS], h_lo, _NT, preferred_element_type=F32) + br)
    gl = [lt[i:i + 1, :] for i in range(N_GROUPS)]
    gmax = jnp.maximum(jnp.maximum(gl[0], gl[1]), jnp.maximum(gl[2], gl[3]))
    denom = sum(jnp.exp(v - gmax) for v in gl)
    g_prob = 1.0 / denom
    g_idx = jnp.where(gl[0] == gmax, 0, jnp.where(gl[1] == gmax, 1, jnp.where(gl[2] == gmax, 2, 3)))
    sel = []
    for j in range(EXPERTS_PER_GROUP):
        rows = [lt[N_GROUPS + g * EXPERTS_PER_GROUP + j:N_GROUPS + g * EXPERTS_PER_GROUP + j + 1, :]
                for g in range(N_GROUPS)]
        sel.append(jnp.where(g_idx == 0, rows[0], jnp.where(g_idx == 1, rows[1],
                                                            jnp.where(g_idx == 2, rows[2], rows[3]))))
    top1 = jnp.maximum(jnp.maximum(sel[0], sel[1]), jnp.maximum(sel[2], sel[3]))
    i1 = jnp.where(sel[0] == top1, 0, jnp.where(sel[1] == top1, 1, jnp.where(sel[2] == top1, 2, 3)))
    rest = [jnp.where(i1 == j, -jnp.inf, sel[j]) for j in range(EXPERTS_PER_GROUP)]
    top2 = jnp.maximum(jnp.maximum(rest[0], rest[1]), jnp.maximum(rest[2], rest[3]))
    i2 = jnp.where(rest[0] == top2, 0, jnp.where(rest[1] == top2, 1, jnp.where(rest[2] == top2, 2, 3)))
    e2 = jnp.exp(top2 - top1)
    w1 = g_prob / (1.0 + e2)
    w2 = g_prob * e2 / (1.0 + e2)
    row = lax.broadcasted_iota(jnp.int32, (ROUTE_ROWS, h.shape[0]), 0)
    return (jnp.where(row == i1, w1, 0.0) + jnp.where(row == i2, w2, 0.0)
            + jnp.where(row == EXPERTS_PER_GROUP, g_idx.astype(F32), 0.0))


def _xattn_kernel(x_ref, g_ref, wq_ref, kv_ref, wo_ref, fg_ref, wr_ref, br_ref, o_ref, rt_ref):
    x = x_ref[...]
    h = _rms(x, g_ref[...]).astype(BF16)
    q = (jnp.dot(h, wq_ref[...], preferred_element_type=F32) * (HEAD_DIM ** -0.5)).astype(BF16)
    heads = []
    for hh in range(XATTN_HEADS):
        cols = slice(hh * HEAD_DIM, (hh + 1) * HEAD_DIM)
        k = kv_ref[:, cols]
        v = kv_ref[:, XATTN_W + hh * HEAD_DIM:XATTN_W + (hh + 1) * HEAD_DIM]
        s = lax.dot_general(q[:, cols], k, _NT, preferred_element_type=F32)
        p = jnp.exp(s - jnp.max(s, axis=-1, keepdims=True))
        o = jnp.dot(p.astype(BF16), v, preferred_element_type=F32) / jnp.sum(p, axis=-1, keepdims=True)
        heads.append(o.astype(BF16))
    y = x + jnp.dot(jnp.concatenate(heads, axis=1), wo_ref[...], preferred_element_type=F32)
    o_ref[...] = y
    rt_ref[...] = _route(_rms(y, fg_ref[...]), wr_ref[...], br_ref[...])


def _xattn(x3, norm_g, wq, kv3, wo, ffn_norm_g, wr_t, br, *, tm):
    b, s, d = x3.shape
    m = kv3.shape[1]
    tm = _tile(s, tm)
    n = s // tm

    def whole(*shape):
        return pl.BlockSpec(shape, lambda bi, i: (0,) * len(shape))

    return pl.pallas_call(
        _xattn_kernel,
        grid=(b, n),
        in_specs=[
            pl.BlockSpec((None, tm, d), lambda bi, i: (bi, i, 0)),
            whole(1, d),
            whole(d, XATTN_W),
            pl.BlockSpec((None, m, 2 * XATTN_W), lambda bi, i: (bi, 0, 0)),
            whole(XATTN_W, d),
            whole(1, d),
            whole(2 * ROUTER_ROWS, d),
            whole(ROUTER_ROWS, 1),
        ],
        out_specs=[pl.BlockSpec((None, tm, d), lambda bi, i: (bi, i, 0)),
                   pl.BlockSpec((ROUTE_ROWS, tm), lambda bi, i: (0, bi * n + i))],
        out_shape=[jax.ShapeDtypeStruct((b, s, d), F32), jax.ShapeDtypeStruct((ROUTE_ROWS, b * s), F32)],
        compiler_params=_params("parallel", "parallel"),
        name="memory_cross_attention",
    )(x3, norm_g.reshape(1, d), wq, kv3, wo, ffn_norm_g.reshape(1, d), wr_t, br)


def _row_gather_kernel(idx_ref, src_ref, dst_ref, sem, *, rows):
    def start(r, carry):
        pltpu.make_async_copy(src_ref.at[pl.ds(idx_ref[0, r], 1), :], dst_ref.at[pl.ds(r, 1), :], sem).start()
        return carry

    lax.fori_loop(0, rows, start, 0, unroll=8)
    pltpu.make_async_copy(src_ref.at[pl.ds(0, rows), :], dst_ref, sem).wait()


def _row_gather(src, idx, *, rows):
    n = idx.shape[0]
    d = src.shape[1]
    rows = _tile(n, rows)
    return pl.pallas_call(
        functools.partial(_row_gather_kernel, rows=rows),
        grid=(n // rows,),
        in_specs=[
            pl.BlockSpec((None, 1, rows), lambda i: (i, 0, 0), memory_space=pltpu.SMEM),
            pl.BlockSpec(memory_space=pl.ANY),
        ],
        out_specs=pl.BlockSpec((rows, d), lambda i: (i, 0)),
        out_shape=jax.ShapeDtypeStruct((n, d), src.dtype),
        scratch_shapes=[pltpu.SemaphoreType.DMA(())],
        compiler_params=_params("arbitrary"),
        name="row_gather",
    )(idx.reshape(n // rows, 1, rows), src)


def _moe_kernel(tg_ref, nu_ref, x_ref, w_ref, ng_ref, wg_ref, wu_ref, wd_ref, fg_ref, o_ref, h_ref, acc_ref, *,
                final_norm):
    i = pl.program_id(0)
    j = pl.program_id(1)
    used = i < nu_ref[0]

    @pl.when(j == 0)
    def _():
        acc_ref[...] = jnp.zeros(acc_ref.shape, F32)
        h_ref[...] = _rms(x_ref[...], ng_ref[...]).astype(BF16)

    @pl.when(used)
    def _():
        h = h_ref[...]
        a = jnp.dot(h, wg_ref[...], preferred_element_type=F32)
        hid = (a * jax.nn.sigmoid(a)) * jnp.dot(h, wu_ref[...], preferred_element_type=F32)
        w = w_ref[...]
        lane = lax.broadcasted_iota(jnp.int32, w.shape, 1)
        wj = jnp.sum(jnp.where(lane == j, w, 0.0), axis=-1, keepdims=True)
        acc_ref[...] += jnp.dot(hid.astype(BF16), wd_ref[...], preferred_element_type=F32) * wj

    @pl.when(j == pl.num_programs(1) - 1)
    def _():
        y = x_ref[...] + acc_ref[...]
        if final_norm:
            y = _rms(y, fg_ref[...])
        o_ref[...] = y


def _moe_grouped(xp, wp, tile_group, n_used, norm_g, w_gate, w_up, w_down, final_g, *, tm, final_norm):
    p, d = xp.shape

    def expert(i, j, tg, nu):
        return (tg[i] * EXPERTS_PER_GROUP + j, 0, 0)

    grid_spec = pltpu.PrefetchScalarGridSpec(
        num_scalar_prefetch=2,
        grid=(p // tm, EXPERTS_PER_GROUP),
        in_specs=[
            pl.BlockSpec((tm, d), lambda i, j, tg, nu: (i, 0)),
            pl.BlockSpec((tm, EXPERTS_PER_GROUP), lambda i, j, tg, nu: (i, 0)),
            pl.BlockSpec((1, d), lambda i, j, tg, nu: (0, 0)),
            pl.BlockSpec((None, d, EXPERT_FF), expert),
            pl.BlockSpec((None, d, EXPERT_FF), expert),
            pl.BlockSpec((None, EXPERT_FF, d), expert),
            pl.BlockSpec((1, d), lambda i, j, tg, nu: (0, 0)),
        ],
        out_specs=pl.BlockSpec((tm, d), lambda i, j, tg, nu: (i, 0)),
        scratch_shapes=[pltpu.VMEM((tm, d), BF16), pltpu.VMEM((tm, d), F32)],
    )
    return pl.pallas_call(
        functools.partial(_moe_kernel, final_norm=final_norm),
        grid_spec=grid_spec,
        out_shape=jax.ShapeDtypeStruct((p, d), F32),
        compiler_params=_params("parallel", "arbitrary"),
        name="experts",
    )(tile_group, n_used, xp, wp, norm_g.reshape(1, d), w_gate, w_up, w_down, final_g.reshape(1, d))


def _group_plan(route, tm):
    t = route.shape[1]
    gid = route[EXPERTS_PER_GROUP].astype(jnp.int32)
    onehot = (gid[:, None] == jnp.arange(N_GROUPS, dtype=jnp.int32)[None, :]).astype(jnp.int32)
    csum = jnp.cumsum(onehot, axis=0)
    count = csum[-1]
    rank = jnp.take_along_axis(csum, gid[:, None], axis=1)[:, 0] - 1
    padded = (count + tm - 1) // tm * tm
    end = jnp.cumsum(padded)
    pos = (end - padded)[gid] + rank
    n_tiles = t // tm + N_GROUPS
    src = jnp.zeros((n_tiles * tm,), jnp.int32).at[pos].set(jnp.arange(t, dtype=jnp.int32))
    tile_start = jnp.arange(n_tiles, dtype=jnp.int32) * tm
    tile_group = jnp.minimum(jnp.sum((tile_start[:, None] >= end[None, :]).astype(jnp.int32), axis=1), N_GROUPS - 1)
    n_used = (end[-1] // tm).reshape(1)
    return pos, src, tile_group, n_used


def _moe(x, route, norm_g, w_gate, w_up, w_down, final_g, *, tm, final_norm):
    pos, src, tile_group, n_used = _group_plan(route, tm)
    xp = _row_gather(x, src, rows=512)
    wp = route[:EXPERTS_PER_GROUP].T[src]
    yp = _moe_grouped(xp, wp, tile_group, n_used, norm_g, w_gate, w_up, w_down, final_g, tm=tm,
                      final_norm=final_norm)
    return _row_gather(yp, pos, rows=512)


def _rope_tables(seq_len):
    rows = seq_len // GRID_W
    row = jnp.repeat(jnp.arange(rows, dtype=F32), GRID_W)
    col = jnp.tile(jnp.arange(GRID_W, dtype=F32), rows)
    n_freq = HEAD_DIM // 4
    inv = ROPE_THETA ** (-jnp.arange(n_freq, dtype=F32) / n_freq)
    ar = row[:, None] * inv
    ac = col[:, None] * inv
    cos = jnp.concatenate([jnp.cos(ar), jnp.cos(ac), jnp.cos(ar), jnp.cos(ac)], axis=1)
    sin = jnp.concatenate([-jnp.sin(ar), -jnp.sin(ac), jnp.sin(ar), jnp.sin(ac)], axis=1)
    return cos, sin


def _pair_major(a):
    lead = a.shape[:-1]
    heads = a.shape[-1] // HEAD_DIM
    a = a.reshape(lead + (heads, 2, 2, HEAD_DIM // 4))
    return jnp.swapaxes(a, -3, -2).reshape(lead + (heads * HEAD_DIM,))


def _rotary_layout(w_in):
    parts = [w_in[:, :OFF_B_Q], _pair_major(w_in[:, OFF_B_Q:OFF_B_V]), w_in[:, OFF_B_V:OFF_C_Q],
             _pair_major(w_in[:, OFF_C_Q:OFF_C_V]), w_in[:, OFF_C_V:]]
    return jnp.concatenate(parts, axis=1)


def _prepare_layer(l, p):
    wr_t = jnp.concatenate([p["router_group_w"][l].T, p["router_expert_w"][l].T,
                            jnp.zeros((ROUTER_ROWS - N_GROUPS - N_EXPERTS, D_MODEL), F32)], axis=0)
    wr_hi = wr_t.astype(BF16)
    wr_t = jnp.concatenate([wr_hi, (wr_t - wr_hi.astype(F32)).astype(BF16)], axis=0)
    br = jnp.concatenate([p["router_group_b"][l], p["router_expert_b"][l].reshape(-1),
                          jnp.zeros((ROUTER_ROWS - N_GROUPS - N_EXPERTS,), F32)]).reshape(ROUTER_ROWS, 1)
    return dict(
        norm_mix_g=p["norm_mix_g"][l], w_in=_rotary_layout(p["w_in"][l].astype(BF16)),
        sgu_norm_g=p["sgu_norm_g"][l], sgu_w=p["sgu_w"][l].astype(BF16), sgu_b_t=p["sgu_b"][l].T,
        q_norm_g=_pair_major(p["attn_q_norm_g"][l]), k_norm_g=_pair_major(p["attn_k_norm_g"][l]),
        ret_tabs=_retention_tables(p["ret_decay"][l]), ret_norm_g=p["ret_norm_g"][l],
        w_a=p["w_branch_a"][l].astype(BF16), w_b=p["w_branch_b"][l].astype(BF16),
        w_c=p["w_branch_c"][l].astype(BF16), w_out=p["w_out"][l].astype(BF16),
        xattn_norm_g=p["xattn_norm_g"][l], mem_norm_g=p["mem_norm_g"][l],
        wq=p["xattn_wq"][l].astype(BF16), wkv=p["xattn_wkv"][l].astype(BF16), wo=p["xattn_wo"][l].astype(BF16),
        ffn_norm_g=p["ffn_norm_g"][l], wr_t=wr_t, br=br,
        w_gate=p["expert_w_gate"][l].astype(BF16), w_up=p["expert_w_up"][l].astype(BF16),
        w_down=p["expert_w_down"][l].astype(BF16),
    )


def _trunk(x, mem, layers, final_norm_g):
    b, s, d = x.shape
    t = b * s
    m = mem.shape[1]
    cos_t, sin_t = _rope_tables(s)
    xt = x.reshape(t, d)
    memt = mem.reshape(b * m, d)
    for l, w in enumerate(layers):
        z = _rms_matmul(xt, w["norm_mix_g"], w["w_in"], tm=1024, tn=1280)
        z3 = z.reshape(b, s, IN_WIDTH)
        out_a = _sgu(z3, w["sgu_norm_g"], w["sgu_w"], w["sgu_b_t"], ts=512)
        qr, kr, vaug, cqr, ckr = _prep(z3, cos_t, sin_t, w["q_norm_g"], w["k_norm_g"], ts=512)
        out_b = _flash(qr, kr, vaug, _score_bound(w["q_norm_g"], w["k_norm_g"], b), tq=512, tk=256)
        out_c = _retention(cqr, ckr, z3, w["ret_tabs"], w["ret_norm_g"], ts=512)
        merged = _merge(out_a.reshape(t, SGU_WIDTH), out_b.reshape(t, ATTN_Q_W), out_c.reshape(t, RET_V_W),
                        w["w_a"], w["w_b"], w["w_c"], z, tm=1024, tn=512)
        xt = _matmul_res(merged, w["w_out"], xt, tm=512, tn=2048)
        kv = _rms_matmul(memt, w["mem_norm_g"], w["wkv"], tm=256, tn=512)
        x3, route = _xattn(xt.reshape(b, s, d), w["xattn_norm_g"], w["wq"], kv.reshape(b, m, 2 * XATTN_W), w["wo"],
                           w["ffn_norm_g"], w["wr_t"], w["br"], tm=512)
        xt = x3.reshape(t, d)
        xt = _moe(xt, route, w["ffn_norm_g"], w["w_gate"], w["w_up"], w["w_down"], final_norm_g, tm=512,
                  final_norm=(l == len(layers) - 1))
    return xt.reshape(b, s, d)


def kernel(x_prompt, x_sample, mem_prompt, mem_sample, norm_mix_g, w_in, sgu_norm_g, sgu_w, sgu_b, attn_q_norm_g, attn_k_norm_g, ret_decay, ret_norm_g, w_branch_a, w_branch_b, w_branch_c, w_out, xattn_norm_g, mem_norm_g, xattn_wq, xattn_wkv, xattn_wo, ffn_norm_g, router_group_w, router_group_b, router_expert_w, router_expert_b, expert_w_gate, expert_w_up, expert_w_down, final_norm_g):
    p = dict(norm_mix_g=norm_mix_g, w_in=w_in, sgu_norm_g=sgu_norm_g, sgu_w=sgu_w, sgu_b=sgu_b,
             attn_q_norm_g=attn_q_norm_g, attn_k_norm_g=attn_k_norm_g, ret_decay=ret_decay, ret_norm_g=ret_norm_g,
             w_branch_a=w_branch_a, w_branch_b=w_branch_b, w_branch_c=w_branch_c, w_out=w_out,
             xattn_norm_g=xattn_norm_g, mem_norm_g=mem_norm_g, xattn_wq=xattn_wq, xattn_wkv=xattn_wkv,
             xattn_wo=xattn_wo, ffn_norm_g=ffn_norm_g, router_group_w=router_group_w, router_group_b=router_group_b,
             router_expert_w=router_expert_w, router_expert_b=router_expert_b, expert_w_gate=expert_w_gate,
             expert_w_up=expert_w_up, expert_w_down=expert_w_down)
    layers = [_prepare_layer(l, p) for l in range(DEPTH)]
    y_prompt = _trunk(x_prompt, mem_prompt, layers, final_norm_g)
    y_sample = _trunk(x_sample, mem_sample, layers, final_norm_g)
    return (y_prompt, y_sample)
```

```python
import functools

import jax
import jax.numpy as jnp
import numpy as np
from jax import lax
from jax.experimental import pallas as pl
from jax.experimental.pallas import tpu as pltpu

F32 = jnp.float32
BF16 = jnp.bfloat16

D_MODEL = 2048
DEPTH = 2
GRID_W = 64
ROPE_THETA = 10000.0
CHUNK = 128
HEAD_DIM = 128
NORM_EPS = 1e-6
SGU_WIDTH = 1024
SGU_GROUPS = 4
SGU_GROUP_W = SGU_WIDTH // SGU_GROUPS
ATTN_HEADS = 8
ATTN_KV_HEADS = 2
ATTN_REP = ATTN_HEADS // ATTN_KV_HEADS
ATTN_Q_W = ATTN_HEADS * HEAD_DIM
ATTN_KV_W = ATTN_KV_HEADS * HEAD_DIM
RET_HEADS = 4
RET_V_DIM = 256
RET_QK_W = RET_HEADS * HEAD_DIM
RET_V_W = RET_HEADS * RET_V_DIM
XATTN_HEADS = 4
XATTN_W = XATTN_HEADS * HEAD_DIM
N_GROUPS = 4
EXPERTS_PER_GROUP = 4
N_EXPERTS = N_GROUPS * EXPERTS_PER_GROUP
EXPERT_FF = 512
ROUTER_ROWS = 24
ROUTE_ROWS = 8

OFF_A_U = 0
OFF_A_V = OFF_A_U + SGU_WIDTH
OFF_B_Q = OFF_A_V + SGU_WIDTH
OFF_B_K = OFF_B_Q + ATTN_Q_W
OFF_B_V = OFF_B_K + ATTN_KV_W
OFF_C_Q = OFF_B_V + ATTN_KV_W
OFF_C_K = OFF_C_Q + RET_QK_W
OFF_C_V = OFF_C_K + RET_QK_W
OFF_C_G = OFF_C_V + RET_V_W
OFF_GATE_A = OFF_C_G + RET_V_W
OFF_GATE_B = OFF_GATE_A + D_MODEL
OFF_GATE_C = OFF_GATE_B + D_MODEL
IN_WIDTH = OFF_GATE_C + D_MODEL

V7X_VMEM_LIMIT_BYTES = 56 * 1024 * 1024

_NT = (((1,), (1,)), ((), ()))
LOG2_E = 1.4426950408889634
FIXED_SHIFT_LIMIT = 60.0


def _tile(n, pref):
    t = min(pref, n)
    while n % t:
        t //= 2
    return t


def _params(*sem):
    return pltpu.CompilerParams(dimension_semantics=sem, vmem_limit_bytes=V7X_VMEM_LIMIT_BYTES)


def _rms(x, g):
    return x * lax.rsqrt(jnp.mean(x * x, axis=-1, keepdims=True) + NORM_EPS) * g


def _rms_matmul_kernel(x_ref, g_ref, w_ref, o_ref, h_ref):
    @pl.when(pl.program_id(1) == 0)
    def _():
        h_ref[...] = _rms(x_ref[...], g_ref[...]).astype(BF16)

    o_ref[...] = jnp.dot(h_ref[...], w_ref[...], preferred_element_type=F32).astype(o_ref.dtype)


def _rms_matmul(x, g, w, *, tm, tn):
    t, d = x.shape
    n = w.shape[1]
    tm, tn = _tile(t, tm), _tile(n, tn)
    return pl.pallas_call(
        _rms_matmul_kernel,
        grid=(t // tm, n // tn),
        in_specs=[
            pl.BlockSpec((tm, d), lambda i, j: (i, 0)),
            pl.BlockSpec((1, d), lambda i, j: (0, 0)),
            pl.BlockSpec((d, tn), lambda i, j: (0, j)),
        ],
        out_specs=pl.BlockSpec((tm, tn), lambda i, j: (i, j)),
        out_shape=jax.ShapeDtypeStruct((t, n), BF16),
        scratch_shapes=[pltpu.VMEM((tm, d), BF16)],
        compiler_params=_params("parallel", "arbitrary"),
        name="rms_matmul",
    )(x, g.reshape(1, d), w)


def _gelu(x):
    return 0.5 * x * (1.0 + lax.erf(x * (2.0 ** -0.5)))


def _sgu_kernel(u_ref, v_ref, ng_ref, ws_ref, bs_ref, o_ref, *, n_chunks):
    for c in range(n_chunks):
        rows = slice(c * CHUNK, (c + 1) * CHUNK)
        u = _gelu(u_ref[rows, :].astype(F32))
        v = _gelu(v_ref[rows, :].astype(F32))
        vc = v - jnp.mean(v, axis=-1, keepdims=True)
        vn = vc * lax.rsqrt(jnp.mean(vc * vc, axis=-1, keepdims=True) + NORM_EPS) * ng_ref[...]
        vn = vn.astype(BF16)
        for g in range(SGU_GROUPS):
            cols = slice(g * SGU_GROUP_W, (g + 1) * SGU_GROUP_W)
            s = jnp.dot(ws_ref[g], vn[:, cols], preferred_element_type=F32) + bs_ref[:, g:g + 1]
            o_ref[rows, cols] = (u[:, cols] * s).astype(o_ref.dtype)


def _sgu(z3, norm_g, w_s, b_s_t, *, ts):
    b, s, _ = z3.shape
    ts = _tile(s, ts)
    return pl.pallas_call(
        functools.partial(_sgu_kernel, n_chunks=ts // CHUNK),
        grid=(b, s // ts),
        in_specs=[
            pl.BlockSpec((None, ts, SGU_WIDTH), lambda bi, i: (bi, i, OFF_A_U // SGU_WIDTH)),
            pl.BlockSpec((None, ts, SGU_WIDTH), lambda bi, i: (bi, i, OFF_A_V // SGU_WIDTH)),
            pl.BlockSpec((1, SGU_WIDTH), lambda bi, i: (0, 0)),
            pl.BlockSpec((SGU_GROUPS, CHUNK, CHUNK), lambda bi, i: (0, 0, 0)),
            pl.BlockSpec((CHUNK, SGU_GROUPS), lambda bi, i: (0, 0)),
        ],
        out_specs=pl.BlockSpec((None, ts, SGU_WIDTH), lambda bi, i: (bi, i, 0)),
        out_shape=jax.ShapeDtypeStruct((b, s, SGU_WIDTH), BF16),
        compiler_params=_params("parallel", "parallel"),
        name="sgu",
    )(z3, z3, norm_g.reshape(1, SGU_WIDTH), w_s, b_s_t)


def _rope(x, cos, sin_signed):
    return x * cos + pltpu.roll(x, HEAD_DIM // 2, 1) * sin_signed


def _prep_kernel(bq_ref, bk_ref, bv_ref, cq_ref, ck_ref, cos_ref, sin_ref, qg_ref, kg_ref,
                 oq_ref, ok_ref, ov_ref, ocq_ref, ock_ref):
    cos = cos_ref[...]
    sin = sin_ref[...]
    scale = HEAD_DIM ** -0.5
    for h in range(ATTN_HEADS):
        cols = slice(h * HEAD_DIM, (h + 1) * HEAD_DIM)
        x = _rope(_rms(bq_ref[:, cols].astype(F32), qg_ref[...]), cos, sin) * (scale * LOG2_E)
        oq_ref[:, cols] = x.astype(BF16)
    for h in range(ATTN_KV_HEADS):
        cols = slice(h * HEAD_DIM, (h + 1) * HEAD_DIM)
        x = _rope(_rms(bk_ref[:, cols].astype(F32), kg_ref[...]), cos, sin)
        ok_ref[:, cols] = x.astype(BF16)
        ov_ref[:, 2 * h * HEAD_DIM:(2 * h + 1) * HEAD_DIM] = bv_ref[:, cols]
        ov_ref[:, (2 * h + 1) * HEAD_DIM:(2 * h + 2) * HEAD_DIM] = jnp.ones((bv_ref.shape[0], HEAD_DIM), BF16)
    for h in range(RET_HEADS):
        cols = slice(h * HEAD_DIM, (h + 1) * HEAD_DIM)
        ocq_ref[:, cols] = (_rope(cq_ref[:, cols].astype(F32), cos, sin) * scale).astype(BF16)
        ock_ref[:, cols] = _rope(ck_ref[:, cols].astype(F32), cos, sin).astype(BF16)


def _prep(z3, cos_t, sin_t, q_norm_g, k_norm_g, *, ts):
    b, s, _ = z3.shape
    ts = _tile(s, ts)
    n = s // ts

    def zspec(width, off):
        return pl.BlockSpec((None, ts, width), lambda bi, i: (bi, i, off // width))

    def ospec(width):
        return pl.BlockSpec((None, ts, width), lambda bi, i: (bi, i, 0))

    tab = pl.BlockSpec((ts, HEAD_DIM), lambda bi, i: (i, 0))
    gain = pl.BlockSpec((1, HEAD_DIM), lambda bi, i: (0, 0))
    widths = (ATTN_Q_W, ATTN_KV_W, 2 * ATTN_KV_W, RET_QK_W, RET_QK_W)
    return pl.pallas_call(
        _prep_kernel,
        grid=(b, n),
        in_specs=[zspec(ATTN_Q_W, OFF_B_Q), zspec(ATTN_KV_W, OFF_B_K), zspec(ATTN_KV_W, OFF_B_V),
                  zspec(RET_QK_W, OFF_C_Q), zspec(RET_QK_W, OFF_C_K), tab, tab, gain, gain],
        out_specs=[ospec(w) for w in widths],
        out_shape=[jax.ShapeDtypeStruct((b, s, w), BF16) for w in widths],
        compiler_params=_params("parallel", "parallel"),
        name="qk_prep",
    )(z3, z3, z3, z3, z3, cos_t, sin_t, q_norm_g.reshape(1, HEAD_DIM), k_norm_g.reshape(1, HEAD_DIM))


def _flash_kernel(b2_ref, q_ref, k_ref, v_ref, o_ref, qc_ref, acc_ref, m_ref, *, tq, tk):
    for r in range(ATTN_REP):
        qc_ref[r * tq:(r + 1) * tq, :] = q_ref[:, r * HEAD_DIM:(r + 1) * HEAD_DIM]
    b2 = b2_ref[pl.program_id(0)]
    n_kv = k_ref.shape[0] // tk
    acc_ref[...] = jnp.zeros(acc_ref.shape, F32)

    def tiles(j):
        off = pl.multiple_of(j * tk, tk)
        k = k_ref[pl.ds(off, tk), :]
        return lax.dot_general(qc_ref[...], k, _NT, preferred_element_type=F32), v_ref[pl.ds(off, tk), :]

    @pl.when(b2 <= FIXED_SHIFT_LIMIT)
    def _():
        def body(j, carry):
            s, v = tiles(j)
            acc_ref[...] += jnp.dot(jnp.exp2(s - b2).astype(BF16), v, preferred_element_type=F32)
            return carry

        lax.fori_loop(0, n_kv, body, 0, unroll=16 if n_kv % 16 == 0 else 1)

    @pl.when(b2 > FIXED_SHIFT_LIMIT)
    def _():
        m_ref[...] = jnp.full(m_ref.shape, -jnp.inf, F32)

        def body(j, carry):
            s, v = tiles(j)
            m_old = m_ref[...]
            m_new = jnp.maximum(m_old, jnp.max(s, axis=-1, keepdims=True))
            p = jnp.exp2(s - m_new).astype(BF16)
            acc_ref[...] = jnp.exp2(m_old - m_new) * acc_ref[...] + jnp.dot(p, v, preferred_element_type=F32)
            m_ref[...] = m_new
            return carry

        lax.fori_loop(0, n_kv, body, 0)

    out = acc_ref[:, :HEAD_DIM] / acc_ref[:, HEAD_DIM:]
    for r in range(ATTN_REP):
        o_ref[:, r * HEAD_DIM:(r + 1) * HEAD_DIM] = out[r * tq:(r + 1) * tq, :].astype(o_ref.dtype)


def _flash(qr, kr, vaug, bound2, *, tq, tk):
    b, s, _ = qr.shape
    tq, tk = _tile(s, tq), _tile(s, tk)
    gw = ATTN_REP * HEAD_DIM
    grid_spec = pltpu.PrefetchScalarGridSpec(
        num_scalar_prefetch=1,
        grid=(b, ATTN_KV_HEADS, s // tq),
        in_specs=[
            pl.BlockSpec((None, tq, gw), lambda bi, g, i, b2: (bi, i, g)),
            pl.BlockSpec((None, s, HEAD_DIM), lambda bi, g, i, b2: (bi, 0, g)),
            pl.BlockSpec((None, s, 2 * HEAD_DIM), lambda bi, g, i, b2: (bi, 0, g)),
        ],
        out_specs=pl.BlockSpec((None, tq, gw), lambda bi, g, i, b2: (bi, i, g)),
        scratch_shapes=[
            pltpu.VMEM((ATTN_REP * tq, HEAD_DIM), BF16),
            pltpu.VMEM((ATTN_REP * tq, 2 * HEAD_DIM), F32),
            pltpu.VMEM((ATTN_REP * tq, 1), F32),
        ],
    )
    return pl.pallas_call(
        functools.partial(_flash_kernel, tq=tq, tk=tk),
        grid_spec=grid_spec,
        out_shape=jax.ShapeDtypeStruct((b, s, ATTN_Q_W), BF16),
        compiler_params=_params("parallel", "parallel", "arbitrary"),
        name="flash_attention",
    )(bound2, qr, kr, vaug)


def _score_bound(q_norm_g, k_norm_g, batch):
    bound = LOG2_E * HEAD_DIM ** 0.5 * jnp.max(jnp.abs(q_norm_g)) * jnp.max(jnp.abs(k_norm_g)) * 1.01
    return jnp.full((batch,), bound, F32)


def _ret_head_views(refs, rows, h, width):
    half = RET_HEADS // 2
    return refs[h // half][rows, (h % half) * width:(h % half + 1) * width]


def _ret_head(chunks, h, q_ref, k_ref, v_refs, st, dm, qd, kd, cd):
    qk = slice(h * HEAD_DIM, (h + 1) * HEAD_DIM)
    intra, outer = {}, {}
    for c in chunks:
        rows = slice(c * CHUNK, (c + 1) * CHUNK)
        q, k, v = q_ref[rows, qk], k_ref[rows, qk], _ret_head_views(v_refs, rows, h, RET_V_DIM)
        s = lax.dot_general(q, k, _NT, preferred_element_type=F32) * dm
        intra[c] = jnp.dot(s.astype(BF16), v, preferred_element_type=F32)
        kt = (k.astype(F32).T * kd).astype(BF16)
        outer[c] = jnp.dot(kt, v, preferred_element_type=F32)
    out = {}
    for c in chunks:
        rows = slice(c * CHUNK, (c + 1) * CHUNK)
        out[c] = intra[c] + qd * jnp.dot(q_ref[rows, qk], st.astype(BF16), preferred_element_type=F32)
        st = cd * st + outer[c]
    return out, st


def _ret_fwd_kernel(q_ref, k_ref, v0_ref, v1_ref, dm_ref, qd_ref, kd_ref, cd_ref, o_ref, st_ref, *, n_chunks):
    @pl.when(pl.program_id(1) == 0)
    def _():
        st_ref[...] = jnp.zeros(st_ref.shape, F32)

    chunks = list(range(n_chunks))
    for h in range(RET_HEADS):
        out, st = _ret_head(chunks, h, q_ref, k_ref, (v0_ref, v1_ref), st_ref[h],
                            dm_ref[h], qd_ref[h], kd_ref[h], cd_ref[h])
        st_ref[h] = st
        for c in chunks:
            o_ref[c * CHUNK:(c + 1) * CHUNK, h * RET_V_DIM:(h + 1) * RET_V_DIM] = out[c]


def _ret_bwd_kernel(q_ref, k_ref, v0_ref, v1_ref, g0_ref, g1_ref, of_ref, dm_ref, qd_ref, kd_ref, cd_ref, ng_ref,
                    o_ref, st_ref, *, n_chunks):
    @pl.when(pl.program_id(1) == 0)
    def _():
        st_ref[...] = jnp.zeros(st_ref.shape, F32)

    chunks = list(reversed(range(n_chunks)))
    for h in range(RET_HEADS):
        vc = slice(h * RET_V_DIM, (h + 1) * RET_V_DIM)
        out, st = _ret_head(chunks, h, q_ref, k_ref, (v0_ref, v1_ref), st_ref[h],
                            dm_ref[h], qd_ref[h], kd_ref[h], cd_ref[h])
        st_ref[h] = st
        for c in chunks:
            rows = slice(c * CHUNK, (c + 1) * CHUNK)
            o = out[c] + of_ref[rows, vc]
            oc = o - jnp.mean(o, axis=-1, keepdims=True)
            o = oc * lax.rsqrt(jnp.mean(oc * oc, axis=-1, keepdims=True) + NORM_EPS) * ng_ref[:, vc]
            gate = _ret_head_views((g0_ref, g1_ref), rows, h, RET_V_DIM).astype(F32)
            o_ref[rows, vc] = (o * (gate * jax.nn.sigmoid(gate))).astype(o_ref.dtype)


def _retention(cqr, ckr, z3, tabs, norm_g, *, ts):
    b, s, _ = cqr.shape
    ts = _tile(s, ts)
    n = s // ts
    dv = RET_V_DIM
    half_w = RET_V_W // 2

    def seq(width, off, rev, part=0):
        if rev:
            return pl.BlockSpec((None, ts, width), lambda bi, i: (bi, n - 1 - i, off // width + part))
        return pl.BlockSpec((None, ts, width), lambda bi, i: (bi, i, off // width + part))

    def whole(*shape):
        return pl.BlockSpec(shape, lambda bi, i: (0,) * len(shape))

    tab_specs = [whole(RET_HEADS, CHUNK, CHUNK), whole(RET_HEADS, CHUNK, 1), whole(RET_HEADS, 1, CHUNK),
                 whole(RET_HEADS, 1, dv)]
    state = [pltpu.VMEM((RET_HEADS, HEAD_DIM, dv), F32)]
    sem = _params("parallel", "arbitrary")
    o_fwd = pl.pallas_call(
        functools.partial(_ret_fwd_kernel, n_chunks=ts // CHUNK),
        grid=(b, n),
        in_specs=[seq(RET_QK_W, 0, False), seq(RET_QK_W, 0, False), seq(half_w, OFF_C_V, False, 0),
                  seq(half_w, OFF_C_V, False, 1)] + tab_specs,
        out_specs=seq(RET_V_W, 0, False),
        out_shape=jax.ShapeDtypeStruct((b, s, RET_V_W), F32),
        scratch_shapes=state,
        compiler_params=sem,
        name="retention_fwd",
    )(cqr, ckr, z3, z3, *tabs[0])
    return pl.pallas_call(
        functools.partial(_ret_bwd_kernel, n_chunks=ts // CHUNK),
        grid=(b, n),
        in_specs=[seq(RET_QK_W, 0, True), seq(RET_QK_W, 0, True), seq(half_w, OFF_C_V, True, 0),
                  seq(half_w, OFF_C_V, True, 1), seq(half_w, OFF_C_G, True, 0), seq(half_w, OFF_C_G, True, 1),
                  seq(RET_V_W, 0, True)] + tab_specs + [whole(1, RET_V_W)],
        out_specs=seq(RET_V_W, 0, True),
        out_shape=jax.ShapeDtypeStruct((b, s, RET_V_W), BF16),
        scratch_shapes=state,
        compiler_params=sem,
        name="retention_bwd",
    )(cqr, ckr, z3, z3, z3, z3, o_fwd, *tabs[1], norm_g.reshape(1, RET_V_W))


def _retention_tables(ret_decay):
    lg = -jnp.exp(ret_decay.astype(F32))
    idx = jnp.arange(CHUNK, dtype=F32)
    diff = idx[:, None] - idx[None, :]
    out = []
    for d, (mask, delta, qpow, kpow) in enumerate((
            (diff >= 0, diff, idx + 1.0, CHUNK - 1.0 - idx),
            (diff < 0, -diff, CHUNK - idx, idx))):
        l = lg[d][:, None, None]
        dm = jnp.where(mask, jnp.exp(l * jnp.where(mask, delta, 0.0)), 0.0)
        qd = jnp.exp(lg[d][:, None] * qpow)[:, :, None]
        kd = jnp.exp(lg[d][:, None] * kpow)[:, None, :]
        cd = jnp.broadcast_to(jnp.exp(lg[d] * CHUNK)[:, None, None], (RET_HEADS, 1, RET_V_DIM))
        out.append((dm, qd, kd, cd))
    return out


def _merge_kernel(a_ref, b_ref, c_ref, wa_ref, wb_ref, wc_ref, ga_ref, gb_ref, gc_ref, o_ref):
    acc = jax.nn.sigmoid(ga_ref[...].astype(F32)) * jnp.dot(a_ref[...], wa_ref[...], preferred_element_type=F32)
    acc = acc + jax.nn.sigmoid(gb_ref[...].astype(F32)) * jnp.dot(b_ref[...], wb_ref[...], preferred_element_type=F32)
    acc = acc + jax.nn.sigmoid(gc_ref[...].astype(F32)) * jnp.dot(c_ref[...], wc_ref[...], preferred_element_type=F32)
    o_ref[...] = acc.astype(o_ref.dtype)


def _merge(out_a, out_b, out_c, w_a, w_b, w_c, z, *, tm, tn):
    t, k = out_a.shape
    tm, tn = _tile(t, tm), _tile(D_MODEL, tn)
    lhs = pl.BlockSpec((tm, k), lambda i, j: (i, 0))
    rhs = pl.BlockSpec((k, tn), lambda i, j: (0, j))

    def gate(off):
        return pl.BlockSpec((tm, tn), lambda i, j: (i, off // tn + j))

    return pl.pallas_call(
        _merge_kernel,
        grid=(t // tm, D_MODEL // tn),
        in_specs=[lhs, lhs, lhs, rhs, rhs, rhs, gate(OFF_GATE_A), gate(OFF_GATE_B), gate(OFF_GATE_C)],
        out_specs=pl.BlockSpec((tm, tn), lambda i, j: (i, j)),
        out_shape=jax.ShapeDtypeStruct((t, D_MODEL), BF16),
        compiler_params=_params("parallel", "parallel"),
        name="branch_merge",
    )(out_a, out_b, out_c, w_a, w_b, w_c, z, z, z)


def _matmul_res_kernel(a_ref, w_ref, x_ref, o_ref):
    o_ref[...] = x_ref[...] + jnp.dot(a_ref[...], w_ref[...], preferred_element_type=F32)


def _matmul_res(a, w, x, *, tm, tn):
    t, k = a.shape
    n = w.shape[1]
    tm, tn = _tile(t, tm), _tile(n, tn)
    return pl.pallas_call(
        _matmul_res_kernel,
        grid=(t // tm, n // tn),
        in_specs=[
            pl.BlockSpec((tm, k), lambda i, j: (i, 0)),
            pl.BlockSpec((k, tn), lambda i, j: (0, j)),
            pl.BlockSpec((tm, tn), lambda i, j: (i, j)),
        ],
        out_specs=pl.BlockSpec((tm, tn), lambda i, j: (i, j)),
        out_shape=jax.ShapeDtypeStruct((t, n), F32),
        compiler_params=_params("parallel", "parallel"),
        name="out_proj_residual",
    )(a, w, x)


def _merge_out_kernel(a_ref, b_ref, c_ref, wa_ref, wb_ref, wc_ref, ga_ref, gb_ref, gc_ref, wo_ref, x_ref,
                      o_ref, m_ref, *, n_blocks):
    j = pl.program_id(1)

    @pl.when(j < n_blocks)
    def _():
        acc = jax.nn.sigmoid(ga_ref[...].astype(F32)) * jnp.dot(a_ref[...], wa_ref[...], preferred_element_type=F32)
        acc = acc + jax.nn.sigmoid(gb_ref[...].astype(F32)) * jnp.dot(b_ref[...], wb_ref[...],
                                                                      preferred_element_type=F32)
        acc = acc + jax.nn.sigmoid(gc_ref[...].astype(F32)) * jnp.dot(c_ref[...], wc_ref[...],
                                                                      preferred_element_type=F32)
        m_ref[j] = acc.astype(BF16)

    @pl.when(j >= n_blocks)
    def _():
        tn = m_ref.shape[2]
        y = x_ref[...]
        for k in range(n_blocks):
            y = y + jnp.dot(m_ref[k], wo_ref[k * tn:(k + 1) * tn, :], preferred_element_type=F32)
        o_ref[...] = y


def _merge_out(out_a, out_b, out_c, w_a, w_b, w_c, z, w_out, x, *, tm, tn):
    t, k = out_a.shape
    d = x.shape[1]
    tm, tn = _tile(t, tm), _tile(d, tn)
    nb = d // tn
    lhs = pl.BlockSpec((tm, k), lambda i, j: (i, 0))
    rhs = pl.BlockSpec((k, tn), lambda i, j: (0, jnp.minimum(j, nb - 1)))

    def gate(off):
        return pl.BlockSpec((tm, tn), lambda i, j: (i, off // tn + jnp.minimum(j, nb - 1)))

    def late(rows):
        return pl.BlockSpec((rows, tn), lambda i, j: (i if rows == tm else 0, jnp.maximum(j - nb, 0)))

    return pl.pallas_call(
        functools.partial(_merge_out_kernel, n_blocks=nb),
        grid=(t // tm, 2 * nb),
        in_specs=[lhs, lhs, lhs, rhs, rhs, rhs, gate(OFF_GATE_A), gate(OFF_GATE_B), gate(OFF_GATE_C),
                  late(d), late(tm)],
        out_specs=late(tm),
        out_shape=jax.ShapeDtypeStruct((t, d), F32),
        scratch_shapes=[pltpu.VMEM((nb, tm, tn), BF16)],
        compiler_params=_params("parallel", "arbitrary"),
        name="merge_out_proj",
    )(out_a, out_b, out_c, w_a, w_b, w_c, z, z, z, w_out, x)


def _route(h, wr, br):
    h_hi = h.astype(BF16)
    h_lo = (h - h_hi.astype(F32)).astype(BF16)
    both = lax.dot_general(wr, h_hi, _NT, preferred_element_type=F32)
    lt = (both[:ROUTER_ROWS] + both[ROUTER_ROWS:]
          + lax.dot_general(wr[:ROUTER_ROWS], h_lo, _NT, preferred_element_type=F32) + br)
    gl = [lt[i:i + 1, :] for i in range(N_GROUPS)]
    gmax = jnp.maximum(jnp.maximum(gl[0], gl[1]), jnp.maximum(gl[2], gl[3]))
    denom = sum(jnp.exp(v - gmax) for v in gl)
    g_prob = 1.0 / denom
    g_idx = jnp.where(gl[0] == gmax, 0, jnp.where(gl[1] == gmax, 1, jnp.where(gl[2] == gmax, 2, 3)))
    sel = []
    for j in range(EXPERTS_PER_GROUP):
        rows = [lt[N_GROUPS + g * EXPERTS_PER_GROUP + j:N_GROUPS + g * EXPERTS_PER_GROUP + j + 1, :]
                for g in range(N_GROUPS)]
        sel.append(jnp.where(g_idx == 0, rows[0], jnp.where(g_idx == 1, rows[1],
                                                            jnp.where(g_idx == 2, rows[2], rows[3]))))
    top1 = jnp.maximum(jnp.maximum(sel[0], sel[1]), jnp.maximum(sel[2], sel[3]))
    i1 = jnp.where(sel[0] == top1, 0, jnp.where(sel[1] == top1, 1, jnp.where(sel[2] == top1, 2, 3)))
    rest = [jnp.where(i1 == j, -jnp.inf, sel[j]) for j in range(EXPERTS_PER_GROUP)]
    top2 = jnp.maximum(jnp.maximum(rest[0], rest[1]), jnp.maximum(rest[2], rest[3]))
    i2 = jnp.where(rest[0] == top2, 0, jnp.where(rest[1] == top2, 1, jnp.where(rest[2] == top2, 2, 3)))
    e2 = jnp.exp(top2 - top1)
    w1 = g_prob / (1.0 + e2)
    w2 = g_prob * e2 / (1.0 + e2)
    row = lax.broadcasted_iota(jnp.int32, (ROUTE_ROWS, h.shape[0]), 0)
    return (jnp.where(row == i1, w1, 0.0) + jnp.where(row == i2, w2, 0.0)
            + jnp.where(row == EXPERTS_PER_GROUP, g_idx.astype(F32), 0.0))


def _xattn_kernel(x_ref, g_ref, wq_ref, kv_ref, wo_ref, fg_ref, wr_ref, br_ref, o_ref, rt_ref):
    x = x_ref[...]
    h = _rms(x, g_ref[...]).astype(BF16)
    q = (jnp.dot(h, wq_ref[...], preferred_element_type=F32) * (HEAD_DIM ** -0.5)).astype(BF16)
    heads = []
    for hh in range(XATTN_HEADS):
        cols = slice(hh * HEAD_DIM, (hh + 1) * HEAD_DIM)
        k = kv_ref[:, cols]
        v = kv_ref[:, XATTN_W + hh * HEAD_DIM:XATTN_W + (hh + 1) * HEAD_DIM]
        s = lax.dot_general(q[:, cols], k, _NT, preferred_element_type=F32)
        p = jnp.exp(s - jnp.max(s, axis=-1, keepdims=True))
        o = jnp.dot(p.astype(BF16), v, preferred_element_type=F32) / jnp.sum(p, axis=-1, keepdims=True)
        heads.append(o.astype(BF16))
    y = x + jnp.dot(jnp.concatenate(heads, axis=1), wo_ref[...], preferred_element_type=F32)
    o_ref[...] = y
    rt_ref[...] = _route(_rms(y, fg_ref[...]), wr_ref[...], br_ref[...])


def _xattn(x3, norm_g, wq, kv3, wo, ffn_norm_g, wr_t, br, *, tm):
    b, s, d = x3.shape
    m = kv3.shape[1]
    tm = _tile(s, tm)
    n = s // tm

    def whole(*shape):
        return pl.BlockSpec(shape, lambda bi, i: (0,) * len(shape))

    return pl.pallas_call(
        _xattn_kernel,
        grid=(b, n),
        in_specs=[
            pl.BlockSpec((None, tm, d), lambda bi, i: (bi, i, 0)),
            whole(1, d),
            whole(d, XATTN_W),
            pl.BlockSpec((None, m, 2 * XATTN_W), lambda bi, i: (bi, 0, 0)),
            whole(XATTN_W, d),
            whole(1, d),
            whole(2 * ROUTER_ROWS, d),
            whole(ROUTER_ROWS, 1),
        ],
        out_specs=[pl.BlockSpec((None, tm, d), lambda bi, i: (bi, i, 0)),
                   pl.BlockSpec((ROUTE_ROWS, tm), lambda bi, i: (0, bi * n + i))],
        out_shape=[jax.ShapeDtypeStruct((b, s, d), F32), jax.ShapeDtypeStruct((ROUTE_ROWS, b * s), F32)],
        compiler_params=_params("parallel", "parallel"),
        name="memory_cross_attention",
    )(x3, norm_g.reshape(1, d), wq, kv3, wo, ffn_norm_g.reshape(1, d), wr_t, br)


def _row_gather_kernel(idx_ref, src_ref, dst_ref, sem, *, rows):
    def start(r, carry):
        pltpu.make_async_copy(src_ref.at[pl.ds(idx_ref[0, r], 1), :], dst_ref.at[pl.ds(r, 1), :], sem).start()
        return carry

    lax.fori_loop(0, rows, start, 0, unroll=8)
    pltpu.make_async_copy(src_ref.at[pl.ds(0, rows), :], dst_ref, sem).wait()


def _row_gather(src, idx, *, rows):
    n = idx.shape[0]
    d = src.shape[1]
    rows = _tile(n, rows)
    return pl.pallas_call(
        functools.partial(_row_gather_kernel, rows=rows),
        grid=(n // rows,),
        in_specs=[
            pl.BlockSpec((None, 1, rows), lambda i: (i, 0, 0), memory_space=pltpu.SMEM),
            pl.BlockSpec(memory_space=pl.ANY),
        ],
        out_specs=pl.BlockSpec((rows, d), lambda i: (i, 0)),
        out_shape=jax.ShapeDtypeStruct((n, d), src.dtype),
        scratch_shapes=[pltpu.SemaphoreType.DMA(())],
        compiler_params=_params("arbitrary"),
        name="row_gather",
    )(idx.reshape(n // rows, 1, rows), src)


def _moe_kernel(tg_ref, nu_ref, x_ref, w_ref, ng_ref, wg_ref, wu_ref, wd_ref, fg_ref, o_ref, h_ref, acc_ref, *,
                final_norm):
    i = pl.program_id(0)
    j = pl.program_id(1)
    used = i < nu_ref[0]

    @pl.when(j == 0)
    def _():
        acc_ref[...] = jnp.zeros(acc_ref.shape, F32)
        h_ref[...] = _rms(x_ref[...], ng_ref[...]).astype(BF16)

    @pl.when(used)
    def _():
        h = h_ref[...]
        a = jnp.dot(h, wg_ref[...], preferred_element_type=F32)
        hid = (a * jax.nn.sigmoid(a)) * jnp.dot(h, wu_ref[...], preferred_element_type=F32)
        w = w_ref[...]
        lane = lax.broadcasted_iota(jnp.int32, w.shape, 1)
        wj = jnp.sum(jnp.where(lane == j, w, 0.0), axis=-1, keepdims=True)
        acc_ref[...] += jnp.dot(hid.astype(BF16), wd_ref[...], preferred_element_type=F32) * wj

    @pl.when(j == pl.num_programs(1) - 1)
    def _():
        y = x_ref[...] + acc_ref[...]
        if final_norm:
            y = _rms(y, fg_ref[...])
        o_ref[...] = y


def _moe_grouped(xp, wp, tile_group, n_used, norm_g, w_gate, w_up, w_down, final_g, *, tm, final_norm):
    p, d = xp.shape

    def expert(i, j, tg, nu):
        return (tg[i] * EXPERTS_PER_GROUP + j, 0, 0)

    grid_spec = pltpu.PrefetchScalarGridSpec(
        num_scalar_prefetch=2,
        grid=(p // tm, EXPERTS_PER_GROUP),
        in_specs=[
            pl.BlockSpec((tm, d), lambda i, j, tg, nu: (i, 0)),
            pl.BlockSpec((tm, EXPERTS_PER_GROUP), lambda i, j, tg, nu: (i, 0)),
            pl.BlockSpec((1, d), lambda i, j, tg, nu: (0, 0)),
            pl.BlockSpec((None, d, EXPERT_FF), expert),
            pl.BlockSpec((None, d, EXPERT_FF), expert),
            pl.BlockSpec((None, EXPERT_FF, d), expert),
            pl.BlockSpec((1, d), lambda i, j, tg, nu: (0, 0)),
        ],
        out_specs=pl.BlockSpec((tm, d), lambda i, j, tg, nu: (i, 0)),
        scratch_shapes=[pltpu.VMEM((tm, d), BF16), pltpu.VMEM((tm, d), F32)],
    )
    return pl.pallas_call(
        functools.partial(_moe_kernel, final_norm=final_norm),
        grid_spec=grid_spec,
        out_shape=jax.ShapeDtypeStruct((p, d), F32),
        compiler_params=_params("parallel", "arbitrary"),
        name="experts",
    )(tile_group, n_used, xp, wp, norm_g.reshape(1, d), w_gate, w_up, w_down, final_g.reshape(1, d))


def _group_plan(route, tm):
    t = route.shape[1]
    gid = route[EXPERTS_PER_GROUP].astype(jnp.int32)
    onehot = (gid[:, None] == jnp.arange(N_GROUPS, dtype=jnp.int32)[None, :]).astype(jnp.int32)
    csum = jnp.cumsum(onehot, axis=0)
    count = csum[-1]
    rank = jnp.take_along_axis(csum, gid[:, None], axis=1)[:, 0] - 1
    padded = (count + tm - 1) // tm * tm
    end = jnp.cumsum(padded)
    pos = (end - padded)[gid] + rank
    n_tiles = t // tm + N_GROUPS
    src = jnp.zeros((n_tiles * tm,), jnp.int32).at[pos].set(jnp.arange(t, dtype=jnp.int32))
    tile_start = jnp.arange(n_tiles, dtype=jnp.int32) * tm
    tile_group = jnp.minimum(jnp.sum((tile_start[:, None] >= end[None, :]).astype(jnp.int32), axis=1), N_GROUPS - 1)
    n_used = (end[-1] // tm).reshape(1)
    return pos, src, tile_group, n_used


def _moe(x, route, norm_g, w_gate, w_up, w_down, final_g, *, tm, final_norm):
    pos, src, tile_group, n_used = _group_plan(route, tm)
    xp = _row_gather(x, src, rows=512)
    wp = route[:EXPERTS_PER_GROUP].T[src]
    yp = _moe_grouped(xp, wp, tile_group, n_used, norm_g, w_gate, w_up, w_down, final_g, tm=tm,
                      final_norm=final_norm)
    return _row_gather(yp, pos, rows=512)


def _rope_tables(seq_len):
    rows = seq_len // GRID_W
    row = jnp.repeat(jnp.arange(rows, dtype=F32), GRID_W)
    col = jnp.tile(jnp.arange(GRID_W, dtype=F32), rows)
    n_freq = HEAD_DIM // 4
    inv = ROPE_THETA ** (-jnp.arange(n_freq, dtype=F32) / n_freq)
    ar = row[:, None] * inv
    ac = col[:, None] * inv
    cos = jnp.concatenate([jnp.cos(ar), jnp.cos(ac), jnp.cos(ar), jnp.cos(ac)], axis=1)
    sin = jnp.concatenate([-jnp.sin(ar), -jnp.sin(ac), jnp.sin(ar), jnp.sin(ac)], axis=1)
    return cos, sin


def _pair_major(a):
    lead = a.shape[:-1]
    heads = a.shape[-1] // HEAD_DIM
    a = a.reshape(lead + (heads, 2, 2, HEAD_DIM // 4))
    return jnp.swapaxes(a, -3, -2).reshape(lead + (heads * HEAD_DIM,))


def _rotary_layout(w_in):
    parts = [w_in[:, :OFF_B_Q], _pair_major(w_in[:, OFF_B_Q:OFF_B_V]), w_in[:, OFF_B_V:OFF_C_Q],
             _pair_major(w_in[:, OFF_C_Q:OFF_C_V]), w_in[:, OFF_C_V:]]
    return jnp.concatenate(parts, axis=1)


def _prepare_layer(l, p):
    wr_t = jnp.concatenate([p["router_group_w"][l].T, p["router_expert_w"][l].T,
                            jnp.zeros((ROUTER_ROWS - N_GROUPS - N_EXPERTS, D_MODEL), F32)], axis=0)
    wr_hi = wr_t.astype(BF16)
    wr_t = jnp.concatenate([wr_hi, (wr_t - wr_hi.astype(F32)).astype(BF16)], axis=0)
    br = jnp.concatenate([p["router_group_b"][l], p["router_expert_b"][l].reshape(-1),
                          jnp.zeros((ROUTER_ROWS - N_GROUPS - N_EXPERTS,), F32)]).reshape(ROUTER_ROWS, 1)
    return dict(
        norm_mix_g=p["norm_mix_g"][l], w_in=_rotary_layout(p["w_in"][l].astype(BF16)),
        sgu_norm_g=p["sgu_norm_g"][l], sgu_w=p["sgu_w"][l].astype(BF16), sgu_b_t=p["sgu_b"][l].T,
        q_norm_g=_pair_major(p["attn_q_norm_g"][l]), k_norm_g=_pair_major(p["attn_k_norm_g"][l]),
        ret_tabs=_retention_tables(p["ret_decay"][l]), ret_norm_g=p["ret_norm_g"][l],
        w_a=p["w_branch_a"][l].astype(BF16), w_b=p["w_branch_b"][l].astype(BF16),
        w_c=p["w_branch_c"][l].astype(BF16), w_out=p["w_out"][l].astype(BF16),
        xattn_norm_g=p["xattn_norm_g"][l], mem_norm_g=p["mem_norm_g"][l],
        wq=p["xattn_wq"][l].astype(BF16), wkv=p["xattn_wkv"][l].astype(BF16), wo=p["xattn_wo"][l].astype(BF16),
        ffn_norm_g=p["ffn_norm_g"][l], wr_t=wr_t, br=br,
        w_gate=p["expert_w_gate"][l].astype(BF16), w_up=p["expert_w_up"][l].astype(BF16),
        w_down=p["expert_w_down"][l].astype(BF16),
    )


def _trunk(x, mem, layers, final_norm_g):
    b, s, d = x.shape
    t = b * s
    m = mem.shape[1]
    cos_t, sin_t = _rope_tables(s)
    xt = x.reshape(t, d)
    memt = mem.reshape(b * m, d)
    for l, w in enumerate(layers):
        z = _rms_matmul(xt, w["norm_mix_g"], w["w_in"], tm=1024, tn=1280)
        z3 = z.reshape(b, s, IN_WIDTH)
        out_a = _sgu(z3, w["sgu_norm_g"], w["sgu_w"], w["sgu_b_t"], ts=512)
        qr, kr, vaug, cqr, ckr = _prep(z3, cos_t, sin_t, w["q_norm_g"], w["k_norm_g"], ts=512)
        out_b = _flash(qr, kr, vaug, _score_bound(w["q_norm_g"], w["k_norm_g"], b), tq=512, tk=256)
        out_c = _retention(cqr, ckr, z3, w["ret_tabs"], w["ret_norm_g"], ts=512)
        xt = _merge_out(out_a.reshape(t, SGU_WIDTH), out_b.reshape(t, ATTN_Q_W), out_c.reshape(t, RET_V_W),
                        w["w_a"], w["w_b"], w["w_c"], z, w["w_out"], xt, tm=1024, tn=512)
        kv = _rms_matmul(memt, w["mem_norm_g"], w["wkv"], tm=256, tn=512)
        x3, route = _xattn(xt.reshape(b, s, d), w["xattn_norm_g"], w["wq"], kv.reshape(b, m, 2 * XATTN_W), w["wo"],
                           w["ffn_norm_g"], w["wr_t"], w["br"], tm=512)
        xt = x3.reshape(t, d)
        xt = _moe(xt, route, w["ffn_norm_g"], w["w_gate"], w["w_up"], w["w_down"], final_norm_g, tm=512,
                  final_norm=(l == len(layers) - 1))
    return xt.reshape(b, s, d)


def kernel(x_prompt, x_sample, mem_prompt, mem_sample, norm_mix_g, w_in, sgu_norm_g, sgu_w, sgu_b, attn_q_norm_g, attn_k_norm_g, ret_decay, ret_norm_g, w_branch_a, w_branch_b, w_branch_c, w_out, xattn_norm_g, mem_norm_g, xattn_wq, xattn_wkv, xattn_wo, ffn_norm_g, router_group_w, router_group_b, router_expert_w, router_expert_b, expert_w_gate, expert_w_up, expert_w_down, final_norm_g):
    p = dict(norm_mix_g=norm_mix_g, w_in=w_in, sgu_norm_g=sgu_norm_g, sgu_w=sgu_w, sgu_b=sgu_b,
             attn_q_norm_g=attn_q_norm_g, attn_k_norm_g=attn_k_norm_g, ret_decay=ret_decay, ret_norm_g=ret_norm_g,
             w_branch_a=w_branch_a, w_branch_b=w_branch_b, w_branch_c=w_branch_c, w_out=w_out,
             xattn_norm_g=xattn_norm_g, mem_norm_g=mem_norm_g, xattn_wq=xattn_wq, xattn_wkv=xattn_wkv,
             xattn_wo=xattn_wo, ffn_norm_g=ffn_norm_g, router_group_w=router_group_w, router_group_b=router_group_b,
             router_expert_w=router_expert_w, router_expert_b=router_expert_b, expert_w_gate=expert_w_gate,
             expert_w_up=expert_w_up, expert_w_down=expert_w_down)
    layers = [_prepare_layer(l, p) for l in range(DEPTH)]
    y_prompt = _trunk(x_prompt, mem_prompt, layers, final_norm_g)
    y_sample = _trunk(x_sample, mem_sample, layers, final_norm_g)
    return (y_prompt, y_sample)
```
